```python
import jax, jax.numpy as jnp
from jax import lax
import numpy as np

D_MODEL = 2048
BATCH = 4
SEQ = 2048
DEPTH = 1

HEAD_DIM = 64
SWA_HEADS = 16
SWA_KV_HEADS = 4
SWA_GROUP = SWA_HEADS // SWA_KV_HEADS
WINDOW = 128
DSA_HEADS = 16
IDX_HEADS = 16
IDX_DIM = 64
DSA_TOPK_MAX = 256
DSA_BLOCK = 128
SWA_Q_WIDTH = SWA_HEADS * HEAD_DIM
SWA_KV_WIDTH = SWA_KV_HEADS * HEAD_DIM
DSA_Q_WIDTH = DSA_HEADS * HEAD_DIM
IDX_Q_WIDTH = IDX_HEADS * IDX_DIM
IN_WIDTH = (SWA_Q_WIDTH + 2 * SWA_KV_WIDTH + DSA_Q_WIDTH + 2 * HEAD_DIM
            + IDX_Q_WIDTH + IDX_DIM + IDX_HEADS + 2 * D_MODEL)
N_ALIBI_HEADS = SWA_HEADS + DSA_HEADS
N_GROUPS = 4
EXPERTS_PER_GROUP = 8
N_EXPERTS = N_GROUPS * EXPERTS_PER_GROUP
EXPERT_TOP_K = 2
D_EXPERT = 512
RMS_EPS = 1e-6
NEG_INF = -1e30

kernel_name = "hybrid_swa_dsa_hmoe_block"


def rms_norm(x, g):
    xf = x.astype(jnp.float32)
    y = xf * lax.rsqrt(jnp.mean(xf * xf, axis=-1, keepdims=True) + RMS_EPS)
    return (y * g.astype(jnp.float32)).astype(x.dtype)


def alibi_slopes():
    i = jnp.arange(1, N_ALIBI_HEADS + 1, dtype=jnp.float32)
    return jnp.exp2(-8.0 * i / N_ALIBI_HEADS)


def swa_sink_attention(q, k, v, sinks, slopes, q_gain, k_gain):
    B, S = q.shape[0], q.shape[1]
    nb = S // WINDOW
    q = rms_norm(q, q_gain)
    k = rms_norm(k, k_gain)
    qb = q.reshape(B, nb, WINDOW, SWA_KV_HEADS, SWA_GROUP, HEAD_DIM)
    pad = jnp.zeros((B, WINDOW, SWA_KV_HEADS, HEAD_DIM), k.dtype)
    kp = jnp.concatenate([pad, k], axis=1)[:, :S]
    vp = jnp.concatenate([pad, v], axis=1)[:, :S]
    blk = (B, nb, WINDOW, SWA_KV_HEADS, HEAD_DIM)
    kw = jnp.concatenate([kp.reshape(blk), k.reshape(blk)], axis=2)
    vw = jnp.concatenate([vp.reshape(blk), v.reshape(blk)], axis=2)
    scale = HEAD_DIM ** -0.5
    s = jnp.einsum('bnqkgd,bnskd->bnkgqs', qb, kw).astype(jnp.float32) * scale
    rel = (jnp.arange(WINDOW)[:, None] + WINDOW - jnp.arange(2 * WINDOW)[None, :])
    kpos = jnp.arange(nb)[:, None] * WINDOW - WINDOW + jnp.arange(2 * WINDOW)[None, :]
    valid = ((rel >= 0) & (rel < WINDOW))[None] & (kpos >= 0)[:, None, :]
    sl = slopes.reshape(SWA_KV_HEADS, SWA_GROUP)[:, :, None, None]
    s = s - sl * rel.astype(jnp.float32)
    s = jnp.where(valid[None, :, None, None], s, NEG_INF)
    sink = sinks.astype(jnp.float32).reshape(SWA_KV_HEADS, SWA_GROUP)[None, None, :, :, None, None]
    m = jnp.maximum(jnp.max(s, axis=-1, keepdims=True), sink)
    e = jnp.exp(s - m)
    p = (e / (jnp.sum(e, axis=-1, keepdims=True) + jnp.exp(sink - m))).astype(v.dtype)
    o = jnp.einsum('bnkgqs,bnskd->bnqkgd', p, vw)
    return o.reshape(B, S, SWA_Q_WIDTH)


def dsa_attention(q, k, v, qi, ki, wi, slopes, q_gain, k_gain):
    B, S = q.shape[0], q.shape[1]
    topk = min(DSA_TOPK_MAX, S // 4)
    nb = S // DSA_BLOCK
    q = rms_norm(q, q_gain)
    k = rms_norm(k, k_gain)
    kif = ki.astype(jnp.float32)
    kpos = jnp.arange(S)
    scale = HEAD_DIM ** -0.5
    sl = slopes.astype(jnp.float32)[None, :, None, None]

    def to_blocks(a):
        return a.reshape((B, nb, DSA_BLOCK) + a.shape[2:]).swapaxes(0, 1)

    def block_fn(args):
        qb, qib, wib, start = args
        t = start + jnp.arange(DSA_BLOCK)
        logits = jnp.einsum('bqhd,bsd->bqhs', qib.astype(jnp.float32), kif) * IDX_DIM ** -0.5
        w = wib.astype(jnp.float32) * IDX_HEADS ** -0.5
        score = jnp.einsum('bqhs,bqh->bqs', jax.nn.relu(logits), w)
        causal = kpos[None, :] <= t[:, None]
        score = jnp.where(causal[None], score, -jnp.inf)
        _, idx = lax.top_k(score, topk)
        kg = jax.vmap(lambda kk, ii: kk[ii])(k, idx)
        vg = jax.vmap(lambda vv, ii: vv[ii])(v, idx)
        dist = t[None, :, None] - idx
        s = jnp.einsum('bqhd,bqkd->bhqk', qb, kg).astype(jnp.float32) * scale
        s = s - sl * dist[:, None].astype(jnp.float32)
        s = jnp.where((dist >= 0)[:, None], s, NEG_INF)
        p = jax.nn.softmax(s, axis=-1).astype(vg.dtype)
        return jnp.einsum('bhqk,bqkd->bqhd', p, vg)

    starts = jnp.arange(nb) * DSA_BLOCK
    o = lax.map(block_fn, (to_blocks(q), to_blocks(qi), to_blocks(wi), starts))
    return o.swapaxes(0, 1).reshape(B, S, DSA_Q_WIDTH)


def hybrid_mixer(h, slopes, w_in, swa_q_norm, swa_k_norm, swa_sinks, dsa_q_norm, dsa_k_norm,
                 w_branch_a, w_branch_b, w_out):
    B, S, _ = h.shape
    proj = h @ w_in
    sizes = [SWA_Q_WIDTH, SWA_KV_WIDTH, SWA_KV_WIDTH, DSA_Q_WIDTH, HEAD_DIM, HEAD_DIM,
             IDX_Q_WIDTH, IDX_DIM, IDX_HEADS, D_MODEL, D_MODEL]
    offs = np.cumsum(sizes)[:-1].tolist()
    qa, ka, va, qb, kb, vb, qi, ki, wi, ga, gb = jnp.split(proj, offs, axis=-1)
    oa = swa_sink_attention(qa.reshape(B, S, SWA_HEADS, HEAD_DIM),
                            ka.reshape(B, S, SWA_KV_HEADS, HEAD_DIM),
                            va.reshape(B, S, SWA_KV_HEADS, HEAD_DIM),
                            swa_sinks, slopes[:SWA_HEADS], swa_q_norm, swa_k_norm)
    ob = dsa_attention(qb.reshape(B, S, DSA_HEADS, HEAD_DIM), kb, vb,
                       qi.reshape(B, S, IDX_HEADS, IDX_DIM), ki, wi,
                       slopes[SWA_HEADS:], dsa_q_norm, dsa_k_norm)
    merged = jax.nn.sigmoid(ga) * (oa @ w_branch_a) + jax.nn.sigmoid(gb) * (ob @ w_branch_b)
    return merged @ w_out


def hier_moe(h, w_group, b_group, w_expert, b_expert, w_gate_up, w_up, w_down):
    B, S, D = h.shape
    hf = h.reshape(B * S, D)
    g_logits = (hf @ w_group).astype(jnp.float32) + b_group.astype(jnp.float32)
    g_prob = jax.nn.softmax(g_logits, axis=-1)
    g_idx = jnp.argmax(g_logits, axis=-1)
    p_g = jnp.take_along_axis(g_prob, g_idx[:, None], axis=1)[:, 0]
    e_logits = ((hf @ w_expert).astype(jnp.float32) + b_expert.astype(jnp.float32)
                ).reshape(-1, N_GROUPS, EXPERTS_PER_GROUP)
    e_sel = jnp.take_along_axis(e_logits, g_idx[:, None, None], axis=1)[:, 0]
    e_prob = jax.nn.softmax(e_sel, axis=-1)
    top_p, top_i = lax.top_k(e_prob, EXPERT_TOP_K)
    top_p = top_p / jnp.sum(top_p, axis=-1, keepdims=True)
    local = jnp.sum(jax.nn.one_hot(top_i, EXPERTS_PER_GROUP, dtype=jnp.float32) * top_p[..., None], axis=1)
    comb = (jax.nn.one_hot(g_idx, N_GROUPS, dtype=jnp.float32)[:, :, None]
            * (p_g[:, None] * local)[:, None, :]).astype(h.dtype)
    out = jnp.zeros_like(hf)
    for g in range(N_GROUPS):
        a = jnp.einsum('nd,edf->nef', hf, w_gate_up[g])
        u = jnp.einsum('nd,edf->nef', hf, w_up[g])
        act = jax.nn.silu(a) * u * comb[:, g, :, None]
        out = out + jnp.einsum('nef,efd->nd', act, w_down[g])
    return out.reshape(B, S, D)


def setup_inputs(seed: int = 0) -> dict:
    key = jax.random.key(seed)
    ks = jax.random.split(key, 24)
    f32 = jnp.float32
    L, D, F = DEPTH, D_MODEL, D_EXPERT

    def nrm(k, shape, scale):
        return jax.random.normal(k, shape, f32) * scale

    return {
        "x": nrm(ks[0], (BATCH, SEQ, D), 1.0),
        "c": nrm(ks[1], (BATCH, D), 1.0),
        "w_ada": nrm(ks[2], (L, D, 6 * D), 0.5 * D ** -0.5),
        "b_ada": nrm(ks[3], (L, 6 * D), 0.01),
        "ln1_g": 1.0 + nrm(ks[4], (L, D), 0.02),
        "w_in": nrm(ks[5], (L, D, IN_WIDTH), D ** -0.5),
        "swa_q_norm": 1.0 + nrm(ks[6], (L, HEAD_DIM), 0.02),
        "swa_k_norm": 1.0 + nrm(ks[7], (L, HEAD_DIM), 0.02),
        "swa_sinks": nrm(ks[8], (L, SWA_HEADS), 0.5),
        "dsa_q_norm": 1.0 + nrm(ks[9], (L, HEAD_DIM), 0.02),
        "dsa_k_norm": 1.0 + nrm(ks[10], (L, HEAD_DIM), 0.02),
        "w_branch_a": nrm(ks[11], (L, SWA_Q_WIDTH, D), SWA_Q_WIDTH ** -0.5),
        "w_branch_b": nrm(ks[12], (L, DSA_Q_WIDTH, D), DSA_Q_WIDTH ** -0.5),
        "w_out": nrm(ks[13], (L, D, D), D ** -0.5),
        "ln2_g": 1.0 + nrm(ks[14], (L, D), 0.02),
        "w_group": nrm(ks[15], (L, D, N_GROUPS), D ** -0.5),
        "b_group": nrm(ks[16], (L, N_GROUPS), 0.01),
        "w_expert": nrm(ks[17], (L, D, N_EXPERTS), D ** -0.5),
        "b_expert": nrm(ks[18], (L, N_EXPERTS), 0.01),
        "w_gate_up": nrm(ks[19], (L, N_GROUPS, EXPERTS_PER_GROUP, D, F), D ** -0.5),
        "w_up": nrm(ks[20], (L, N_GROUPS, EXPERTS_PER_GROUP, D, F), D ** -0.5),
        "w_down": nrm(ks[21], (L, N_GROUPS, EXPERTS_PER_GROUP, F, D), F ** -0.5),
    }


def reference(x, c, w_ada, b_ada, ln1_g, w_in, swa_q_norm, swa_k_norm, swa_sinks,
              dsa_q_norm, dsa_k_norm, w_branch_a, w_branch_b, w_out, ln2_g,
              w_group, b_group, w_expert, b_expert, w_gate_up, w_up, w_down):
    slopes = alibi_slopes()
    cond = jax.nn.silu(c)
    for l in range(DEPTH):
        mod = (cond @ w_ada[l] + b_ada[l])[:, None, :]
        sh1, sc1, gt1, sh2, sc2, gt2 = jnp.split(mod, 6, axis=-1)
        h = rms_norm(x, ln1_g[l]) * (1.0 + sc1) + sh1
        x = x + gt1 * hybrid_mixer(h, slopes, w_in[l], swa_q_norm[l], swa_k_norm[l], swa_sinks[l],
                                   dsa_q_norm[l], dsa_k_norm[l], w_branch_a[l], w_branch_b[l], w_out[l])
        h = rms_norm(x, ln2_g[l]) * (1.0 + sc2) + sh2
        x = x + gt2 * hier_moe(h, w_group[l], b_group[l], w_expert[l], b_expert[l],
                               w_gate_up[l], w_up[l], w_down[l])
    return x
```

```python
import functools

import jax
import jax.numpy as jnp
from jax import lax
from jax.experimental import pallas as pl
from jax.experimental.pallas import tpu as pltpu

F32 = jnp.float32
BF16 = jnp.bfloat16
I32 = jnp.int32

D_MODEL = 2048
HEAD_DIM = 64
SWA_HEADS = 16
SWA_KV_HEADS = 4
SWA_GROUP = SWA_HEADS // SWA_KV_HEADS
WINDOW = 128
DSA_HEADS = 16
IDX_HEADS = 16
IDX_DIM = 64
DSA_TOPK_MAX = 256
DSA_BLOCK = 128
N_GROUPS = 4
EXPERTS_PER_GROUP = 8
N_EXPERTS = N_GROUPS * EXPERTS_PER_GROUP
D_EXPERT = 512
RMS_EPS = 1e-6
NEG_INF = -1e30
INT_MIN = -(2 ** 31)

LANES = 128
VMEM_LIMIT = 56 * 1024 * 1024

COL_QA, COL_QB, COL_QI = 0, 1024, 2048
COL_KA, COL_VA, COL_SMALL = 3072, 3328, 3584
ATT_WIDTH = 3840
SM_KB, SM_VB, SM_KI, SM_WI = 0, 64, 128, 192

NT_DIMS = (((1,), (1,)), ((), ()))


def _cparams(*sem):
    return pltpu.CompilerParams(dimension_semantics=sem, vmem_limit_bytes=VMEM_LIMIT)


def _sigmoid(x):
    return 1.0 / (1.0 + jnp.exp(-x))


def _rms(x, g):
    return x * lax.rsqrt(jnp.mean(x * x, axis=-1, keepdims=True) + RMS_EPS) * g


def _ada_kernel(c_ref, w_ref, b_ref, o_ref):
    c = c_ref[...]
    cond = c * _sigmoid(c)
    o_ref[...] = jnp.dot(cond, w_ref[...], preferred_element_type=F32,
                         precision=lax.Precision.HIGHEST) + b_ref[...]


def _ada(c_pad, w, b):
    n = w.shape[1]
    tn = 1024
    return pl.pallas_call(
        _ada_kernel,
        grid=(n // tn,),
        in_specs=[pl.BlockSpec((8, D_MODEL), lambda j: (0, 0)),
                  pl.BlockSpec((D_MODEL, tn), lambda j: (0, j)),
                  pl.BlockSpec((1, tn), lambda j: (0, j))],
        out_specs=pl.BlockSpec((8, tn), lambda j: (0, j)),
        out_shape=jax.ShapeDtypeStruct((8, n), F32),
        compiler_params=_cparams("parallel"),
        name="ada_mod",
    )(c_pad, w, b)


def _ln_mod_kernel(x_ref, g_ref, sc_ref, sh_ref, o_ref):
    y = _rms(x_ref[0], g_ref[...])
    o_ref[0] = (y * (1.0 + sc_ref[0]) + sh_ref[0]).astype(o_ref.dtype)


def _ln_mod(x, g, sc, sh):
    b, s, d = x.shape
    ts = 512
    return pl.pallas_call(
        _ln_mod_kernel,
        grid=(b, s // ts),
        in_specs=[pl.BlockSpec((1, ts, d), lambda i, j: (i, j, 0)),
                  pl.BlockSpec((1, d), lambda i, j: (0, 0)),
                  pl.BlockSpec((1, 1, d), lambda i, j: (i, 0, 0)),
                  pl.BlockSpec((1, 1, d), lambda i, j: (i, 0, 0))],
        out_specs=pl.BlockSpec((1, ts, d), lambda i, j: (i, j, 0)),
        out_shape=jax.ShapeDtypeStruct((b, s, d), BF16),
        compiler_params=_cparams("parallel", "parallel"),
        name="ln_mod",
    )(x, g, sc, sh)


def _mm_kernel(a_ref, w_ref, o_ref):
    o_ref[...] = jnp.dot(a_ref[...], w_ref[...], preferred_element_type=F32).astype(o_ref.dtype)


def _mm(a, w, tm, tn, out_dtype):
    m, k = a.shape
    n = w.shape[1]
    return pl.pallas_call(
        _mm_kernel,
        grid=(m // tm, n // tn),
        in_specs=[pl.BlockSpec((tm, k), lambda i, j: (i, 0)),
                  pl.BlockSpec((k, tn), lambda i, j: (0, j))],
        out_specs=pl.BlockSpec((tm, tn), lambda i, j: (i, j)),
        out_shape=jax.ShapeDtypeStruct((m, n), out_dtype),
        compiler_params=_cparams("parallel", "parallel"),
        name="proj_att",
    )(a, w)


def _swa_kernel(sink_ref, slope_ref, q_ref, kp_ref, kc_ref, vp_ref, vc_ref, qg_ref, kg_ref, o_ref):
    n = pl.program_id(1)
    w2 = 2 * WINDOW
    q = q_ref[0]
    kw = jnp.concatenate([kp_ref[0], kc_ref[0]], axis=0)
    vw = jnp.concatenate([vp_ref[0], vc_ref[0]], axis=0).astype(BF16)
    row = lax.broadcasted_iota(I32, (WINDOW, w2), 0)
    col = lax.broadcasted_iota(I32, (WINDOW, w2), 1)
    rel = row + WINDOW - col
    valid = (rel >= 0) & (rel < WINDOW) & ((col >= WINDOW) | (n > 0))
    relf = rel.astype(F32)
    scale = HEAD_DIM ** -0.5
    for g in range(SWA_KV_HEADS):
        kg = _rms(kw[:, g * HEAD_DIM:(g + 1) * HEAD_DIM], kg_ref[...]).astype(BF16)
        vg = vw[:, g * HEAD_DIM:(g + 1) * HEAD_DIM]
        for j in range(SWA_GROUP):
            h = g * SWA_GROUP + j
            qh = _rms(q[:, h * HEAD_DIM:(h + 1) * HEAD_DIM], qg_ref[...]).astype(BF16)
            s = lax.dot_general(qh, kg, NT_DIMS, preferred_element_type=F32) * scale
            s = s - slope_ref[h] * relf
            s = jnp.where(valid, s, NEG_INF)
            sink = sink_ref[h]
            m = jnp.maximum(jnp.max(s, axis=-1, keepdims=True), sink)
            e = jnp.exp(s - m)
            denom = jnp.sum(e, axis=-1, keepdims=True) + jnp.exp(sink - m)
            p = (e / denom).astype(BF16)
            o = jnp.dot(p, vg, preferred_element_type=F32)
            o_ref[0, :, h * HEAD_DIM:(h + 1) * HEAD_DIM] = o.astype(o_ref.dtype)


def _swa(proj3, sinks, slopes, q_gain, k_gain):
    b, s, _ = proj3.shape
    nb = s // WINDOW
    kvw = SWA_KV_HEADS * HEAD_DIM
    qw = SWA_HEADS * HEAD_DIM
    smem = pl.BlockSpec(memory_space=pltpu.SMEM)
    prev = lambda i, n: (i, jnp.maximum(n - 1, 0), COL_KA // kvw)
    cur = lambda i, n: (i, n, COL_KA // kvw)
    prev_v = lambda i, n: (i, jnp.maximum(n - 1, 0), COL_VA // kvw)
    cur_v = lambda i, n: (i, n, COL_VA // kvw)
    return pl.pallas_call(
        _swa_kernel,
        grid=(b, nb),
        in_specs=[smem, smem,
                  pl.BlockSpec((1, WINDOW, qw), lambda i, n: (i, n, COL_QA // qw)),
                  pl.BlockSpec((1, WINDOW, kvw), prev),
                  pl.BlockSpec((1, WINDOW, kvw), cur),
                  pl.BlockSpec((1, WINDOW, kvw), prev_v),
                  pl.BlockSpec((1, WINDOW, kvw), cur_v),
                  pl.BlockSpec((1, HEAD_DIM), lambda i, n: (0, 0)),
                  pl.BlockSpec((1, HEAD_DIM), lambda i, n: (0, 0))],
        out_specs=pl.BlockSpec((1, WINDOW, qw), lambda i, n: (i, n, 0)),
        out_shape=jax.ShapeDtypeStruct((b, s, qw), BF16),
        compiler_params=_cparams("parallel", "parallel"),
        name="swa_attn",
    )(sinks, slopes, proj3, proj3, proj3, proj3, proj3, q_gain, k_gain)


def _dsa_kernel(slope_ref, qb_ref, qi_ref, sblk_ref, sfull_ref, qg_ref, kg_ref, o_ref, *, seq, topk):
    i = pl.program_id(1)
    blk = DSA_BLOCK
    sfull = sfull_ref[0]
    kb = sfull[:, SM_KB:SM_KB + HEAD_DIM]
    vb = sfull[:, SM_VB:SM_VB + HEAD_DIM]
    ki = sfull[:, SM_KI:SM_KI + IDX_DIM].astype(BF16)
    wi = sblk_ref[0][:, SM_WI:SM_WI + IDX_HEADS] * (IDX_HEADS ** -0.5)
    qi = qi_ref[0]

    score = jnp.zeros((blk, seq), F32)
    for h in range(IDX_HEADS):
        qh = qi[:, h * IDX_DIM:(h + 1) * IDX_DIM].astype(BF16)
        logit = lax.dot_general(qh, ki, NT_DIMS, preferred_element_type=F32) * (IDX_DIM ** -0.5)
        score = score + jnp.maximum(logit, 0.0) * wi[:, h:h + 1]

    t = i * blk + lax.broadcasted_iota(I32, (blk, seq), 0)
    kpos = lax.broadcasted_iota(I32, (blk, seq), 1)
    causal = kpos <= t

    bits = lax.bitcast_convert_type(score + 0.0, I32)
    key = jnp.where(bits < 0, bits ^ jnp.int32(0x7FFFFFFF), bits)
    key = jnp.where(causal, key, jnp.int32(INT_MIN))

    def search(it, thr):
        cand = thr + lax.shift_left(jnp.int32(1), 31 - it)
        cnt = jnp.sum((key >= cand).astype(F32), axis=-1, keepdims=True)
        return jnp.where(cnt >= float(topk), cand, thr)

    thr = lax.fori_loop(0, 32, search, jnp.full((blk, 1), INT_MIN, I32))

    gt = key > thr
    eq = (key == thr) & causal
    need = float(topk) - jnp.sum(gt.astype(F32), axis=-1, keepdims=True)
    eqf = eq.astype(BF16)
    cw = 256
    upper = (lax.broadcasted_iota(I32, (cw, cw), 0) < lax.broadcasted_iota(I32, (cw, cw), 1)).astype(BF16)
    carry = jnp.zeros((blk, 1), F32)
    pres = []
    for c in range(seq // cw):
        piece = eqf[:, c * cw:(c + 1) * cw]
        pres.append(jnp.dot(piece, upper, preferred_element_type=F32) + carry)
        carry = carry + jnp.sum(piece.astype(F32), axis=-1, keepdims=True)
    before = jnp.concatenate(pres, axis=1)
    sel = gt | (eq & (before < need))
    mbias = jnp.where(sel, 0.0, NEG_INF)

    kn = _rms(kb, kg_ref[...]).astype(BF16)
    vbf = vb.astype(BF16)
    dist = (t - kpos).astype(F32)
    q = qb_ref[0]
    scale = HEAD_DIM ** -0.5
    for h in range(DSA_HEADS):
        qh = _rms(q[:, h * HEAD_DIM:(h + 1) * HEAD_DIM], qg_ref[...]).astype(BF16)
        s = lax.dot_general(qh, kn, NT_DIMS, preferred_element_type=F32) * scale
        s = s - slope_ref[h] * dist + mbias
        m = jnp.max(s, axis=-1, keepdims=True)
        e = jnp.exp(s - m)
        l = jnp.sum(e, axis=-1, keepdims=True)
        o = jnp.dot(e.astype(BF16), vbf, preferred_element_type=F32) / l
        o_ref[0, :, h * HEAD_DIM:(h + 1) * HEAD_DIM] = o.astype(o_ref.dtype)


def _dsa(proj3, slopes, q_gain, k_gain):
    b, s, _ = proj3.shape
    nb = s // DSA_BLOCK
    qw = DSA_HEADS * HEAD_DIM
    smw = 256
    smem = pl.BlockSpec(memory_space=pltpu.SMEM)
    return pl.pallas_call(
        functools.partial(_dsa_kernel, seq=s, topk=min(DSA_TOPK_MAX, s // 4)),
        grid=(b, nb),
        in_specs=[smem,
                  pl.BlockSpec((1, DSA_BLOCK, qw), lambda i, n: (i, n, COL_QB // qw)),
                  pl.BlockSpec((1, DSA_BLOCK, qw), lambda i, n: (i, n, COL_QI // qw)),
                  pl.BlockSpec((1, DSA_BLOCK, smw), lambda i, n: (i, n, COL_SMALL // smw)),
                  pl.BlockSpec((1, s, smw), lambda i, n: (i, 0, COL_SMALL // smw)),
                  pl.BlockSpec((1, HEAD_DIM), lambda i, n: (0, 0)),
                  pl.BlockSpec((1, HEAD_DIM), lambda i, n: (0, 0))],
        out_specs=pl.BlockSpec((1, DSA_BLOCK, qw), lambda i, n: (i, n, 0)),
        out_shape=jax.ShapeDtypeStruct((b, s, qw), BF16),
        compiler_params=_cparams("parallel", "parallel"),
        name="dsa_attn",
    )(slopes, proj3, proj3, proj3, proj3, q_gain, k_gain)


def _merge_kernel(h_ref, oa_ref, ob_ref, wga_ref, wgb_ref, wa_ref, wb_ref, o_ref):
    h = h_ref[...]
    ga = jnp.dot(h, wga_ref[...], preferred_element_type=F32)
    gb = jnp.dot(h, wgb_ref[...], preferred_element_type=F32)
    a = jnp.dot(oa_ref[...], wa_ref[...], preferred_element_type=F32)
    b = jnp.dot(ob_ref[...], wb_ref[...], preferred_element_type=F32)
    o_ref[...] = (_sigmoid(ga) * a + _sigmoid(gb) * b).astype(o_ref.dtype)


def _merge(h, oa, ob, w_gate, wa, wb):
    m, d = h.shape
    tm, tn = 1024, 512
    ngt = d // tn
    return pl.pallas_call(
        _merge_kernel,
        grid=(m // tm, ngt),
        in_specs=[pl.BlockSpec((tm, d), lambda i, j: (i, 0)),
                  pl.BlockSpec((tm, oa.shape[1]), lambda i, j: (i, 0)),
                  pl.BlockSpec((tm, ob.shape[1]), lambda i, j: (i, 0)),
                  pl.BlockSpec((d, tn), lambda i, j: (0, j)),
                  pl.BlockSpec((d, tn), lambda i, j: (0, j + ngt)),
                  pl.BlockSpec((wa.shape[0], tn), lambda i, j: (0, j)),
                  pl.BlockSpec((wb.shape[0], tn), lambda i, j: (0, j))],
        out_specs=pl.BlockSpec((tm, tn), lambda i, j: (i, j)),
        out_shape=jax.ShapeDtypeStruct((m, d), BF16),
        compiler_params=_cparams("parallel", "parallel"),
        name="merge_branches",
    )(h, oa, ob, w_gate, w_gate, wa, wb)


def _outproj_kernel(mg_ref, wo_ref, x_ref, gt_ref, g2_ref, sc_ref, sh_ref, wr_ref, br_ref,
                    x1_ref, h2_ref, comb_ref):
    y = jnp.dot(mg_ref[0], wo_ref[...], preferred_element_type=F32)
    x1 = x_ref[0] + gt_ref[0] * y
    x1_ref[0] = x1
    h2 = _rms(x1, g2_ref[...]) * (1.0 + sc_ref[0]) + sh_ref[0]
    h2_ref[0] = h2.astype(h2_ref.dtype)

    logits = jnp.dot(h2, wr_ref[...], preferred_element_type=F32,
                     precision=lax.Precision.HIGHEST) + br_ref[...]
    lane = lax.broadcasted_iota(I32, logits.shape, 1)
    is_g = (lane >= N_EXPERTS) & (lane < N_EXPERTS + N_GROUPS)
    neg = -jnp.inf
    gl = jnp.where(is_g, logits, neg)
    gmax = jnp.max(gl, axis=-1, keepdims=True)
    g_idx = jnp.min(jnp.where(gl == gmax, lane - N_EXPERTS, N_GROUPS), axis=-1, keepdims=True)
    p_g = 1.0 / jnp.sum(jnp.where(is_g, jnp.exp(gl - gmax), 0.0), axis=-1, keepdims=True)
    in_grp = (lane < N_EXPERTS) & ((lane // EXPERTS_PER_GROUP) == g_idx)
    el = jnp.where(in_grp, logits, neg)
    m1 = jnp.max(el, axis=-1, keepdims=True)
    i1 = jnp.min(jnp.where(el == m1, lane, LANES), axis=-1, keepdims=True)
    el2 = jnp.where(lane == i1, neg, el)
    m2 = jnp.max(el2, axis=-1, keepdims=True)
    i2 = jnp.min(jnp.where(el2 == m2, lane, LANES), axis=-1, keepdims=True)
    e2 = jnp.exp(m2 - m1)
    w1 = 1.0 / (1.0 + e2)
    w2 = e2 / (1.0 + e2)
    comb_ref[0] = p_g * (jnp.where(lane == i1, w1, 0.0) + jnp.where(lane == i2, w2, 0.0))


def _outproj(merged3, w_out, x, gt1, g2, sc2, sh2, w_r, b_r):
    b, s, d = x.shape
    tm = 256
    row = lambda i, j: (i, j, 0)
    per_b = lambda i, j: (i, 0, 0)
    fixed = lambda i, j: (0, 0)
    return pl.pallas_call(
        _outproj_kernel,
        grid=(b, s // tm),
        in_specs=[pl.BlockSpec((1, tm, d), row),
                  pl.BlockSpec((d, d), fixed),
                  pl.BlockSpec((1, tm, d), row),
                  pl.BlockSpec((1, 1, d), per_b),
                  pl.BlockSpec((1, d), fixed),
                  pl.BlockSpec((1, 1, d), per_b),
                  pl.BlockSpec((1, 1, d), per_b),
                  pl.BlockSpec((d, LANES), fixed),
                  pl.BlockSpec((1, LANES), fixed)],
        out_specs=[pl.BlockSpec((1, tm, d), row),
                   pl.BlockSpec((1, tm, d), row),
                   pl.BlockSpec((1, tm, LANES), row)],
        out_shape=[jax.ShapeDtypeStruct((b, s, d), F32),
                   jax.ShapeDtypeStruct((b, s, d), BF16),
                   jax.ShapeDtypeStruct((b, s, LANES), F32)],
        compiler_params=_cparams("parallel", "parallel"),
        name="outproj_ln2_router",
    )(merged3, w_out, x, gt1, g2, sc2, sh2, w_r, b_r)


def _moe_kernel(h_ref, comb_ref, wg_ref, wu_ref, wd_ref, x1_ref, gt_ref, o_ref):
    e = pl.program_id(2)

    @pl.when(e == 0)
    def _():
        o_ref[...] = jnp.zeros_like(o_ref)

    h = h_ref[0]
    a = jnp.dot(h, wg_ref[0].astype(BF16), preferred_element_type=F32)
    u = jnp.dot(h, wu_ref[0].astype(BF16), preferred_element_type=F32)
    comb = comb_ref[0]
    lane = lax.broadcasted_iota(I32, comb.shape, 1)
    c = jnp.sum(jnp.where(lane == e, comb, 0.0), axis=-1, keepdims=True)
    act = (a * _sigmoid(a)) * u * c
    o_ref[0] += jnp.dot(act.astype(BF16), wd_ref[0].astype(BF16), preferred_element_type=F32)

    @pl.when(e == N_EXPERTS - 1)
    def _():
        o_ref[0] = x1_ref[0] + gt_ref[0] * o_ref[0]


def _moe(h2, comb, wg, wu, wd, x1, gt2):
    b, s, d = x1.shape
    tm = 512
    f = wg.shape[-1]
    row = lambda i, j, e: (i, j, 0)
    return pl.pallas_call(
        _moe_kernel,
        grid=(b, s // tm, N_EXPERTS),
        in_specs=[pl.BlockSpec((1, tm, d), row),
                  pl.BlockSpec((1, tm, LANES), row),
                  pl.BlockSpec((1, d, f), lambda i, j, e: (e, 0, 0)),
                  pl.BlockSpec((1, d, f), lambda i, j, e: (e, 0, 0)),
                  pl.BlockSpec((1, f, d), lambda i, j, e: (e, 0, 0)),
                  pl.BlockSpec((1, tm, d), row),
                  pl.BlockSpec((1, 1, d), lambda i, j, e: (i, 0, 0))],
        out_specs=pl.BlockSpec((1, tm, d), row),
        out_shape=jax.ShapeDtypeStruct((b, s, d), F32),
        compiler_params=_cparams("parallel", "parallel", "arbitrary"),
        name="moe_experts",
    )(h2, comb, wg, wu, wd, x1, gt2)


def _alibi_slopes():
    n = SWA_HEADS + DSA_HEADS
    i = jnp.arange(1, n + 1, dtype=F32)
    return jnp.exp2(-8.0 * i / n)


def _layer(x, mod, ln1_g, w_in, swa_q_norm, swa_k_norm, swa_sinks, dsa_q_norm, dsa_k_norm,
           w_branch_a, w_branch_b, w_out, ln2_g, w_group, b_group, w_expert, b_expert,
           w_gate_up, w_up, w_down, slopes):
    b, s, d = x.shape
    sh1, sc1, gt1, sh2, sc2, gt2 = [m[:, None, :] for m in jnp.split(mod, 6, axis=-1)]

    o = 0
    seg = {}
    for name, width in (("qa", 1024), ("ka", 256), ("va", 256), ("qb", 1024), ("kb", 64), ("vb", 64),
                        ("qi", 1024), ("ki", 64), ("wi", 16), ("g", 2 * d)):
        seg[name] = (o, o + width)
        o += width
    cols = lambda nm: w_in[:, seg[nm][0]:seg[nm][1]]
    pad = jnp.zeros((d, ATT_WIDTH - COL_SMALL - 208), w_in.dtype)
    w_att = jnp.concatenate([cols("qa"), cols("qb"), cols("qi"), cols("ka"), cols("va"),
                             cols("kb"), cols("vb"), cols("ki"), cols("wi"), pad], axis=1).astype(BF16)
    w_gate = cols("g").astype(BF16)

    h1 = _ln_mod(x, ln1_g[None, :], sc1, sh1)
    h1f = h1.reshape(b * s, d)
    proj = _mm(h1f, w_att, 1024, 768, F32).reshape(b, s, ATT_WIDTH)
    oa = _swa(proj, swa_sinks, slopes[:SWA_HEADS], swa_q_norm[None, :], swa_k_norm[None, :])
    ob = _dsa(proj, slopes[SWA_HEADS:], dsa_q_norm[None, :], dsa_k_norm[None, :])
    merged = _merge(h1f, oa.reshape(b * s, -1), ob.reshape(b * s, -1), w_gate,
                    w_branch_a.astype(BF16), w_branch_b.astype(BF16))

    w_r = jnp.concatenate([w_expert, w_group,
                           jnp.zeros((d, LANES - N_EXPERTS - N_GROUPS), F32)], axis=1)
    b_r = jnp.concatenate([b_expert, b_group,
                           jnp.zeros((LANES - N_EXPERTS - N_GROUPS,), F32)])[None, :]
    x1, h2, comb = _outproj(merged.reshape(b, s, d), w_out.astype(BF16), x, gt1,
                            ln2_g[None, :], sc2, sh2, w_r, b_r)
    f = w_gate_up.shape[-1]
    return _moe(h2, comb, w_gate_up.reshape(N_EXPERTS, d, f), w_up.reshape(N_EXPERTS, d, f),
                w_down.reshape(N_EXPERTS, f, d), x1, gt2)


def kernel(x, c, w_ada, b_ada, ln1_g, w_in, swa_q_norm, swa_k_norm, swa_sinks, dsa_q_norm, dsa_k_norm,
           w_branch_a, w_branch_b, w_out, ln2_g, w_group, b_group, w_expert, b_expert,
           w_gate_up, w_up, w_down):
    slopes = _alibi_slopes()
    bsz = c.shape[0]
    c_pad = jnp.concatenate([c, jnp.zeros((8 - bsz, c.shape[1]), c.dtype)], axis=0)
    for l in range(w_ada.shape[0]):
        mod = _ada(c_pad, w_ada[l], b_ada[l][None, :])[:bsz]
        x = _layer(x, mod, ln1_g[l], w_in[l], swa_q_norm[l], swa_k_norm[l], swa_sinks[l],
                   dsa_q_norm[l], dsa_k_norm[l], w_branch_a[l], w_branch_b[l], w_out[l], ln2_g[l],
                   w_group[l], b_group[l], w_expert[l], b_expert[l],
                   w_gate_up[l], w_up[l], w_down[l], slopes)
    return x
```

```python
import functools

import jax
import jax.numpy as jnp
from jax import lax
from jax.experimental import pallas as pl
from jax.experimental.pallas import tpu as pltpu

F32 = jnp.float32
BF16 = jnp.bfloat16
I32 = jnp.int32

D_MODEL = 2048
HEAD_DIM = 64
SWA_HEADS = 16
SWA_KV_HEADS = 4
SWA_GROUP = SWA_HEADS // SWA_KV_HEADS
WINDOW = 128
DSA_HEADS = 16
IDX_HEADS = 16
IDX_DIM = 64
DSA_TOPK_MAX = 256
DSA_BLOCK = 128
N_GROUPS = 4
EXPERTS_PER_GROUP = 8
N_EXPERTS = N_GROUPS * EXPERTS_PER_GROUP
D_EXPERT = 512
RMS_EPS = 1e-6
NEG_INF = -1e30
INT_MIN = -(2 ** 31)

LANES = 128
VMEM_LIMIT = 56 * 1024 * 1024

COL_QA, COL_QB, COL_QI = 0, 1024, 2048
COL_KA, COL_VA, COL_SMALL = 3072, 3328, 3584
ATT_WIDTH = 3840
SM_KB, SM_VB, SM_KI, SM_WI = 0, 64, 128, 192

NT_DIMS = (((1,), (1,)), ((), ()))


def _cparams(*sem):
    return pltpu.CompilerParams(dimension_semantics=sem, vmem_limit_bytes=VMEM_LIMIT)


def _sigmoid(x):
    return 1.0 / (1.0 + jnp.exp(-x))


def _rms(x, g):
    return x * lax.rsqrt(jnp.mean(x * x, axis=-1, keepdims=True) + RMS_EPS) * g


def _ada_kernel(c_ref, w_ref, b_ref, o_ref):
    c = c_ref[...]
    cond = c * _sigmoid(c)
    o_ref[...] = jnp.dot(cond, w_ref[...], preferred_element_type=F32,
                         precision=lax.Precision.HIGHEST) + b_ref[...]


def _ada(c_pad, w, b):
    n = w.shape[1]
    tn = 1024
    return pl.pallas_call(
        _ada_kernel,
        grid=(n // tn,),
        in_specs=[pl.BlockSpec((8, D_MODEL), lambda j: (0, 0)),
                  pl.BlockSpec((D_MODEL, tn), lambda j: (0, j)),
                  pl.BlockSpec((1, tn), lambda j: (0, j))],
        out_specs=pl.BlockSpec((8, tn), lambda j: (0, j)),
        out_shape=jax.ShapeDtypeStruct((8, n), F32),
        compiler_params=_cparams("parallel"),
        name="ada_mod",
    )(c_pad, w, b)


def _ln_mod_kernel(x_ref, g_ref, sc_ref, sh_ref, o_ref):
    y = _rms(x_ref[0], g_ref[...])
    o_ref[0] = (y * (1.0 + sc_ref[0]) + sh_ref[0]).astype(o_ref.dtype)


def _ln_mod(x, g, sc, sh):
    b, s, d = x.shape
    ts = 512
    return pl.pallas_call(
        _ln_mod_kernel,
        grid=(b, s // ts),
        in_specs=[pl.BlockSpec((1, ts, d), lambda i, j: (i, j, 0)),
                  pl.BlockSpec((1, d), lambda i, j: (0, 0)),
                  pl.BlockSpec((1, 1, d), lambda i, j: (i, 0, 0)),
                  pl.BlockSpec((1, 1, d), lambda i, j: (i, 0, 0))],
        out_specs=pl.BlockSpec((1, ts, d), lambda i, j: (i, j, 0)),
        out_shape=jax.ShapeDtypeStruct((b, s, d), BF16),
        compiler_params=_cparams("parallel", "parallel"),
        name="ln_mod",
    )(x, g, sc, sh)


def _mm_kernel(a_ref, w_ref, o_ref):
    o_ref[...] = jnp.dot(a_ref[...], w_ref[...], preferred_element_type=F32).astype(o_ref.dtype)


def _mm(a, w, tm, tn, out_dtype):
    m, k = a.shape
    n = w.shape[1]
    return pl.pallas_call(
        _mm_kernel,
        grid=(m // tm, n // tn),
        in_specs=[pl.BlockSpec((tm, k), lambda i, j: (i, 0)),
                  pl.BlockSpec((k, tn), lambda i, j: (0, j))],
        out_specs=pl.BlockSpec((tm, tn), lambda i, j: (i, j)),
        out_shape=jax.ShapeDtypeStruct((m, n), out_dtype),
        compiler_params=_cparams("parallel", "parallel"),
        name="proj_att",
    )(a, w)


def _swa_kernel(sink_ref, slope_ref, q_ref, kp_ref, kc_ref, vp_ref, vc_ref, qg_ref, kg_ref, o_ref):
    n = pl.program_id(1)
    w2 = 2 * WINDOW
    q = q_ref[0]
    kw = jnp.concatenate([kp_ref[0], kc_ref[0]], axis=0)
    vw = jnp.concatenate([vp_ref[0], vc_ref[0]], axis=0).astype(BF16)
    row = lax.broadcasted_iota(I32, (WINDOW, w2), 0)
    col = lax.broadcasted_iota(I32, (WINDOW, w2), 1)
    rel = row + WINDOW - col
    valid = (rel >= 0) & (rel < WINDOW) & ((col >= WINDOW) | (n > 0))
    relf = rel.astype(F32)
    scale = HEAD_DIM ** -0.5
    for g in range(SWA_KV_HEADS):
        kg = _rms(kw[:, g * HEAD_DIM:(g + 1) * HEAD_DIM], kg_ref[...]).astype(BF16)
        vg = vw[:, g * HEAD_DIM:(g + 1) * HEAD_DIM]
        for j in range(SWA_GROUP):
            h = g * SWA_GROUP + j
            qh = _rms(q[:, h * HEAD_DIM:(h + 1) * HEAD_DIM], qg_ref[...]).astype(BF16)
            s = lax.dot_general(qh, kg, NT_DIMS, preferred_element_type=F32) * scale
            s = s - slope_ref[h] * relf
            s = jnp.where(valid, s, NEG_INF)
            sink = sink_ref[h]
            m = jnp.maximum(jnp.max(s, axis=-1, keepdims=True), sink)
            e = jnp.exp(s - m)
            denom = jnp.sum(e, axis=-1, keepdims=True) + jnp.exp(sink - m)
            p = (e / denom).astype(BF16)
            o = jnp.dot(p, vg, preferred_element_type=F32)
            o_ref[0, :, h * HEAD_DIM:(h + 1) * HEAD_DIM] = o.astype(o_ref.dtype)


def _swa(proj3, sinks, slopes, q_gain, k_gain):
    b, s, _ = proj3.shape
    nb = s // WINDOW
    kvw = SWA_KV_HEADS * HEAD_DIM
    qw = SWA_HEADS * HEAD_DIM
    smem = pl.BlockSpec(memory_space=pltpu.SMEM)
    prev = lambda i, n: (i, jnp.maximum(n - 1, 0), COL_KA // kvw)
    cur = lambda i, n: (i, n, COL_KA // kvw)
    prev_v = lambda i, n: (i, jnp.maximum(n - 1, 0), COL_VA // kvw)
    cur_v = lambda i, n: (i, n, COL_VA // kvw)
    return pl.pallas_call(
        _swa_kernel,
        grid=(b, nb),
        in_specs=[smem, smem,
                  pl.BlockSpec((1, WINDOW, qw), lambda i, n: (i, n, COL_QA // qw)),
                  pl.BlockSpec((1, WINDOW, kvw), prev),
                  pl.BlockSpec((1, WINDOW, kvw), cur),
                  pl.BlockSpec((1, WINDOW, kvw), prev_v),
                  pl.BlockSpec((1, WINDOW, kvw), cur_v),
                  pl.BlockSpec((1, HEAD_DIM), lambda i, n: (0, 0)),
                  pl.BlockSpec((1, HEAD_DIM), lambda i, n: (0, 0))],
        out_specs=pl.BlockSpec((1, WINDOW, qw), lambda i, n: (i, n, 0)),
        out_shape=jax.ShapeDtypeStruct((b, s, qw), BF16),
        compiler_params=_cparams("parallel", "parallel"),
        name="swa_attn",
    )(sinks, slopes, proj3, proj3, proj3, proj3, proj3, q_gain, k_gain)


def _dsa_kernel(slope_ref, qb_ref, qi_ref, sblk_ref, sfull_ref, qg_ref, kg_ref, o_ref, *, seq, topk):
    i = pl.program_id(1)
    blk = DSA_BLOCK
    sfull = sfull_ref[0]
    kb = sfull[:, SM_KB:SM_KB + HEAD_DIM]
    vb = sfull[:, SM_VB:SM_VB + HEAD_DIM]
    ki = sfull[:, SM_KI:SM_KI + IDX_DIM].astype(BF16)
    wi = sblk_ref[0][:, SM_WI:SM_WI + IDX_HEADS] * (IDX_HEADS ** -0.5)
    qi = qi_ref[0]

    score = jnp.zeros((blk, seq), F32)
    for h in range(IDX_HEADS):
        qh = qi[:, h * IDX_DIM:(h + 1) * IDX_DIM].astype(BF16)
        logit = lax.dot_general(qh, ki, NT_DIMS, preferred_element_type=F32) * (IDX_DIM ** -0.5)
        score = score + jnp.maximum(logit, 0.0) * wi[:, h:h + 1]

    t = i * blk + lax.broadcasted_iota(I32, (blk, seq), 0)
    kpos = lax.broadcasted_iota(I32, (blk, seq), 1)
    causal = kpos <= t

    bits = lax.bitcast_convert_type(score + 0.0, I32)
    key = jnp.where(bits < 0, bits ^ jnp.int32(0x7FFFFFFF), bits)
    key = jnp.where(causal, key, jnp.int32(INT_MIN))

    def search(it, thr):
        cand = thr + lax.shift_left(jnp.int32(1), 31 - it)
        cnt = jnp.sum((key >= cand).astype(F32), axis=-1, keepdims=True)
        return jnp.where(cnt >= float(topk), cand, thr)

    thr = lax.fori_loop(0, 32, search, jnp.full((blk, 1), INT_MIN, I32))

    gt = key > thr
    eq = (key == thr) & causal
    need = float(topk) - jnp.sum(gt.astype(F32), axis=-1, keepdims=True)
    eqf = eq.astype(BF16)
    cw = 256
    upper = (lax.broadcasted_iota(I32, (cw, cw), 0) < lax.broadcasted_iota(I32, (cw, cw), 1)).astype(BF16)
    carry = jnp.zeros((blk, 1), F32)
    pres = []
    for c in range(seq // cw):
        piece = eqf[:, c * cw:(c + 1) * cw]
        pres.append(jnp.dot(piece, upper, preferred_element_type=F32) + carry)
        carry = carry + jnp.sum(piece.astype(F32), axis=-1, keepdims=True)
    before = jnp.concatenate(pres, axis=1)
    sel = gt | (eq & (before < need))
    mbias = jnp.where(sel, 0.0, NEG_INF)

    kn = _rms(kb, kg_ref[...]).astype(BF16)
    vbf = vb.astype(BF16)
    dist = (t - kpos).astype(F32)
    q = qb_ref[0]
    scale = HEAD_DIM ** -0.5
    for h in range(DSA_HEADS):
        qh = _rms(q[:, h * HEAD_DIM:(h + 1) * HEAD_DIM], qg_ref[...]).astype(BF16)
        s = lax.dot_general(qh, kn, NT_DIMS, preferred_element_type=F32) * scale
        s = s - slope_ref[h] * dist + mbias
        m = jnp.max(s, axis=-1, keepdims=True)
        e = jnp.exp(s - m)
        l = jnp.sum(e, axis=-1, keepdims=True)
        o = jnp.dot(e.astype(BF16), vbf, preferred_element_type=F32) / l
        o_ref[0, :, h * HEAD_DIM:(h + 1) * HEAD_DIM] = o.astype(o_ref.dtype)


def _dsa(proj3, slopes, q_gain, k_gain):
    b, s, _ = proj3.shape
    nb = s // DSA_BLOCK
    qw = DSA_HEADS * HEAD_DIM
    smw = 256
    smem = pl.BlockSpec(memory_space=pltpu.SMEM)
    return pl.pallas_call(
        functools.partial(_dsa_kernel, seq=s, topk=min(DSA_TOPK_MAX, s // 4)),
        grid=(b, nb),
        in_specs=[smem,
                  pl.BlockSpec((1, DSA_BLOCK, qw), lambda i, n: (i, n, COL_QB // qw)),
                  pl.BlockSpec((1, DSA_BLOCK, qw), lambda i, n: (i, n, COL_QI // qw)),
                  pl.BlockSpec((1, DSA_BLOCK, smw), lambda i, n: (i, n, COL_SMALL // smw)),
                  pl.BlockSpec((1, s, smw), lambda i, n: (i, 0, COL_SMALL // smw)),
                  pl.BlockSpec((1, HEAD_DIM), lambda i, n: (0, 0)),
                  pl.BlockSpec((1, HEAD_DIM), lambda i, n: (0, 0))],
        out_specs=pl.BlockSpec((1, DSA_BLOCK, qw), lambda i, n: (i, n, 0)),
        out_shape=jax.ShapeDtypeStruct((b, s, qw), BF16),
        compiler_params=_cparams("parallel", "parallel"),
        name="dsa_attn",
    )(slopes, proj3, proj3, proj3, proj3, q_gain, k_gain)


def _merge_kernel(h_ref, oa_ref, ob_ref, wga_ref, wgb_ref, wa_ref, wb_ref, o_ref):
    h = h_ref[...]
    ga = jnp.dot(h, wga_ref[...], preferred_element_type=F32)
    gb = jnp.dot(h, wgb_ref[...], preferred_element_type=F32)
    a = jnp.dot(oa_ref[...], wa_ref[...], preferred_element_type=F32)
    b = jnp.dot(ob_ref[...], wb_ref[...], preferred_element_type=F32)
    o_ref[...] = (_sigmoid(ga) * a + _sigmoid(gb) * b).astype(o_ref.dtype)


def _merge(h, oa, ob, w_gate, wa, wb):
    m, d = h.shape
    tm, tn = 1024, 512
    ngt = d // tn
    return pl.pallas_call(
        _merge_kernel,
        grid=(m // tm, ngt),
        in_specs=[pl.BlockSpec((tm, d), lambda i, j: (i, 0)),
                  pl.BlockSpec((tm, oa.shape[1]), lambda i, j: (i, 0)),
                  pl.BlockSpec((tm, ob.shape[1]), lambda i, j: (i, 0)),
                  pl.BlockSpec((d, tn), lambda i, j: (0, j)),
                  pl.BlockSpec((d, tn), lambda i, j: (0, j + ngt)),
                  pl.BlockSpec((wa.shape[0], tn), lambda i, j: (0, j)),
                  pl.BlockSpec((wb.shape[0], tn), lambda i, j: (0, j))],
        out_specs=pl.BlockSpec((tm, tn), lambda i, j: (i, j)),
        out_shape=jax.ShapeDtypeStruct((m, d), BF16),
        compiler_params=_cparams("parallel", "parallel"),
        name="merge_branches",
    )(h, oa, ob, w_gate, w_gate, wa, wb)


MOE_TILE = 256
MOE_SLOTS = 2
TOKEN_TILE = 256


def _moe_max_tiles(n_tokens):
    return -(-(n_tokens * MOE_SLOTS + N_EXPERTS * (MOE_TILE - 1)) // MOE_TILE)


def _pack_bf16_pair(x):
    k = x.shape[1] // 2
    bits = lax.bitcast_convert_type(x.astype(BF16).astype(F32), I32)
    return lax.shift_right_logical(bits[:, :k], 16) | (bits[:, k:] & jnp.int32(-65536))


def _unpack_bf16_pair(w):
    lo = lax.bitcast_convert_type(lax.shift_left(w, 16), F32).astype(BF16)
    hi = lax.bitcast_convert_type(w & jnp.int32(-65536), F32).astype(BF16)
    return lo, hi


def _outproj_kernel(mg_ref, wo_ref, x_ref, gt_ref, g2_ref, sc_ref, sh_ref, wr_ref, br_ref,
                    x1_ref, h2p_ref, ids_ref, cw_ref):
    y = jnp.dot(mg_ref[0], wo_ref[...], preferred_element_type=F32)
    x1 = x_ref[0] + gt_ref[0] * y
    x1_ref[0] = x1
    h2 = _rms(x1, g2_ref[...]) * (1.0 + sc_ref[0]) + sh_ref[0]
    h2p_ref[0] = _pack_bf16_pair(h2)

    logits = jnp.dot(h2, wr_ref[...], preferred_element_type=F32,
                     precision=lax.Precision.HIGHEST) + br_ref[...]
    lane = lax.broadcasted_iota(I32, logits.shape, 1)
    is_g = (lane >= N_EXPERTS) & (lane < N_EXPERTS + N_GROUPS)
    neg = -jnp.inf
    gl = jnp.where(is_g, logits, neg)
    gmax = jnp.max(gl, axis=-1, keepdims=True)
    g_idx = jnp.min(jnp.where(gl == gmax, lane - N_EXPERTS, N_GROUPS), axis=-1, keepdims=True)
    p_g = 1.0 / jnp.sum(jnp.where(is_g, jnp.exp(gl - gmax), 0.0), axis=-1, keepdims=True)
    in_grp = (lane < N_EXPERTS) & ((lane // EXPERTS_PER_GROUP) == g_idx)
    el = jnp.where(in_grp, logits, neg)
    m1 = jnp.max(el, axis=-1, keepdims=True)
    i1 = jnp.min(jnp.where(el == m1, lane, LANES), axis=-1, keepdims=True)
    el2 = jnp.where(lane == i1, neg, el)
    m2 = jnp.max(el2, axis=-1, keepdims=True)
    i2 = jnp.min(jnp.where(el2 == m2, lane, LANES), axis=-1, keepdims=True)
    e2 = jnp.exp(m2 - m1)
    w1 = 1.0 / (1.0 + e2)
    w2 = e2 / (1.0 + e2)
    ids_ref[0] = jnp.where(lane == 0, i1, jnp.where(lane == 1, i2, 0))
    cw_ref[0] = p_g * jnp.where(lane == 0, w1, jnp.where(lane == 1, w2, 0.0))


def _outproj(merged3, w_out, x, gt1, g2, sc2, sh2, w_r, b_r):
    b, s, d = x.shape
    tm = 256
    row = lambda i, j: (i, j, 0)
    per_b = lambda i, j: (i, 0, 0)
    fixed = lambda i, j: (0, 0)
    return pl.pallas_call(
        _outproj_kernel,
        grid=(b, s // tm),
        in_specs=[pl.BlockSpec((1, tm, d), row),
                  pl.BlockSpec((d, d), fixed),
                  pl.BlockSpec((1, tm, d), row),
                  pl.BlockSpec((1, 1, d), per_b),
                  pl.BlockSpec((1, d), fixed),
                  pl.BlockSpec((1, 1, d), per_b),
                  pl.BlockSpec((1, 1, d), per_b),
                  pl.BlockSpec((d, LANES), fixed),
                  pl.BlockSpec((1, LANES), fixed)],
        out_specs=[pl.BlockSpec((1, tm, d), row),
                   pl.BlockSpec((1, tm, d // 2), row),
                   pl.BlockSpec((1, tm, LANES), row),
                   pl.BlockSpec((1, tm, LANES), row)],
        out_shape=[jax.ShapeDtypeStruct((b, s, d), F32),
                   jax.ShapeDtypeStruct((b, s, d // 2), I32),
                   jax.ShapeDtypeStruct((b, s, LANES), I32),
                   jax.ShapeDtypeStruct((b, s, LANES), F32)],
        compiler_params=_cparams("parallel", "parallel"),
        name="outproj_ln2_router",
    )(merged3, w_out, x, gt1, g2, sc2, sh2, w_r, b_r)


def _plan_kernel(ids_ref, dest_ref, te_ref, rank_ref):
    n = ids_ref.shape[0]
    cb = 256
    lane = lax.broadcasted_iota(I32, (cb, LANES), 1)
    lower = (lax.broadcasted_iota(I32, (cb, cb), 0) > lax.broadcasted_iota(I32, (cb, cb), 1)).astype(BF16)

    def count(bi, carry):
        rows = pl.ds(pl.multiple_of(bi * cb, cb), cb)
        ids = ids_ref[rows, :]
        m1 = lane == ids[:, 0:1]
        m2 = lane == ids[:, 1:2]
        onehot = m1 | m2
        before = jnp.dot(lower, onehot.astype(BF16), preferred_element_type=F32) + carry
        r1 = jnp.sum(jnp.where(m1, before, 0.0), axis=-1, keepdims=True)
        r2 = jnp.sum(jnp.where(m2, before, 0.0), axis=-1, keepdims=True)
        rank_ref[rows, :] = jnp.where(lane == 0, r1, jnp.where(lane == 1, r2, 0.0))
        return carry + jnp.sum(onehot.astype(F32), axis=0, keepdims=True)

    counts = lax.fori_loop(0, n // cb, count, jnp.zeros((1, LANES), F32))
    padded = jnp.floor((counts + (MOE_TILE - 1)) / MOE_TILE) * MOE_TILE
    upper = (lax.broadcasted_iota(I32, (LANES, LANES), 0)
             < lax.broadcasted_iota(I32, (LANES, LANES), 1)).astype(F32)
    offs = jnp.dot(jnp.broadcast_to(padded, (8, LANES)), upper, preferred_element_type=F32,
                   precision=lax.Precision.HIGHEST)[0:1]

    def place(bi, c):
        rows = pl.ds(pl.multiple_of(bi * cb, cb), cb)
        ids = ids_ref[rows, :]
        o1 = jnp.sum(jnp.where(lane == ids[:, 0:1], offs, 0.0), axis=-1, keepdims=True)
        o2 = jnp.sum(jnp.where(lane == ids[:, 1:2], offs, 0.0), axis=-1, keepdims=True)
        slot = rank_ref[rows, :] + jnp.where(lane == 0, o1, jnp.where(lane == 1, o2, 0.0))
        dest_ref[rows, :] = slot.astype(I32)
        return c

    lax.fori_loop(0, n // cb, place, 0)

    sq_row = lax.broadcasted_iota(I32, (LANES, LANES), 0)
    sq_lane = lax.broadcasted_iota(I32, (LANES, LANES), 1)
    tile_start = sq_row.astype(F32) * MOE_TILE
    finished = ((offs + padded) <= tile_start) & (sq_lane < N_EXPERTS)
    tile_expert = jnp.minimum(jnp.sum(finished.astype(F32), axis=-1, keepdims=True), N_EXPERTS - 1.0)
    n_tiles = jnp.sum(padded, axis=-1, keepdims=True) / MOE_TILE
    te_ref[...] = jnp.where(sq_lane == 1, n_tiles, tile_expert).astype(I32)


def _plan(ids):
    n = ids.shape[0]
    return pl.pallas_call(
        _plan_kernel,
        out_shape=[jax.ShapeDtypeStruct((n, LANES), I32),
                   jax.ShapeDtypeStruct((LANES, LANES), I32)],
        scratch_shapes=[pltpu.VMEM((n, LANES), F32)],
        compiler_params=pltpu.CompilerParams(vmem_limit_bytes=VMEM_LIMIT),
        name="moe_plan",
    )(ids)


def _dispatch_kernel(dest_ref, h_ref, hs_zero_ref, hs_ref, sem):
    del hs_zero_ref
    tq = h_ref.shape[0]

    def issue(r, c):
        for k in range(MOE_SLOTS):
            slot = dest_ref[0, 0, MOE_SLOTS * r + k]
            pltpu.make_async_copy(h_ref.at[pl.ds(r, 1)], hs_ref.at[pl.ds(slot, 1)], sem).start(priority=k)
        return c

    lax.fori_loop(0, tq, issue, 0, unroll=8)
    done = hs_ref.at[pl.ds(0, MOE_SLOTS * tq)]
    pltpu.make_async_copy(done, done, sem).wait()


def _dispatch(dest3, h2p, n_rows):
    n, w = h2p.shape
    tq = TOKEN_TILE
    hs_zero = jnp.zeros((n_rows, w), h2p.dtype)
    return pl.pallas_call(
        _dispatch_kernel,
        grid=(n // tq,),
        in_specs=[pl.BlockSpec((1, 1, MOE_SLOTS * tq), lambda i: (i, 0, 0), memory_space=pltpu.SMEM),
                  pl.BlockSpec((tq, w), lambda i: (i, 0)),
                  pl.BlockSpec(memory_space=pl.ANY)],
        out_specs=pl.BlockSpec(memory_space=pl.ANY),
        out_shape=jax.ShapeDtypeStruct((n_rows, w), h2p.dtype),
        scratch_shapes=[pltpu.SemaphoreType.DMA(())],
        input_output_aliases={2: 0},
        compiler_params=_cparams("arbitrary"),
        name="moe_dispatch",
    )(dest3, h2p, hs_zero)


def _expert_kernel(te_ref, nt_ref, hs_ref, wg_ref, wu_ref, wd_ref, ys_ref):
    del te_ref
    used = pl.program_id(0) < nt_ref[0]

    @pl.when(jnp.logical_not(used))
    def _():
        ys_ref[...] = jnp.zeros_like(ys_ref)

    @pl.when(used)
    def _():
        lo, hi = _unpack_bf16_pair(hs_ref[...])
        k = lo.shape[1]

        def up(w_ref):
            return (jnp.dot(lo, w_ref[0, :k, :].astype(BF16), preferred_element_type=F32)
                    + jnp.dot(hi, w_ref[0, k:, :].astype(BF16), preferred_element_type=F32))

        a = up(wg_ref)
        u = up(wu_ref)
        act = (a * _sigmoid(a) * u).astype(BF16)
        ys_ref[...] = jnp.dot(act, wd_ref[0].astype(BF16), preferred_element_type=F32)


def _experts(tile_expert, n_tiles, hs, wg, wu, wd):
    n_rows, hw = hs.shape
    d, f = wg.shape[1], wg.shape[2]
    row = lambda t, te, nt: (jnp.minimum(t, nt[0] - 1), 0)
    wsel = lambda t, te, nt: (te[t], 0, 0)
    return pl.pallas_call(
        _expert_kernel,
        grid_spec=pltpu.PrefetchScalarGridSpec(
            num_scalar_prefetch=2,
            grid=(n_rows // MOE_TILE,),
            in_specs=[pl.BlockSpec((MOE_TILE, hw), row),
                      pl.BlockSpec((1, d, f), wsel),
                      pl.BlockSpec((1, d, f), wsel),
                      pl.BlockSpec((1, f, d), wsel)],
            out_specs=pl.BlockSpec((MOE_TILE, d), lambda t, te, nt: (t, 0))),
        out_shape=jax.ShapeDtypeStruct((n_rows, d), F32),
        compiler_params=_cparams("arbitrary"),
        name="moe_experts",
    )(tile_expert, n_tiles, hs, wg, wu, wd)


def _combine_kernel(dest_ref, ys_ref, x1_ref, cw_ref, gt_ref, o_ref, buf, sem):
    tq = x1_ref.shape[1]

    def issue(r, c):
        for k in range(MOE_SLOTS):
            slot = dest_ref[0, 0, MOE_SLOTS * r + k]
            pltpu.make_async_copy(ys_ref.at[pl.ds(slot, 1)], buf.at[k, pl.ds(r, 1)], sem).start(priority=k)
        return c

    lax.fori_loop(0, tq, issue, 0, unroll=8)
    pltpu.make_async_copy(buf, buf, sem).wait()
    cw = cw_ref[0]
    y = cw[:, 0:1] * buf[0] + cw[:, 1:2] * buf[1]
    o_ref[0] = x1_ref[0] + gt_ref[0] * y


def _combine(dest3, ys, x1, cw, gt2):
    b, s, d = x1.shape
    tq = TOKEN_TILE
    spb = s // tq
    row = lambda i, j: (i, j, 0)
    return pl.pallas_call(
        _combine_kernel,
        grid=(b, spb),
        in_specs=[pl.BlockSpec((1, 1, MOE_SLOTS * tq), lambda i, j: (i * spb + j, 0, 0),
                               memory_space=pltpu.SMEM),
                  pl.BlockSpec(memory_space=pl.ANY),
                  pl.BlockSpec((1, tq, d), row),
                  pl.BlockSpec((1, tq, LANES), row),
                  pl.BlockSpec((1, 1, d), lambda i, j: (i, 0, 0))],
        out_specs=pl.BlockSpec((1, tq, d), row),
        out_shape=jax.ShapeDtypeStruct((b, s, d), F32),
        scratch_shapes=[pltpu.VMEM((MOE_SLOTS, tq, d), F32), pltpu.SemaphoreType.DMA(())],
        compiler_params=_cparams("arbitrary", "arbitrary"),
        name="moe_combine",
    )(dest3, ys, x1, cw, gt2)


def _moe(h2p, ids, cw, wg, wu, wd, x1, gt2):
    b, s, d = x1.shape
    n = b * s
    n_rows = _moe_max_tiles(n) * MOE_TILE
    dest, te = _plan(ids.reshape(n, LANES))
    dest3 = dest[:, :MOE_SLOTS].reshape(n // TOKEN_TILE, 1, MOE_SLOTS * TOKEN_TILE)
    hs = _dispatch(dest3, h2p.reshape(n, d // 2), n_rows)
    ys = _experts(te[:n_rows // MOE_TILE, 0], te[0, 1:2], hs, wg, wu, wd)
    return _combine(dest3, ys, x1, cw, gt2)


def _alibi_slopes():
    n = SWA_HEADS + DSA_HEADS
    i = jnp.arange(1, n + 1, dtype=F32)
    return jnp.exp2(-8.0 * i / n)


def _layer(x, mod, ln1_g, w_in, swa_q_norm, swa_k_norm, swa_sinks, dsa_q_norm, dsa_k_norm,
           w_branch_a, w_branch_b, w_out, ln2_g, w_group, b_group, w_expert, b_expert,
           w_gate_up, w_up, w_down, slopes):
    b, s, d = x.shape
    sh1, sc1, gt1, sh2, sc2, gt2 = [m[:, None, :] for m in jnp.split(mod, 6, axis=-1)]

    o = 0
    seg = {}
    for name, width in (("qa", 1024), ("ka", 256), ("va", 256), ("qb", 1024), ("kb", 64), ("vb", 64),
                        ("qi", 1024), ("ki", 64), ("wi", 16), ("g", 2 * d)):
        seg[name] = (o, o + width)
        o += width
    cols = lambda nm: w_in[:, seg[nm][0]:seg[nm][1]]
    pad = jnp.zeros((d, ATT_WIDTH - COL_SMALL - 208), w_in.dtype)
    w_att = jnp.concatenate([cols("qa"), cols("qb"), cols("qi"), cols("ka"), cols("va"),
                             cols("kb"), cols("vb"), cols("ki"), cols("wi"), pad], axis=1).astype(BF16)
    w_gate = cols("g").astype(BF16)

    h1 = _ln_mod(x, ln1_g[None, :], sc1, sh1)
    h1f = h1.reshape(b * s, d)
    proj = _mm(h1f, w_att, 1024, 768, F32).reshape(b, s, ATT_WIDTH)
    oa = _swa(proj, swa_sinks, slopes[:SWA_HEADS], swa_q_norm[None, :], swa_k_norm[None, :])
    ob = _dsa(proj, slopes[SWA_HEADS:], dsa_q_norm[None, :], dsa_k_norm[None, :])
    merged = _merge(h1f, oa.reshape(b * s, -1), ob.reshape(b * s, -1), w_gate,
                    w_branch_a.astype(BF16), w_branch_b.astype(BF16))

    w_r = jnp.concatenate([w_expert, w_group,
                           jnp.zeros((d, LANES - N_EXPERTS - N_GROUPS), F32)], axis=1)
    b_r = jnp.concatenate([b_expert, b_group,
                           jnp.zeros((LANES - N_EXPERTS - N_GROUPS,), F32)])[None, :]
    x1, h2p, ids, cw = _outproj(merged.reshape(b, s, d), w_out.astype(BF16), x, gt1,
                                ln2_g[None, :], sc2, sh2, w_r, b_r)
    f = w_gate_up.shape[-1]
    return _moe(h2p, ids, cw, w_gate_up.reshape(N_EXPERTS, d, f), w_up.reshape(N_EXPERTS, d, f),
                w_down.reshape(N_EXPERTS, f, d), x1, gt2)


def kernel(x, c, w_ada, b_ada, ln1_g, w_in, swa_q_norm, swa_k_norm, swa_sinks, dsa_q_norm, dsa_k_norm,
           w_branch_a, w_branch_b, w_out, ln2_g, w_group, b_group, w_expert, b_expert,
           w_gate_up, w_up, w_down):
    slopes = _alibi_slopes()
    bsz = c.shape[0]
    c_pad = jnp.concatenate([c, jnp.zeros((8 - bsz, c.shape[1]), c.dtype)], axis=0)
    for l in range(w_ada.shape[0]):
        mod = _ada(c_pad, w_ada[l], b_ada[l][None, :])[:bsz]
        x = _layer(x, mod, ln1_g[l], w_in[l], swa_q_norm[l], swa_k_norm[l], swa_sinks[l],
                   dsa_q_norm[l], dsa_k_norm[l], w_branch_a[l], w_branch_b[l], w_out[l], ln2_g[l],
                   w_group[l], b_group[l], w_expert[l], b_expert[l],
                   w_gate_up[l], w_up[l], w_down[l], slopes)
    return x
```

```python
import functools

import jax
import jax.numpy as jnp
from jax import lax
from jax.experimental import pallas as pl
from jax.experimental.pallas import tpu as pltpu

F32 = jnp.float32
BF16 = jnp.bfloat16
I32 = jnp.int32

D_MODEL = 2048
HEAD_DIM = 64
SWA_HEADS = 16
SWA_KV_HEADS = 4
SWA_GROUP = SWA_HEADS // SWA_KV_HEADS
WINDOW = 128
DSA_HEADS = 16
IDX_HEADS = 16
IDX_DIM = 64
DSA_TOPK_MAX = 256
DSA_BLOCK = 128
N_GROUPS = 4
EXPERTS_PER_GROUP = 8
N_EXPERTS = N_GROUPS * EXPERTS_PER_GROUP
D_EXPERT = 512
RMS_EPS = 1e-6
NEG_INF = -1e30
INT_MIN = -(2 ** 31)

LANES = 128
VMEM_LIMIT = 56 * 1024 * 1024

COL_QA, COL_QB, COL_QI = 0, 1024, 2048
COL_KA, COL_VA, COL_SMALL = 3072, 3328, 3584
ATT_WIDTH = 3840
SM_KB, SM_VB, SM_KI, SM_WI = 0, 64, 128, 192

NT_DIMS = (((1,), (1,)), ((), ()))


def _cparams(*sem):
    return pltpu.CompilerParams(dimension_semantics=sem, vmem_limit_bytes=VMEM_LIMIT)


def _sigmoid(x):
    return 1.0 / (1.0 + jnp.exp(-x))


def _rms(x, g):
    return x * lax.rsqrt(jnp.mean(x * x, axis=-1, keepdims=True) + RMS_EPS) * g


def _ada_kernel(c_ref, w_ref, b_ref, o_ref):
    c = c_ref[...]
    cond = c * _sigmoid(c)
    o_ref[...] = jnp.dot(cond, w_ref[...], preferred_element_type=F32,
                         precision=lax.Precision.HIGHEST) + b_ref[...]


def _ada(c_pad, w, b):
    n = w.shape[1]
    tn = 1024
    return pl.pallas_call(
        _ada_kernel,
        grid=(n // tn,),
        in_specs=[pl.BlockSpec((8, D_MODEL), lambda j: (0, 0)),
                  pl.BlockSpec((D_MODEL, tn), lambda j: (0, j)),
                  pl.BlockSpec((1, tn), lambda j: (0, j))],
        out_specs=pl.BlockSpec((8, tn), lambda j: (0, j)),
        out_shape=jax.ShapeDtypeStruct((8, n), F32),
        compiler_params=_cparams("parallel"),
        name="ada_mod",
    )(c_pad, w, b)


def _ln_mod_kernel(x_ref, g_ref, sc_ref, sh_ref, o_ref):
    y = _rms(x_ref[0], g_ref[...])
    o_ref[0] = (y * (1.0 + sc_ref[0]) + sh_ref[0]).astype(o_ref.dtype)


def _ln_mod(x, g, sc, sh):
    b, s, d = x.shape
    ts = 512
    return pl.pallas_call(
        _ln_mod_kernel,
        grid=(b, s // ts),
        in_specs=[pl.BlockSpec((1, ts, d), lambda i, j: (i, j, 0)),
                  pl.BlockSpec((1, d), lambda i, j: (0, 0)),
                  pl.BlockSpec((1, 1, d), lambda i, j: (i, 0, 0)),
                  pl.BlockSpec((1, 1, d), lambda i, j: (i, 0, 0))],
        out_specs=pl.BlockSpec((1, ts, d), lambda i, j: (i, j, 0)),
        out_shape=jax.ShapeDtypeStruct((b, s, d), BF16),
        compiler_params=_cparams("parallel", "parallel"),
        name="ln_mod",
    )(x, g, sc, sh)


def _mm_kernel(a_ref, w_ref, o_ref):
    o_ref[...] = jnp.dot(a_ref[...], w_ref[...], preferred_element_type=F32).astype(o_ref.dtype)


def _mm(a, w, tm, tn, out_dtype):
    m, k = a.shape
    n = w.shape[1]
    return pl.pallas_call(
        _mm_kernel,
        grid=(m // tm, n // tn),
        in_specs=[pl.BlockSpec((tm, k), lambda i, j: (i, 0)),
                  pl.BlockSpec((k, tn), lambda i, j: (0, j))],
        out_specs=pl.BlockSpec((tm, tn), lambda i, j: (i, j)),
        out_shape=jax.ShapeDtypeStruct((m, n), out_dtype),
        compiler_params=_cparams("parallel", "parallel"),
        name="proj_att",
    )(a, w)


def _swa_kernel(sink_ref, slope_ref, q_ref, kp_ref, kc_ref, vp_ref, vc_ref, qg_ref, kg_ref, o_ref):
    n = pl.program_id(1)
    w2 = 2 * WINDOW
    q = q_ref[0]
    kw = jnp.concatenate([kp_ref[0], kc_ref[0]], axis=0)
    vw = jnp.concatenate([vp_ref[0], vc_ref[0]], axis=0).astype(BF16)
    row = lax.broadcasted_iota(I32, (WINDOW, w2), 0)
    col = lax.broadcasted_iota(I32, (WINDOW, w2), 1)
    rel = row + WINDOW - col
    valid = (rel >= 0) & (rel < WINDOW) & ((col >= WINDOW) | (n > 0))
    relf = rel.astype(F32)
    scale = HEAD_DIM ** -0.5
    for g in range(SWA_KV_HEADS):
        kg = _rms(kw[:, g * HEAD_DIM:(g + 1) * HEAD_DIM], kg_ref[...]).astype(BF16)
        vg = vw[:, g * HEAD_DIM:(g + 1) * HEAD_DIM]
        for j in range(SWA_GROUP):
            h = g * SWA_GROUP + j
            qh = _rms(q[:, h * HEAD_DIM:(h + 1) * HEAD_DIM], qg_ref[...]).astype(BF16)
            s = lax.dot_general(qh, kg, NT_DIMS, preferred_element_type=F32) * scale
            s = s - slope_ref[h] * relf
            s = jnp.where(valid, s, NEG_INF)
            sink = sink_ref[h]
            m = jnp.maximum(jnp.max(s, axis=-1, keepdims=True), sink)
            e = jnp.exp(s - m)
            denom = jnp.sum(e, axis=-1, keepdims=True) + jnp.exp(sink - m)
            p = (e / denom).astype(BF16)
            o = jnp.dot(p, vg, preferred_element_type=F32)
            o_ref[0, :, h * HEAD_DIM:(h + 1) * HEAD_DIM] = o.astype(o_ref.dtype)


def _swa(proj3, sinks, slopes, q_gain, k_gain):
    b, s, _ = proj3.shape
    nb = s // WINDOW
    kvw = SWA_KV_HEADS * HEAD_DIM
    qw = SWA_HEADS * HEAD_DIM
    smem = pl.BlockSpec(memory_space=pltpu.SMEM)
    prev = lambda i, n: (i, jnp.maximum(n - 1, 0), COL_KA // kvw)
    cur = lambda i, n: (i, n, COL_KA // kvw)
    prev_v = lambda i, n: (i, jnp.maximum(n - 1, 0), COL_VA // kvw)
    cur_v = lambda i, n: (i, n, COL_VA // kvw)
    return pl.pallas_call(
        _swa_kernel,
        grid=(b, nb),
        in_specs=[smem, smem,
                  pl.BlockSpec((1, WINDOW, qw), lambda i, n: (i, n, COL_QA // qw)),
                  pl.BlockSpec((1, WINDOW, kvw), prev),
                  pl.BlockSpec((1, WINDOW, kvw), cur),
                  pl.BlockSpec((1, WINDOW, kvw), prev_v),
                  pl.BlockSpec((1, WINDOW, kvw), cur_v),
                  pl.BlockSpec((1, HEAD_DIM), lambda i, n: (0, 0)),
                  pl.BlockSpec((1, HEAD_DIM), lambda i, n: (0, 0))],
        out_specs=pl.BlockSpec((1, WINDOW, qw), lambda i, n: (i, n, 0)),
        out_shape=jax.ShapeDtypeStruct((b, s, qw), BF16),
        compiler_params=_cparams("parallel", "parallel"),
        name="swa_attn",
    )(sinks, slopes, proj3, proj3, proj3, proj3, proj3, q_gain, k_gain)


DSA_CHUNK = 256
ALIBI_SPLIT = 3
POS_LO_BITS = 7


def _alibi_columns(slopes):
    parts, rest = [], slopes.astype(F32)
    for _ in range(ALIBI_SPLIT):
        piece = rest.astype(BF16).astype(F32)
        parts.append(piece)
        rest = rest - piece
    cols = [p * float(2 ** POS_LO_BITS) for p in parts] + parts
    pad = jnp.zeros((slopes.shape[0], LANES - 2 * ALIBI_SPLIT), F32)
    return jnp.concatenate([jnp.stack(cols, axis=1), pad], axis=1)


def _dsa_kernel(qb_ref, qi_ref, sblk_ref, sfull_ref, acol_ref, qg_ref, kg_ref, o_ref,
                ki2, kx, vx, lhs_i, lhs_q, wb, key_s, keyt_s, mb_s, thr_s, acc_s, m_s, e_s, al_s, *, topk):
    i = pl.program_id(1)
    blk, ck = DSA_BLOCK, DSA_CHUNK
    nch = (i + 2) // 2
    seq = ki2.shape[0]
    max_ch = seq // ck
    half = lax.broadcasted_iota(I32, (blk, LANES), 1) < HEAD_DIM

    @pl.when(i == 0)
    def _():
        sf = sfull_ref[0]
        kib = sf[:, SM_KI:SM_KI + IDX_DIM].astype(BF16)
        ki2[...] = jnp.concatenate([kib, kib], axis=1)
        kn = _rms(sf[:, SM_KB:SM_KB + HEAD_DIM], kg_ref[...]).astype(BF16)
        pos = lax.broadcasted_iota(I32, (seq, LANES), 0)
        lane = lax.broadcasted_iota(I32, (seq, LANES), 1)
        hi = lax.shift_right_logical(pos, POS_LO_BITS).astype(F32)
        lo = (pos & (2 ** POS_LO_BITS - 1)).astype(F32)
        posc = jnp.where(lane < ALIBI_SPLIT, hi, jnp.where(lane < 2 * ALIBI_SPLIT, lo, 0.0)).astype(BF16)
        kx[...] = jnp.concatenate([kn, kn, posc], axis=1)
        ones = jnp.where(lax.broadcasted_iota(I32, (seq, HEAD_DIM), 1) == 0, 1.0, 0.0).astype(BF16)
        vx[...] = jnp.concatenate([sf[:, SM_VB:SM_VB + HEAD_DIM].astype(BF16), ones], axis=1)

    qi = qi_ref[0] * (IDX_DIM ** -0.5)
    q = qb_ref[0]
    g2 = jnp.concatenate([qg_ref[...], qg_ref[...]], axis=1)
    wi = sblk_ref[0][:, SM_WI:SM_WI + IDX_HEADS] * (IDX_HEADS ** -0.5)
    for p in range(DSA_HEADS // 2):
        qp = q[:, p * LANES:(p + 1) * LANES]
        sq = qp * qp
        ms_lo = jnp.sum(jnp.where(half, sq, 0.0), axis=-1, keepdims=True) / HEAD_DIM
        ms_hi = jnp.sum(jnp.where(half, 0.0, sq), axis=-1, keepdims=True) / HEAD_DIM
        qn = qp * lax.rsqrt(jnp.where(half, ms_lo, ms_hi) + RMS_EPS) * g2 * (HEAD_DIM ** -0.5)
        qip = qi[:, p * LANES:(p + 1) * LANES]
        for k in range(2):
            h = 2 * p + k
            keep = half if k == 0 else jnp.logical_not(half)
            rows = slice(h * blk, (h + 1) * blk)
            lhs_i[rows, :] = jnp.where(keep, qip, 0.0).astype(BF16)
            lhs_q[rows, :LANES] = jnp.where(keep, qn, 0.0).astype(BF16)
            lhs_q[rows, LANES:] = jnp.broadcast_to(acol_ref[h:h + 1, :], (blk, LANES)).astype(BF16)
            wb[h] = jnp.broadcast_to(wi[:, h:h + 1], (blk, LANES))

    row_t = i * blk + lax.broadcasted_iota(I32, (blk, ck), 0)
    col_k = lax.broadcasted_iota(I32, (blk, ck), 1)

    def score_chunk(c, carry):
        ks = pl.ds(pl.multiple_of(c * ck, ck), ck)
        logits = lax.dot_general(lhs_i[...], ki2[ks, :], NT_DIMS, preferred_element_type=F32)
        sc = jnp.zeros((blk, ck), F32)
        for h in range(IDX_HEADS):
            w = wb[h]
            lh = jnp.maximum(logits[h * blk:(h + 1) * blk, :], 0.0)
            sc = sc + lh * jnp.concatenate([w, w], axis=1)
        bits = lax.bitcast_convert_type(sc + 0.0, I32)
        key = jnp.where(bits < 0, bits ^ jnp.int32(0x7FFFFFFF), bits)
        key = jnp.where(c * ck + col_k <= row_t, key, jnp.int32(INT_MIN))
        key_s[c] = key
        keyt_s[c] = key.T
        return carry

    lax.fori_loop(0, nch, score_chunk, 0)

    thr_s[...] = jnp.full(thr_s.shape, INT_MIN, I32)
    for n in range(2, max_ch + 1):
        @pl.when(nch == n)
        def _(n=n):
            def step(it, thr):
                cand = thr + lax.shift_left(jnp.int32(1), 31 - it)
                cnt = jnp.zeros((1, blk), F32)
                for c in range(n):
                    cnt = cnt + jnp.sum(jnp.where(keyt_s[c] >= cand, 1.0, 0.0), axis=0, keepdims=True)
                return jnp.where(cnt >= float(topk), cand, thr)

            thr = lax.fori_loop(0, 32, step, jnp.full((1, blk), INT_MIN, I32))
            thr_s[...] = jnp.broadcast_to(thr, thr_s.shape).T

    thr = thr_s[...]
    thr2 = jnp.concatenate([thr, thr], axis=1)

    def count_gt(c, cnt):
        k = key_s[c]
        return cnt + jnp.where(k[:, :LANES] > thr, 1.0, 0.0) + jnp.where(k[:, LANES:] > thr, 1.0, 0.0)

    n_gt = jnp.sum(lax.fori_loop(0, nch, count_gt, jnp.zeros((blk, LANES), F32)), axis=-1, keepdims=True)
    need = float(topk) - n_gt
    upper = (lax.broadcasted_iota(I32, (ck, ck), 0) < lax.broadcasted_iota(I32, (ck, ck), 1)).astype(BF16)

    def mask_chunk(c, n_eq):
        k = key_s[c]
        eq = (k == thr2) & (c * ck + col_k <= row_t)
        eqf = eq.astype(BF16)
        before = jnp.dot(eqf, upper, preferred_element_type=F32) + n_eq
        sel = (k > thr2) | (eq & (before < need))
        mb_s[c] = jnp.where(sel, 0.0, NEG_INF)
        return n_eq + jnp.sum(eqf.astype(F32), axis=-1, keepdims=True)

    lax.fori_loop(0, nch, mask_chunk, jnp.zeros((blk, 1), F32))

    acc_s[...] = jnp.zeros_like(acc_s)
    m_s[...] = jnp.full(m_s.shape, -jnp.inf, F32)

    def attend(c, carry):
        ks = pl.ds(pl.multiple_of(c * ck, ck), ck)
        s_all = lax.dot_general(lhs_q[...], kx[ks, :], NT_DIMS, preferred_element_type=F32)
        mb = mb_s[c]
        for h in range(DSA_HEADS):
            rows = slice(h * blk, (h + 1) * blk)
            s = s_all[rows, :] + mb
            m_old = m_s[rows, :]
            m_new = jnp.maximum(m_old, jnp.max(s, axis=-1, keepdims=True))
            al_s[rows, :] = jnp.exp(m_old - m_new)
            e_s[rows, :] = jnp.exp(s - jnp.concatenate([m_new, m_new], axis=1)).astype(BF16)
            m_s[rows, :] = m_new
        pv = jnp.dot(e_s[...], vx[ks, :], preferred_element_type=F32)
        acc_s[...] = acc_s[...] * al_s[...] + pv
        return carry

    lax.fori_loop(0, nch, attend, 0)

    for h in range(DSA_HEADS):
        a = acc_s[h * blk:(h + 1) * blk, :]
        o = a[:, :HEAD_DIM] / a[:, HEAD_DIM:HEAD_DIM + 1]
        o_ref[0, :, h * HEAD_DIM:(h + 1) * HEAD_DIM] = o.astype(o_ref.dtype)


def _dsa(proj3, slopes, q_gain, k_gain):
    b, s, _ = proj3.shape
    nb = s // DSA_BLOCK
    qw = DSA_HEADS * HEAD_DIM
    smw = 256
    rows = DSA_HEADS * DSA_BLOCK
    fixed = lambda i, n: (0, 0)
    return pl.pallas_call(
        functools.partial(_dsa_kernel, topk=min(DSA_TOPK_MAX, s // 4)),
        grid=(b, nb),
        in_specs=[pl.BlockSpec((1, DSA_BLOCK, qw), lambda i, n: (i, n, COL_QB // qw)),
                  pl.BlockSpec((1, DSA_BLOCK, qw), lambda i, n: (i, n, COL_QI // qw)),
                  pl.BlockSpec((1, DSA_BLOCK, smw), lambda i, n: (i, n, COL_SMALL // smw)),
                  pl.BlockSpec((1, s, smw), lambda i, n: (i, 0, COL_SMALL // smw)),
                  pl.BlockSpec((DSA_HEADS, LANES), fixed),
                  pl.BlockSpec((1, HEAD_DIM), fixed),
                  pl.BlockSpec((1, HEAD_DIM), fixed)],
        out_specs=pl.BlockSpec((1, DSA_BLOCK, qw), lambda i, n: (i, n, 0)),
        out_shape=jax.ShapeDtypeStruct((b, s, qw), BF16),
        scratch_shapes=[pltpu.VMEM((s, LANES), BF16),
                        pltpu.VMEM((s, 2 * LANES), BF16),
                        pltpu.VMEM((s, LANES), BF16),
                        pltpu.VMEM((rows, LANES), BF16),
                        pltpu.VMEM((rows, 2 * LANES), BF16),
                        pltpu.VMEM((IDX_HEADS, DSA_BLOCK, LANES), F32),
                        pltpu.VMEM((s // DSA_CHUNK, DSA_BLOCK, DSA_CHUNK), I32),
                        pltpu.VMEM((s // DSA_CHUNK, DSA_CHUNK, DSA_BLOCK), I32),
                        pltpu.VMEM((s // DSA_CHUNK, DSA_BLOCK, DSA_CHUNK), F32),
                        pltpu.VMEM((DSA_BLOCK, LANES), I32),
                        pltpu.VMEM((rows, LANES), F32),
                        pltpu.VMEM((rows, LANES), F32),
                        pltpu.VMEM((rows, DSA_CHUNK), BF16),
                        pltpu.VMEM((rows, LANES), F32)],
        compiler_params=_cparams("arbitrary", "arbitrary"),
        name="dsa_attn",
    )(proj3, proj3, proj3, proj3, _alibi_columns(slopes), q_gain, k_gain)


def _merge_kernel(h_ref, oa_ref, ob_ref, wga_ref, wgb_ref, wa_ref, wb_ref, o_ref):
    h = h_ref[...]
    ga = jnp.dot(h, wga_ref[...], preferred_element_type=F32)
    gb = jnp.dot(h, wgb_ref[...], preferred_element_type=F32)
    a = jnp.dot(oa_ref[...], wa_ref[...], preferred_element_type=F32)
    b = jnp.dot(ob_ref[...], wb_ref[...], preferred_element_type=F32)
    o_ref[...] = (_sigmoid(ga) * a + _sigmoid(gb) * b).astype(o_ref.dtype)


def _merge(h, oa, ob, w_gate, wa, wb):
    m, d = h.shape
    tm, tn = 1024, 512
    ngt = d // tn
    return pl.pallas_call(
        _merge_kernel,
        grid=(m // tm, ngt),
        in_specs=[pl.BlockSpec((tm, d), lambda i, j: (i, 0)),
                  pl.BlockSpec((tm, oa.shape[1]), lambda i, j: (i, 0)),
                  pl.BlockSpec((tm, ob.shape[1]), lambda i, j: (i, 0)),
                  pl.BlockSpec((d, tn), lambda i, j: (0, j)),
                  pl.BlockSpec((d, tn), lambda i, j: (0, j + ngt)),
                  pl.BlockSpec((wa.shape[0], tn), lambda i, j: (0, j)),
                  pl.BlockSpec((wb.shape[0], tn), lambda i, j: (0, j))],
        out_specs=pl.BlockSpec((tm, tn), lambda i, j: (i, j)),
        out_shape=jax.ShapeDtypeStruct((m, d), BF16),
        compiler_params=_cparams("parallel", "parallel"),
        name="merge_branches",
    )(h, oa, ob, w_gate, w_gate, wa, wb)


MOE_TILE = 256
MOE_SLOTS = 2
TOKEN_TILE = 256


def _moe_max_tiles(n_tokens):
    return -(-(n_tokens * MOE_SLOTS + N_EXPERTS * (MOE_TILE - 1)) // MOE_TILE)


def _pack_bf16_pair(x):
    k = x.shape[1] // 2
    bits = lax.bitcast_convert_type(x.astype(BF16).astype(F32), I32)
    return lax.shift_right_logical(bits[:, :k], 16) | (bits[:, k:] & jnp.int32(-65536))


def _unpack_bf16_pair(w):
    lo = lax.bitcast_convert_type(lax.shift_left(w, 16), F32).astype(BF16)
    hi = lax.bitcast_convert_type(w & jnp.int32(-65536), F32).astype(BF16)
    return lo, hi


def _outproj_kernel(mg_ref, wo_ref, x_ref, gt_ref, g2_ref, sc_ref, sh_ref, wr_ref, br_ref,
                    x1_ref, h2p_ref, ids_ref, cw_ref):
    y = jnp.dot(mg_ref[0], wo_ref[...], preferred_element_type=F32)
    x1 = x_ref[0] + gt_ref[0] * y
    x1_ref[0] = x1
    h2 = _rms(x1, g2_ref[...]) * (1.0 + sc_ref[0]) + sh_ref[0]
    h2p_ref[0] = _pack_bf16_pair(h2)

    logits = jnp.dot(h2, wr_ref[...], preferred_element_type=F32,
                     precision=lax.Precision.HIGHEST) + br_ref[...]
    lane = lax.broadcasted_iota(I32, logits.shape, 1)
    is_g = (lane >= N_EXPERTS) & (lane < N_EXPERTS + N_GROUPS)
    neg = -jnp.inf
    gl = jnp.where(is_g, logits, neg)
    gmax = jnp.max(gl, axis=-1, keepdims=True)
    g_idx = jnp.min(jnp.where(gl == gmax, lane - N_EXPERTS, N_GROUPS), axis=-1, keepdims=True)
    p_g = 1.0 / jnp.sum(jnp.where(is_g, jnp.exp(gl - gmax), 0.0), axis=-1, keepdims=True)
    in_grp = (lane < N_EXPERTS) & ((lane // EXPERTS_PER_GROUP) == g_idx)
    el = jnp.where(in_grp, logits, neg)
    m1 = jnp.max(el, axis=-1, keepdims=True)
    i1 = jnp.min(jnp.where(el == m1, lane, LANES), axis=-1, keepdims=True)
    el2 = jnp.where(lane == i1, neg, el)
    m2 = jnp.max(el2, axis=-1, keepdims=True)
    i2 = jnp.min(jnp.where(el2 == m2, lane, LANES), axis=-1, keepdims=True)
    e2 = jnp.exp(m2 - m1)
    w1 = 1.0 / (1.0 + e2)
    w2 = e2 / (1.0 + e2)
    ids_ref[0] = jnp.where(lane == 0, i1, jnp.where(lane == 1, i2, 0))
    cw_ref[0] = p_g * jnp.where(lane == 0, w1, jnp.where(lane == 1, w2, 0.0))


def _outproj(merged3, w_out, x, gt1, g2, sc2, sh2, w_r, b_r):
    b, s, d = x.shape
    tm = 256
    row = lambda i, j: (i, j, 0)
    per_b = lambda i, j: (i, 0, 0)
    fixed = lambda i, j: (0, 0)
    return pl.pallas_call(
        _outproj_kernel,
        grid=(b, s // tm),
        in_specs=[pl.BlockSpec((1, tm, d), row),
                  pl.BlockSpec((d, d), fixed),
                  pl.BlockSpec((1, tm, d), row),
                  pl.BlockSpec((1, 1, d), per_b),
                  pl.BlockSpec((1, d), fixed),
                  pl.BlockSpec((1, 1, d), per_b),
                  pl.BlockSpec((1, 1, d), per_b),
                  pl.BlockSpec((d, LANES), fixed),
                  pl.BlockSpec((1, LANES), fixed)],
        out_specs=[pl.BlockSpec((1, tm, d), row),
                   pl.BlockSpec((1, tm, d // 2), row),
                   pl.BlockSpec((1, tm, LANES), row),
                   pl.BlockSpec((1, tm, LANES), row)],
        out_shape=[jax.ShapeDtypeStruct((b, s, d), F32),
                   jax.ShapeDtypeStruct((b, s, d // 2), I32),
                   jax.ShapeDtypeStruct((b, s, LANES), I32),
                   jax.ShapeDtypeStruct((b, s, LANES), F32)],
        compiler_params=_cparams("parallel", "parallel"),
        name="outproj_ln2_router",
    )(merged3, w_out, x, gt1, g2, sc2, sh2, w_r, b_r)


def _plan_kernel(ids_ref, dest_ref, te_ref, rank_ref):
    n = ids_ref.shape[0]
    cb = 256
    lane = lax.broadcasted_iota(I32, (cb, LANES), 1)
    lower = (lax.broadcasted_iota(I32, (cb, cb), 0) > lax.broadcasted_iota(I32, (cb, cb), 1)).astype(BF16)

    def count(bi, carry):
        rows = pl.ds(pl.multiple_of(bi * cb, cb), cb)
        ids = ids_ref[rows, :]
        m1 = lane == ids[:, 0:1]
        m2 = lane == ids[:, 1:2]
        onehot = m1 | m2
        before = jnp.dot(lower, onehot.astype(BF16), preferred_element_type=F32) + carry
        r1 = jnp.sum(jnp.where(m1, before, 0.0), axis=-1, keepdims=True)
        r2 = jnp.sum(jnp.where(m2, before, 0.0), axis=-1, keepdims=True)
        rank_ref[rows, :] = jnp.where(lane == 0, r1, jnp.where(lane == 1, r2, 0.0))
        return carry + jnp.sum(onehot.astype(F32), axis=0, keepdims=True)

    counts = lax.fori_loop(0, n // cb, count, jnp.zeros((1, LANES), F32))
    padded = jnp.floor((counts + (MOE_TILE - 1)) / MOE_TILE) * MOE_TILE
    upper = (lax.broadcasted_iota(I32, (LANES, LANES), 0)
             < lax.broadcasted_iota(I32, (LANES, LANES), 1)).astype(F32)
    offs = jnp.dot(jnp.broadcast_to(padded, (8, LANES)), upper, preferred_element_type=F32,
                   precision=lax.Precision.HIGHEST)[0:1]

    def place(bi, c):
        rows = pl.ds(pl.multiple_of(bi * cb, cb), cb)
        ids = ids_ref[rows, :]
        o1 = jnp.sum(jnp.where(lane == ids[:, 0:1], offs, 0.0), axis=-1, keepdims=True)
        o2 = jnp.sum(jnp.where(lane == ids[:, 1:2], offs, 0.0), axis=-1, keepdims=True)
        slot = rank_ref[rows, :] + jnp.where(lane == 0, o1, jnp.where(lane == 1, o2, 0.0))
        dest_ref[rows, :] = slot.astype(I32)
        return c

    lax.fori_loop(0, n // cb, place, 0)

    sq_row = lax.broadcasted_iota(I32, (LANES, LANES), 0)
    sq_lane = lax.broadcasted_iota(I32, (LANES, LANES), 1)
    tile_start = sq_row.astype(F32) * MOE_TILE
    finished = ((offs + padded) <= tile_start) & (sq_lane < N_EXPERTS)
    tile_expert = jnp.minimum(jnp.sum(finished.astype(F32), axis=-1, keepdims=True), N_EXPERTS - 1.0)
    n_tiles = jnp.sum(padded, axis=-1, keepdims=True) / MOE_TILE
    te_ref[...] = jnp.where(sq_lane == 1, n_tiles, tile_expert).astype(I32)


def _plan(ids):
    n = ids.shape[0]
    return pl.pallas_call(
        _plan_kernel,
        out_shape=[jax.ShapeDtypeStruct((n, LANES), I32),
                   jax.ShapeDtypeStruct((LANES, LANES), I32)],
        scratch_shapes=[pltpu.VMEM((n, LANES), F32)],
        compiler_params=pltpu.CompilerParams(vmem_limit_bytes=VMEM_LIMIT),
        name="moe_plan",
    )(ids)


def _dispatch_kernel(dest_ref, h_ref, hs_zero_ref, hs_ref, sem):
    del hs_zero_ref
    tq = h_ref.shape[0]

    def issue(r, c):
        for k in range(MOE_SLOTS):
            slot = dest_ref[0, 0, MOE_SLOTS * r + k]
            pltpu.make_async_copy(h_ref.at[pl.ds(r, 1)], hs_ref.at[pl.ds(slot, 1)], sem).start(priority=k)
        return c

    lax.fori_loop(0, tq, issue, 0, unroll=8)
    done = hs_ref.at[pl.ds(0, MOE_SLOTS * tq)]
    pltpu.make_async_copy(done, done, sem).wait()


def _dispatch(dest3, h2p, n_rows):
    n, w = h2p.shape
    tq = TOKEN_TILE
    hs_zero = jnp.zeros((n_rows, w), h2p.dtype)
    return pl.pallas_call(
        _dispatch_kernel,
        grid=(n // tq,),
        in_specs=[pl.BlockSpec((1, 1, MOE_SLOTS * tq), lambda i: (i, 0, 0), memory_space=pltpu.SMEM),
                  pl.BlockSpec((tq, w), lambda i: (i, 0)),
                  pl.BlockSpec(memory_space=pl.ANY)],
        out_specs=pl.BlockSpec(memory_space=pl.ANY),
        out_shape=jax.ShapeDtypeStruct((n_rows, w), h2p.dtype),
        scratch_shapes=[pltpu.SemaphoreType.DMA(())],
        input_output_aliases={2: 0},
        compiler_params=_cparams("arbitrary"),
        name="moe_dispatch",
    )(dest3, h2p, hs_zero)


def _expert_kernel(te_ref, nt_ref, hs_ref, wg_ref, wu_ref, wd_ref, ys_ref):
    del te_ref
    used = pl.program_id(0) < nt_ref[0]

    @pl.when(jnp.logical_not(used))
    def _():
        ys_ref[...] = jnp.zeros_like(ys_ref)

    @pl.when(used)
    def _():
        lo, hi = _unpack_bf16_pair(hs_ref[...])
        k = lo.shape[1]

        def up(w_ref):
            return (jnp.dot(lo, w_ref[0, :k, :].astype(BF16), preferred_element_type=F32)
                    + jnp.dot(hi, w_ref[0, k:, :].astype(BF16), preferred_element_type=F32))

        a = up(wg_ref)
        u = up(wu_ref)
        act = (a * _sigmoid(a) * u).astype(BF16)
        ys_ref[...] = jnp.dot(act, wd_ref[0].astype(BF16), preferred_element_type=F32)


def _experts(tile_expert, n_tiles, hs, wg, wu, wd):
    n_rows, hw = hs.shape
    d, f = wg.shape[1], wg.shape[2]
    row = lambda t, te, nt: (jnp.minimum(t, nt[0] - 1), 0)
    wsel = lambda t, te, nt: (te[t], 0, 0)
    return pl.pallas_call(
        _expert_kernel,
        grid_spec=pltpu.PrefetchScalarGridSpec(
            num_scalar_prefetch=2,
            grid=(n_rows // MOE_TILE,),
            in_specs=[pl.BlockSpec((MOE_TILE, hw), row),
                      pl.BlockSpec((1, d, f), wsel),
                      pl.BlockSpec((1, d, f), wsel),
                      pl.BlockSpec((1, f, d), wsel)],
            out_specs=pl.BlockSpec((MOE_TILE, d), lambda t, te, nt: (t, 0))),
        out_shape=jax.ShapeDtypeStruct((n_rows, d), F32),
        compiler_params=_cparams("arbitrary"),
        name="moe_experts",
    )(tile_expert, n_tiles, hs, wg, wu, wd)


def _combine_kernel(dest_ref, ys_ref, x1_ref, cw_ref, gt_ref, o_ref, buf, sem):
    tq = x1_ref.shape[1]

    def issue(r, c):
        for k in range(MOE_SLOTS):
            slot = dest_ref[0, 0, MOE_SLOTS * r + k]
            pltpu.make_async_copy(ys_ref.at[pl.ds(slot, 1)], buf.at[k, pl.ds(r, 1)], sem).start(priority=k)
        return c

    lax.fori_loop(0, tq, issue, 0, unroll=8)
    pltpu.make_async_copy(buf, buf, sem).wait()
    cw = cw_ref[0]
    y = cw[:, 0:1] * buf[0] + cw[:, 1:2] * buf[1]
    o_ref[0] = x1_ref[0] + gt_ref[0] * y


def _combine(dest3, ys, x1, cw, gt2):
    b, s, d = x1.shape
    tq = TOKEN_TILE
    spb = s // tq
    row = lambda i, j: (i, j, 0)
    return pl.pallas_call(
        _combine_kernel,
        grid=(b, spb),
        in_specs=[pl.BlockSpec((1, 1, MOE_SLOTS * tq), lambda i, j: (i * spb + j, 0, 0),
                               memory_space=pltpu.SMEM),
                  pl.BlockSpec(memory_space=pl.ANY),
                  pl.BlockSpec((1, tq, d), row),
                  pl.BlockSpec((1, tq, LANES), row),
                  pl.BlockSpec((1, 1, d), lambda i, j: (i, 0, 0))],
        out_specs=pl.BlockSpec((1, tq, d), row),
        out_shape=jax.ShapeDtypeStruct((b, s, d), F32),
        scratch_shapes=[pltpu.VMEM((MOE_SLOTS, tq, d), F32), pltpu.SemaphoreType.DMA(())],
        compiler_params=_cparams("arbitrary", "arbitrary"),
        name="moe_combine",
    )(dest3, ys, x1, cw, gt2)


def _moe(h2p, ids, cw, wg, wu, wd, x1, gt2):
    b, s, d = x1.shape
    n = b * s
    n_rows = _moe_max_tiles(n) * MOE_TILE
    dest, te = _plan(ids.reshape(n, LANES))
    dest3 = dest[:, :MOE_SLOTS].reshape(n // TOKEN_TILE, 1, MOE_SLOTS * TOKEN_TILE)
    hs = _dispatch(dest3, h2p.reshape(n, d // 2), n_rows)
    ys = _experts(te[:n_rows // MOE_TILE, 0], te[0, 1:2], hs, wg, wu, wd)
    return _combine(dest3, ys, x1, cw, gt2)


def _alibi_slopes():
    n = SWA_HEADS + DSA_HEADS
    i = jnp.arange(1, n + 1, dtype=F32)
    return jnp.exp2(-8.0 * i / n)


def _layer(x, mod, ln1_g, w_in, swa_q_norm, swa_k_norm, swa_sinks, dsa_q_norm, dsa_k_norm,
           w_branch_a, w_branch_b, w_out, ln2_g, w_group, b_group, w_expert, b_expert,
           w_gate_up, w_up, w_down, slopes):
    b, s, d = x.shape
    sh1, sc1, gt1, sh2, sc2, gt2 = [m[:, None, :] for m in jnp.split(mod, 6, axis=-1)]

    o = 0
    seg = {}
    for name, width in (("qa", 1024), ("ka", 256), ("va", 256), ("qb", 1024), ("kb", 64), ("vb", 64),
                        ("qi", 1024), ("ki", 64), ("wi", 16), ("g", 2 * d)):
        seg[name] = (o, o + width)
        o += width
    cols = lambda nm: w_in[:, seg[nm][0]:seg[nm][1]]
    pad = jnp.zeros((d, ATT_WIDTH - COL_SMALL - 208), w_in.dtype)
    w_att = jnp.concatenate([cols("qa"), cols("qb"), cols("qi"), cols("ka"), cols("va"),
                             cols("kb"), cols("vb"), cols("ki"), cols("wi"), pad], axis=1).astype(BF16)
    w_gate = cols("g").astype(BF16)

    h1 = _ln_mod(x, ln1_g[None, :], sc1, sh1)
    h1f = h1.reshape(b * s, d)
    proj = _mm(h1f, w_att, 1024, 768, F32).reshape(b, s, ATT_WIDTH)
    oa = _swa(proj, swa_sinks, slopes[:SWA_HEADS], swa_q_norm[None, :], swa_k_norm[None, :])
    ob = _dsa(proj, slopes[SWA_HEADS:], dsa_q_norm[None, :], dsa_k_norm[None, :])
    merged = _merge(h1f, oa.reshape(b * s, -1), ob.reshape(b * s, -1), w_gate,
                    w_branch_a.astype(BF16), w_branch_b.astype(BF16))

    w_r = jnp.concatenate([w_expert, w_group,
                           jnp.zeros((d, LANES - N_EXPERTS - N_GROUPS), F32)], axis=1)
    b_r = jnp.concatenate([b_expert, b_group,
                           jnp.zeros((LANES - N_EXPERTS - N_GROUPS,), F32)])[None, :]
    x1, h2p, ids, cw = _outproj(merged.reshape(b, s, d), w_out.astype(BF16), x, gt1,
                                ln2_g[None, :], sc2, sh2, w_r, b_r)
    f = w_gate_up.shape[-1]
    return _moe(h2p, ids, cw, w_gate_up.reshape(N_EXPERTS, d, f), w_up.reshape(N_EXPERTS, d, f),
                w_down.reshape(N_EXPERTS, f, d), x1, gt2)


def kernel(x, c, w_ada, b_ada, ln1_g, w_in, swa_q_norm, swa_k_norm, swa_sinks, dsa_q_norm, dsa_k_norm,
           w_branch_a, w_branch_b, w_out, ln2_g, w_group, b_group, w_expert, b_expert,
           w_gate_up, w_up, w_down):
    slopes = _alibi_slopes()
    bsz = c.shape[0]
    c_pad = jnp.concatenate([c, jnp.zeros((8 - bsz, c.shape[1]), c.dtype)], axis=0)
    for l in range(w_ada.shape[0]):
        mod = _ada(c_pad, w_ada[l], b_ada[l][None, :])[:bsz]
        x = _layer(x, mod, ln1_g[l], w_in[l], swa_q_norm[l], swa_k_norm[l], swa_sinks[l],
                   dsa_q_norm[l], dsa_k_norm[l], w_branch_a[l], w_branch_b[l], w_out[l], ln2_g[l],
                   w_group[l], b_group[l], w_expert[l], b_expert[l],
                   w_gate_up[l], w_up[l], w_down[l], slopes)
    return x
```

```python
import functools

import jax
import jax.numpy as jnp
from jax import lax
from jax.experimental import pallas as pl
from jax.experimental.pallas import tpu as pltpu

F32 = jnp.float32
BF16 = jnp.bfloat16
I32 = jnp.int32

D_MODEL = 2048
HEAD_DIM = 64
SWA_HEADS = 16
SWA_KV_HEADS = 4
SWA_GROUP = SWA_HEADS // SWA_KV_HEADS
WINDOW = 128
DSA_HEADS = 16
IDX_HEADS = 16
IDX_DIM = 64
DSA_TOPK_MAX = 256
DSA_BLOCK = 128
N_GROUPS = 4
EXPERTS_PER_GROUP = 8
N_EXPERTS = N_GROUPS * EXPERTS_PER_GROUP
D_EXPERT = 512
RMS_EPS = 1e-6
NEG_INF = -1e30
INT_MIN = -(2 ** 31)

LANES = 128
VMEM_LIMIT = 56 * 1024 * 1024

COL_QA, COL_QB, COL_QI = 0, 1024, 2048
COL_KA, COL_VA, COL_SMALL = 3072, 3328, 3584
ATT_WIDTH = 3840
SM_KB, SM_VB, SM_KI, SM_WI = 0, 64, 128, 192

NT_DIMS = (((1,), (1,)), ((), ()))


def _cparams(*sem):
    return pltpu.CompilerParams(dimension_semantics=sem, vmem_limit_bytes=VMEM_LIMIT)


def _sigmoid(x):
    return 1.0 / (1.0 + jnp.exp(-x))


def _rms(x, g):
    return x * lax.rsqrt(jnp.mean(x * x, axis=-1, keepdims=True) + RMS_EPS) * g


def _ada_kernel(c_ref, w_ref, b_ref, o_ref):
    c = c_ref[...]
    cond = c * _sigmoid(c)
    o_ref[...] = jnp.dot(cond, w_ref[...], preferred_element_type=F32,
                         precision=lax.Precision.HIGHEST) + b_ref[...]


def _ada(c_pad, w, b):
    n = w.shape[1]
    tn = 1024
    return pl.pallas_call(
        _ada_kernel,
        grid=(n // tn,),
        in_specs=[pl.BlockSpec((8, D_MODEL), lambda j: (0, 0)),
                  pl.BlockSpec((D_MODEL, tn), lambda j: (0, j)),
                  pl.BlockSpec((1, tn), lambda j: (0, j))],
        out_specs=pl.BlockSpec((8, tn), lambda j: (0, j)),
        out_shape=jax.ShapeDtypeStruct((8, n), F32),
        compiler_params=_cparams("parallel"),
        name="ada_mod",
    )(c_pad, w, b)


def _ln_mod_kernel(x_ref, g_ref, sc_ref, sh_ref, o_ref):
    y = _rms(x_ref[0], g_ref[...])
    o_ref[0] = (y * (1.0 + sc_ref[0]) + sh_ref[0]).astype(o_ref.dtype)


def _ln_mod(x, g, sc, sh):
    b, s, d = x.shape
    ts = 512
    return pl.pallas_call(
        _ln_mod_kernel,
        grid=(b, s // ts),
        in_specs=[pl.BlockSpec((1, ts, d), lambda i, j: (i, j, 0)),
                  pl.BlockSpec((1, d), lambda i, j: (0, 0)),
                  pl.BlockSpec((1, 1, d), lambda i, j: (i, 0, 0)),
                  pl.BlockSpec((1, 1, d), lambda i, j: (i, 0, 0))],
        out_specs=pl.BlockSpec((1, ts, d), lambda i, j: (i, j, 0)),
        out_shape=jax.ShapeDtypeStruct((b, s, d), BF16),
        compiler_params=_cparams("parallel", "parallel"),
        name="ln_mod",
    )(x, g, sc, sh)


def _mm_kernel(a_ref, w_ref, o_ref):
    o_ref[...] = jnp.dot(a_ref[...], w_ref[...], preferred_element_type=F32).astype(o_ref.dtype)


def _mm(a, w, tm, tn, out_dtype):
    m, k = a.shape
    n = w.shape[1]
    return pl.pallas_call(
        _mm_kernel,
        grid=(m // tm, n // tn),
        in_specs=[pl.BlockSpec((tm, k), lambda i, j: (i, 0)),
                  pl.BlockSpec((k, tn), lambda i, j: (0, j))],
        out_specs=pl.BlockSpec((tm, tn), lambda i, j: (i, j)),
        out_shape=jax.ShapeDtypeStruct((m, n), out_dtype),
        compiler_params=_cparams("parallel", "parallel"),
        name="proj_att",
    )(a, w)


def _pair_rms(xp, half, gain2):
    sq = xp * xp
    ms_lo = jnp.sum(jnp.where(half, sq, 0.0), axis=-1, keepdims=True) / HEAD_DIM
    ms_hi = jnp.sum(jnp.where(half, 0.0, sq), axis=-1, keepdims=True) / HEAD_DIM
    return xp * lax.rsqrt(jnp.where(half, ms_lo, ms_hi) + RMS_EPS) * gain2


def _swa_kernel(sink_ref, slope_ref, q_ref, kp_ref, kc_ref, vp_ref, vc_ref, qg_ref, kg_ref, o_ref, lhs, e_s):
    n = pl.program_id(1)
    w, w2 = WINDOW, 2 * WINDOW
    q = q_ref[0]
    kw = jnp.concatenate([kp_ref[0], kc_ref[0]], axis=0)
    vw = jnp.concatenate([vp_ref[0], vc_ref[0]], axis=0)
    gq2 = jnp.concatenate([qg_ref[...], qg_ref[...]], axis=1)
    gk2 = jnp.concatenate([kg_ref[...], kg_ref[...]], axis=1)
    half_q = lax.broadcasted_iota(I32, (w, LANES), 1) < HEAD_DIM
    half_k = lax.broadcasted_iota(I32, (w2, LANES), 1) < HEAD_DIM
    row = lax.broadcasted_iota(I32, (w, w2), 0)
    col = lax.broadcasted_iota(I32, (w, w2), 1)
    rel = row + w - col
    valid = (rel >= 0) & (rel < w) & ((col >= w) | (n > 0))
    mbias = jnp.where(valid, 0.0, NEG_INF)
    relf = rel.astype(F32)

    ones_col = jnp.ones((w2, LANES), BF16)
    for g in range(SWA_KV_HEADS):
        if g % 2 == 0:
            grp = slice((g // 2) * LANES, (g // 2 + 1) * LANES)
            kn = _pair_rms(kw[:, grp], half_k, gk2)
            kn_sw = pltpu.roll(kn, HEAD_DIM, axis=1)
            vp = vw[:, grp]
            vp_sw = pltpu.roll(vp, HEAD_DIM, axis=1)
        kdup = jnp.where(half_k, kn, kn_sw) if g % 2 == 0 else jnp.where(half_k, kn_sw, kn)
        vdup = jnp.where(half_k, vp, vp_sw) if g % 2 == 0 else jnp.where(half_k, vp_sw, vp)
        vx = jnp.concatenate([vdup.astype(BF16), ones_col], axis=1)
        for r in range(SWA_GROUP):
            h = g * SWA_GROUP + r
            if r % 2 == 0:
                qn = _pair_rms(q[:, (h // 2) * LANES:(h // 2 + 1) * LANES], half_q, gq2) * (HEAD_DIM ** -0.5)
            keep = half_q if r % 2 == 0 else jnp.logical_not(half_q)
            lhs[r * w:(r + 1) * w, :] = jnp.where(keep, qn, 0.0).astype(BF16)
        s_all = lax.dot_general(lhs[...], kdup.astype(BF16), NT_DIMS, preferred_element_type=F32)
        sink_e = []
        for r in range(SWA_GROUP):
            h = g * SWA_GROUP + r
            rows = slice(r * w, (r + 1) * w)
            s = s_all[rows, :] + (mbias - slope_ref[h] * relf)
            sink = jnp.full((w, LANES), sink_ref[h], F32)
            m = jnp.maximum(jnp.max(s, axis=-1, keepdims=True), sink)
            e_s[rows, :] = jnp.exp(s - jnp.concatenate([m, m], axis=1)).astype(BF16)
            sink_e.append(jnp.exp(sink - m))
        pv = jnp.dot(e_s[...], vx, preferred_element_type=F32)
        for r in range(0, SWA_GROUP, 2):
            h = g * SWA_GROUP + r
            ev, od = pv[r * w:(r + 1) * w, :], pv[(r + 1) * w:(r + 2) * w, :]
            den = jnp.where(half_q, ev[:, LANES:] + sink_e[r], od[:, LANES:] + sink_e[r + 1])
            o = jnp.where(half_q, ev[:, :LANES], od[:, :LANES]) / den
            o_ref[0, :, (h // 2) * LANES:(h // 2 + 1) * LANES] = o.astype(o_ref.dtype)


def _swa(proj3, sinks, slopes, q_gain, k_gain):
    b, s, _ = proj3.shape
    nb = s // WINDOW
    kvw = SWA_KV_HEADS * HEAD_DIM
    qw = SWA_HEADS * HEAD_DIM
    smem = pl.BlockSpec(memory_space=pltpu.SMEM)
    fixed = lambda i, n: (0, 0)
    prev = lambda i, n: (i, jnp.maximum(n - 1, 0), COL_KA // kvw)
    cur = lambda i, n: (i, n, COL_KA // kvw)
    prev_v = lambda i, n: (i, jnp.maximum(n - 1, 0), COL_VA // kvw)
    cur_v = lambda i, n: (i, n, COL_VA // kvw)
    return pl.pallas_call(
        _swa_kernel,
        grid=(b, nb),
        in_specs=[smem, smem,
                  pl.BlockSpec((1, WINDOW, qw), lambda i, n: (i, n, COL_QA // qw)),
                  pl.BlockSpec((1, WINDOW, kvw), prev),
                  pl.BlockSpec((1, WINDOW, kvw), cur),
                  pl.BlockSpec((1, WINDOW, kvw), prev_v),
                  pl.BlockSpec((1, WINDOW, kvw), cur_v),
                  pl.BlockSpec((1, HEAD_DIM), fixed),
                  pl.BlockSpec((1, HEAD_DIM), fixed)],
        out_specs=pl.BlockSpec((1, WINDOW, qw), lambda i, n: (i, n, 0)),
        out_shape=jax.ShapeDtypeStruct((b, s, qw), BF16),
        scratch_shapes=[pltpu.VMEM((SWA_GROUP * WINDOW, LANES), BF16),
                        pltpu.VMEM((SWA_GROUP * WINDOW, 2 * WINDOW), BF16)],
        compiler_params=_cparams("parallel", "parallel"),
        name="swa_attn",
    )(sinks, slopes, proj3, proj3, proj3, proj3, proj3, q_gain, k_gain)


DSA_CHUNK = 256
ALIBI_SPLIT = 3
POS_LO_BITS = 7


def _bf16_head(x):
    return lax.bitcast_convert_type(lax.bitcast_convert_type(x, I32) & jnp.int32(-65536), F32)


def _alibi_columns(slopes):
    parts, rest = [], slopes.astype(F32)
    for _ in range(ALIBI_SPLIT):
        piece = _bf16_head(rest)
        parts.append(piece)
        rest = rest - piece
    cols = [p * float(2 ** POS_LO_BITS) for p in parts] + parts
    pad = jnp.zeros((slopes.shape[0], LANES - 2 * ALIBI_SPLIT), F32)
    return jnp.concatenate([jnp.stack(cols, axis=1), pad], axis=1)


def _dsa_kernel(qb_ref, qi_ref, sblk_ref, sfull_ref, acol_ref, qg_ref, kg_ref, o_ref,
                ki2, kx, vx, lhs_i, lhs_q, wb, key_s, keyt_s, mb_s, thr_s, acc_s, m_s, e_s, al_s, *, topk):
    i = pl.program_id(1)
    blk, ck = DSA_BLOCK, DSA_CHUNK
    nch = (i + 2) // 2
    seq = ki2.shape[0]
    max_ch = seq // ck
    half = lax.broadcasted_iota(I32, (blk, LANES), 1) < HEAD_DIM

    @pl.when(i == 0)
    def _():
        sf = sfull_ref[0]
        kib = sf[:, SM_KI:SM_KI + IDX_DIM].astype(BF16)
        ki2[...] = jnp.concatenate([kib, kib], axis=1)
        kn = _rms(sf[:, SM_KB:SM_KB + HEAD_DIM], kg_ref[...]).astype(BF16)
        pos = lax.broadcasted_iota(I32, (seq, LANES), 0)
        lane = lax.broadcasted_iota(I32, (seq, LANES), 1)
        hi = lax.shift_right_logical(pos, POS_LO_BITS).astype(F32)
        lo = (pos & (2 ** POS_LO_BITS - 1)).astype(F32)
        posc = jnp.where(lane < ALIBI_SPLIT, hi, jnp.where(lane < 2 * ALIBI_SPLIT, lo, 0.0)).astype(BF16)
        kx[...] = jnp.concatenate([kn, kn, posc], axis=1)
        ones = jnp.where(lax.broadcasted_iota(I32, (seq, HEAD_DIM), 1) == 0, 1.0, 0.0).astype(BF16)
        vx[...] = jnp.concatenate([sf[:, SM_VB:SM_VB + HEAD_DIM].astype(BF16), ones], axis=1)

    qi = qi_ref[0] * (IDX_DIM ** -0.5)
    q = qb_ref[0]
    g2 = jnp.concatenate([qg_ref[...], qg_ref[...]], axis=1)
    wi = sblk_ref[0][:, SM_WI:SM_WI + IDX_HEADS] * (IDX_HEADS ** -0.5)
    for p in range(DSA_HEADS // 2):
        qn = _pair_rms(q[:, p * LANES:(p + 1) * LANES], half, g2) * (HEAD_DIM ** -0.5)
        qip = qi[:, p * LANES:(p + 1) * LANES]
        for k in range(2):
            h = 2 * p + k
            keep = half if k == 0 else jnp.logical_not(half)
            rows = slice(h * blk, (h + 1) * blk)
            lhs_i[rows, :] = jnp.where(keep, qip, 0.0).astype(BF16)
            lhs_q[rows, :LANES] = jnp.where(keep, qn, 0.0).astype(BF16)
            lhs_q[rows, LANES:] = jnp.broadcast_to(acol_ref[h:h + 1, :], (blk, LANES)).astype(BF16)
            wb[h] = jnp.broadcast_to(wi[:, h:h + 1], (blk, LANES))

    row_t = i * blk + lax.broadcasted_iota(I32, (blk, ck), 0)
    col_k = lax.broadcasted_iota(I32, (blk, ck), 1)

    def score_chunk(c, carry):
        ks = pl.ds(pl.multiple_of(c * ck, ck), ck)
        logits = lax.dot_general(lhs_i[...], ki2[ks, :], NT_DIMS, preferred_element_type=F32)
        sc = jnp.zeros((blk, ck), F32)
        for h in range(IDX_HEADS):
            w = wb[h]
            lh = jnp.maximum(logits[h * blk:(h + 1) * blk, :], 0.0)
            sc = sc + lh * jnp.concatenate([w, w], axis=1)
        bits = lax.bitcast_convert_type(sc + 0.0, I32)
        key = jnp.where(bits < 0, bits ^ jnp.int32(0x7FFFFFFF), bits)
        key = jnp.where(c * ck + col_k <= row_t, key, jnp.int32(INT_MIN))
        key_s[c] = key
        keyt_s[c] = key.T
        return carry

    lax.fori_loop(0, nch, score_chunk, 0)

    thr_s[...] = jnp.full(thr_s.shape, INT_MIN, I32)
    for n in range(2, max_ch + 1):
        @pl.when(nch == n)
        def _(n=n):
            def step(it, thr):
                cand = thr + lax.shift_left(jnp.int32(1), 31 - it)
                cnt = jnp.zeros((1, blk), F32)
                for c in range(n):
                    cnt = cnt + jnp.sum(jnp.where(keyt_s[c] >= cand, 1.0, 0.0), axis=0, keepdims=True)
                return jnp.where(cnt >= float(topk), cand, thr)

            thr = lax.fori_loop(0, 32, step, jnp.full((1, blk), INT_MIN, I32))
            thr_s[...] = jnp.broadcast_to(thr, thr_s.shape).T

    thr = thr_s[...]
    thr2 = jnp.concatenate([thr, thr], axis=1)

    def count_gt(c, cnt):
        k = key_s[c]
        return cnt + jnp.where(k[:, :LANES] > thr, 1.0, 0.0) + jnp.where(k[:, LANES:] > thr, 1.0, 0.0)

    n_gt = jnp.sum(lax.fori_loop(0, nch, count_gt, jnp.zeros((blk, LANES), F32)), axis=-1, keepdims=True)
    need = float(topk) - n_gt
    upper = (lax.broadcasted_iota(I32, (ck, ck), 0) < lax.broadcasted_iota(I32, (ck, ck), 1)).astype(BF16)

    def mask_chunk(c, n_eq):
        k = key_s[c]
        eq = (k == thr2) & (c * ck + col_k <= row_t)
        eqf = eq.astype(BF16)
        before = jnp.dot(eqf, upper, preferred_element_type=F32) + n_eq
        sel = (k > thr2) | (eq & (before < need))
        mb_s[c] = jnp.where(sel, 0.0, NEG_INF)
        return n_eq + jnp.sum(eqf.astype(F32), axis=-1, keepdims=True)

    lax.fori_loop(0, nch, mask_chunk, jnp.zeros((blk, 1), F32))

    acc_s[...] = jnp.zeros_like(acc_s)
    m_s[...] = jnp.full(m_s.shape, -jnp.inf, F32)

    def attend(c, carry):
        ks = pl.ds(pl.multiple_of(c * ck, ck), ck)
        s_all = lax.dot_general(lhs_q[...], kx[ks, :], NT_DIMS, preferred_element_type=F32)
        mb = mb_s[c]
        for h in range(DSA_HEADS):
            rows = slice(h * blk, (h + 1) * blk)
            s = s_all[rows, :] + mb
            m_old = m_s[rows, :]
            m_new = jnp.maximum(m_old, jnp.max(s, axis=-1, keepdims=True))
            al_s[rows, :] = jnp.exp(m_old - m_new)
            e_s[rows, :] = jnp.exp(s - jnp.concatenate([m_new, m_new], axis=1)).astype(BF16)
            m_s[rows, :] = m_new
        pv = jnp.dot(e_s[...], vx[ks, :], preferred_element_type=F32)
        acc_s[...] = acc_s[...] * al_s[...] + pv
        return carry

    lax.fori_loop(0, nch, attend, 0)

    for h in range(DSA_HEADS):
        a = acc_s[h * blk:(h + 1) * blk, :]
        o = a[:, :HEAD_DIM] / a[:, HEAD_DIM:HEAD_DIM + 1]
        o_ref[0, :, h * HEAD_DIM:(h + 1) * HEAD_DIM] = o.astype(o_ref.dtype)


def _dsa(proj3, slopes, q_gain, k_gain):
    b, s, _ = proj3.shape
    nb = s // DSA_BLOCK
    qw = DSA_HEADS * HEAD_DIM
    smw = 256
    rows = DSA_HEADS * DSA_BLOCK
    fixed = lambda i, n: (0, 0)
    return pl.pallas_call(
        functools.partial(_dsa_kernel, topk=min(DSA_TOPK_MAX, s // 4)),
        grid=(b, nb),
        in_specs=[pl.BlockSpec((1, DSA_BLOCK, qw), lambda i, n: (i, n, COL_QB // qw)),
                  pl.BlockSpec((1, DSA_BLOCK, qw), lambda i, n: (i, n, COL_QI // qw)),
                  pl.BlockSpec((1, DSA_BLOCK, smw), lambda i, n: (i, n, COL_SMALL // smw)),
                  pl.BlockSpec((1, s, smw), lambda i, n: (i, 0, COL_SMALL // smw)),
                  pl.BlockSpec((DSA_HEADS, LANES), fixed),
                  pl.BlockSpec((1, HEAD_DIM), fixed),
                  pl.BlockSpec((1, HEAD_DIM), fixed)],
        out_specs=pl.BlockSpec((1, DSA_BLOCK, qw), lambda i, n: (i, n, 0)),
        out_shape=jax.ShapeDtypeStruct((b, s, qw), BF16),
        scratch_shapes=[pltpu.VMEM((s, LANES), BF16),
                        pltpu.VMEM((s, 2 * LANES), BF16),
                        pltpu.VMEM((s, LANES), BF16),
                        pltpu.VMEM((rows, LANES), BF16),
                        pltpu.VMEM((rows, 2 * LANES), BF16),
                        pltpu.VMEM((IDX_HEADS, DSA_BLOCK, LANES), F32),
                        pltpu.VMEM((s // DSA_CHUNK, DSA_BLOCK, DSA_CHUNK), I32),
                        pltpu.VMEM((s // DSA_CHUNK, DSA_CHUNK, DSA_BLOCK), I32),
                        pltpu.VMEM((s // DSA_CHUNK, DSA_BLOCK, DSA_CHUNK), F32),
                        pltpu.VMEM((DSA_BLOCK, LANES), I32),
                        pltpu.VMEM((rows, LANES), F32),
                        pltpu.VMEM((rows, LANES), F32),
                        pltpu.VMEM((rows, DSA_CHUNK), BF16),
                        pltpu.VMEM((rows, LANES), F32)],
        compiler_params=_cparams("arbitrary", "arbitrary"),
        name="dsa_attn",
    )(proj3, proj3, proj3, proj3, _alibi_columns(slopes), q_gain, k_gain)


def _merge_kernel(h_ref, oa_ref, ob_ref, wga_ref, wgb_ref, wa_ref, wb_ref, o_ref):
    h = h_ref[...]
    ga = jnp.dot(h, wga_ref[...], preferred_element_type=F32)
    gb = jnp.dot(h, wgb_ref[...], preferred_element_type=F32)
    a = jnp.dot(oa_ref[...], wa_ref[...], preferred_element_type=F32)
    b = jnp.dot(ob_ref[...], wb_ref[...], preferred_element_type=F32)
    o_ref[...] = (_sigmoid(ga) * a + _sigmoid(gb) * b).astype(o_ref.dtype)


def _merge(h, oa, ob, w_gate, wa, wb):
    m, d = h.shape
    tm, tn = 1024, 512
    ngt = d // tn
    return pl.pallas_call(
        _merge_kernel,
        grid=(m // tm, ngt),
        in_specs=[pl.BlockSpec((tm, d), lambda i, j: (i, 0)),
                  pl.BlockSpec((tm, oa.shape[1]), lambda i, j: (i, 0)),
                  pl.BlockSpec((tm, ob.shape[1]), lambda i, j: (i, 0)),
                  pl.BlockSpec((d, tn), lambda i, j: (0, j)),
                  pl.BlockSpec((d, tn), lambda i, j: (0, j + ngt)),
                  pl.BlockSpec((wa.shape[0], tn), lambda i, j: (0, j)),
                  pl.BlockSpec((wb.shape[0], tn), lambda i, j: (0, j))],
        out_specs=pl.BlockSpec((tm, tn), lambda i, j: (i, j)),
        out_shape=jax.ShapeDtypeStruct((m, d), BF16),
        compiler_params=_cparams("parallel", "parallel"),
        name="merge_branches",
    )(h, oa, ob, w_gate, w_gate, wa, wb)


MOE_TILE = 256
MOE_SLOTS = 2
TOKEN_TILE = 256


def _moe_max_tiles(n_tokens):
    return -(-(n_tokens * MOE_SLOTS + N_EXPERTS * (MOE_TILE - 1)) // MOE_TILE)


def _pack_bf16_pair(x):
    k = x.shape[1] // 2
    bits = lax.bitcast_convert_type(x.astype(BF16).astype(F32), I32)
    return lax.shift_right_logical(bits[:, :k], 16) | (bits[:, k:] & jnp.int32(-65536))


def _unpack_bf16_pair(w):
    lo = lax.bitcast_convert_type(lax.shift_left(w, 16), F32).astype(BF16)
    hi = lax.bitcast_convert_type(w & jnp.int32(-65536), F32).astype(BF16)
    return lo, hi


def _outproj_kernel(mg_ref, wo_ref, x_ref, gt_ref, g2_ref, sc_ref, sh_ref, wr_ref, br_ref,
                    x1_ref, h2p_ref, ids_ref, cw_ref):
    y = jnp.dot(mg_ref[0], wo_ref[...], preferred_element_type=F32)
    x1 = x_ref[0] + gt_ref[0] * y
    x1_ref[0] = x1
    h2 = _rms(x1, g2_ref[...]) * (1.0 + sc_ref[0]) + sh_ref[0]
    h2p_ref[0] = _pack_bf16_pair(h2)

    h_hi = h2.astype(BF16)
    h_lo = (h2 - h_hi.astype(F32)).astype(BF16)
    logits = (jnp.dot(h_hi, wr_ref[0], preferred_element_type=F32)
              + jnp.dot(h_lo, wr_ref[0], preferred_element_type=F32)
              + jnp.dot(h_hi, wr_ref[1], preferred_element_type=F32)) + br_ref[...]
    lane = lax.broadcasted_iota(I32, logits.shape, 1)
    is_g = (lane >= N_EXPERTS) & (lane < N_EXPERTS + N_GROUPS)
    neg = -jnp.inf
    gl = jnp.where(is_g, logits, neg)
    gmax = jnp.max(gl, axis=-1, keepdims=True)
    g_idx = jnp.min(jnp.where(gl == gmax, lane - N_EXPERTS, N_GROUPS), axis=-1, keepdims=True)
    p_g = 1.0 / jnp.sum(jnp.where(is_g, jnp.exp(gl - gmax), 0.0), axis=-1, keepdims=True)
    in_grp = (lane < N_EXPERTS) & ((lane // EXPERTS_PER_GROUP) == g_idx)
    el = jnp.where(in_grp, logits, neg)
    m1 = jnp.max(el, axis=-1, keepdims=True)
    i1 = jnp.min(jnp.where(el == m1, lane, LANES), axis=-1, keepdims=True)
    el2 = jnp.where(lane == i1, neg, el)
    m2 = jnp.max(el2, axis=-1, keepdims=True)
    i2 = jnp.min(jnp.where(el2 == m2, lane, LANES), axis=-1, keepdims=True)
    e2 = jnp.exp(m2 - m1)
    w1 = 1.0 / (1.0 + e2)
    w2 = e2 / (1.0 + e2)
    ids_ref[0] = jnp.where(lane == 0, i1, jnp.where(lane == 1, i2, 0))
    cw_ref[0] = p_g * jnp.where(lane == 0, w1, jnp.where(lane == 1, w2, 0.0))


def _outproj(merged3, w_out, x, gt1, g2, sc2, sh2, w_r, b_r):
    b, s, d = x.shape
    tm = 256
    row = lambda i, j: (i, j, 0)
    per_b = lambda i, j: (i, 0, 0)
    fixed = lambda i, j: (0, 0)
    return pl.pallas_call(
        _outproj_kernel,
        grid=(b, s // tm),
        in_specs=[pl.BlockSpec((1, tm, d), row),
                  pl.BlockSpec((d, d), fixed),
                  pl.BlockSpec((1, tm, d), row),
                  pl.BlockSpec((1, 1, d), per_b),
                  pl.BlockSpec((1, d), fixed),
                  pl.BlockSpec((1, 1, d), per_b),
                  pl.BlockSpec((1, 1, d), per_b),
                  pl.BlockSpec((2, d, LANES), lambda i, j: (0, 0, 0)),
                  pl.BlockSpec((1, LANES), fixed)],
        out_specs=[pl.BlockSpec((1, tm, d), row),
                   pl.BlockSpec((1, tm, d // 2), row),
                   pl.BlockSpec((1, tm, LANES), row),
                   pl.BlockSpec((1, tm, LANES), row)],
        out_shape=[jax.ShapeDtypeStruct((b, s, d), F32),
                   jax.ShapeDtypeStruct((b, s, d // 2), I32),
                   jax.ShapeDtypeStruct((b, s, LANES), I32),
                   jax.ShapeDtypeStruct((b, s, LANES), F32)],
        compiler_params=_cparams("parallel", "parallel"),
        name="outproj_ln2_router",
    )(merged3, w_out, x, gt1, g2, sc2, sh2, w_r, b_r)


def _plan_kernel(ids_ref, dest_ref, te_ref, rank_ref):
    n = ids_ref.shape[0]
    cb = 256
    lane = lax.broadcasted_iota(I32, (cb, LANES), 1)
    lower = (lax.broadcasted_iota(I32, (cb, cb), 0) > lax.broadcasted_iota(I32, (cb, cb), 1)).astype(BF16)

    def count(bi, carry):
        rows = pl.ds(pl.multiple_of(bi * cb, cb), cb)
        ids = ids_ref[rows, :]
        m1 = lane == ids[:, 0:1]
        m2 = lane == ids[:, 1:2]
        onehot = m1 | m2
        before = jnp.dot(lower, onehot.astype(BF16), preferred_element_type=F32) + carry
        r1 = jnp.sum(jnp.where(m1, before, 0.0), axis=-1, keepdims=True)
        r2 = jnp.sum(jnp.where(m2, before, 0.0), axis=-1, keepdims=True)
        rank_ref[rows, :] = jnp.where(lane == 0, r1, jnp.where(lane == 1, r2, 0.0))
        return carry + jnp.sum(onehot.astype(F32), axis=0, keepdims=True)

    counts = lax.fori_loop(0, n // cb, count, jnp.zeros((1, LANES), F32))
    padded = jnp.floor((counts + (MOE_TILE - 1)) / MOE_TILE) * MOE_TILE
    upper = (lax.broadcasted_iota(I32, (LANES, LANES), 0)
             < lax.broadcasted_iota(I32, (LANES, LANES), 1)).astype(F32)
    offs = jnp.dot(jnp.broadcast_to(padded, (8, LANES)), upper, preferred_element_type=F32,
                   precision=lax.Precision.HIGHEST)[0:1]

    def place(bi, c):
        rows = pl.ds(pl.multiple_of(bi * cb, cb), cb)
        ids = ids_ref[rows, :]
        o1 = jnp.sum(jnp.where(lane == ids[:, 0:1], offs, 0.0), axis=-1, keepdims=True)
        o2 = jnp.sum(jnp.where(lane == ids[:, 1:2], offs, 0.0), axis=-1, keepdims=True)
        slot = rank_ref[rows, :] + jnp.where(lane == 0, o1, jnp.where(lane == 1, o2, 0.0))
        dest_ref[rows, :] = slot.astype(I32)
        return c

    lax.fori_loop(0, n // cb, place, 0)

    sq_row = lax.broadcasted_iota(I32, (LANES, LANES), 0)
    sq_lane = lax.broadcasted_iota(I32, (LANES, LANES), 1)
    tile_start = sq_row.astype(F32) * MOE_TILE
    finished = ((offs + padded) <= tile_start) & (sq_lane < N_EXPERTS)
    tile_expert = jnp.minimum(jnp.sum(finished.astype(F32), axis=-1, keepdims=True), N_EXPERTS - 1.0)
    n_tiles = jnp.sum(padded, axis=-1, keepdims=True) / MOE_TILE
    te_ref[...] = jnp.where(sq_lane == 1, n_tiles, tile_expert).astype(I32)


def _plan(ids):
    n = ids.shape[0]
    return pl.pallas_call(
        _plan_kernel,
        out_shape=[jax.ShapeDtypeStruct((n, LANES), I32),
                   jax.ShapeDtypeStruct((LANES, LANES), I32)],
        scratch_shapes=[pltpu.VMEM((n, LANES), F32)],
        compiler_params=pltpu.CompilerParams(vmem_limit_bytes=VMEM_LIMIT),
        name="moe_plan",
    )(ids)


def _dispatch_kernel(dest_ref, h_ref, hs_zero_ref, hs_ref, sem):
    del hs_zero_ref
    tq = h_ref.shape[0]

    def issue(r, c):
        for k in range(MOE_SLOTS):
            slot = dest_ref[0, 0, MOE_SLOTS * r + k]
            pltpu.make_async_copy(h_ref.at[pl.ds(r, 1)], hs_ref.at[pl.ds(slot, 1)], sem).start(priority=k)
        return c

    lax.fori_loop(0, tq, issue, 0, unroll=8)
    done = hs_ref.at[pl.ds(0, MOE_SLOTS * tq)]
    pltpu.make_async_copy(done, done, sem).wait()


def _dispatch(dest3, h2p, n_rows):
    n, w = h2p.shape
    tq = TOKEN_TILE
    hs_zero = jnp.zeros((n_rows, w), h2p.dtype)
    return pl.pallas_call(
        _dispatch_kernel,
        grid=(n // tq,),
        in_specs=[pl.BlockSpec((1, 1, MOE_SLOTS * tq), lambda i: (i, 0, 0), memory_space=pltpu.SMEM),
                  pl.BlockSpec((tq, w), lambda i: (i, 0)),
                  pl.BlockSpec(memory_space=pl.ANY)],
        out_specs=pl.BlockSpec(memory_space=pl.ANY),
        out_shape=jax.ShapeDtypeStruct((n_rows, w), h2p.dtype),
        scratch_shapes=[pltpu.SemaphoreType.DMA(())],
        input_output_aliases={2: 0},
        compiler_params=_cparams("arbitrary"),
        name="moe_dispatch",
    )(dest3, h2p, hs_zero)


def _expert_kernel(te_ref, nt_ref, hs_ref, wg_hbm, wu_hbm, wd_hbm, ys_ref,
                   wg_buf, wu_buf, wd_buf, sem, slot_ref):
    t = pl.program_id(0)
    nt = nt_ref[0]
    used = t < nt
    last = te_ref.shape[0] - 1

    def weight_copies(e, slot):
        return (pltpu.make_async_copy(wg_hbm.at[e], wg_buf.at[slot], sem.at[0, slot]),
                pltpu.make_async_copy(wu_hbm.at[e], wu_buf.at[slot], sem.at[1, slot]),
                pltpu.make_async_copy(wd_hbm.at[e], wd_buf.at[slot], sem.at[2, slot]))

    @pl.when(t == 0)
    def _():
        slot_ref[0] = 0
        for c in weight_copies(te_ref[0], 0):
            c.start()

    @pl.when(jnp.logical_not(used))
    def _():
        ys_ref[...] = jnp.zeros_like(ys_ref)

    @pl.when(used)
    def _():
        e = te_ref[t]
        first = (t == 0) | (te_ref[jnp.maximum(t - 1, 0)] != e)

        @pl.when(first & (t > 0))
        def _():
            slot_ref[0] = 1 - slot_ref[0]

        slot = slot_ref[0]

        @pl.when(first)
        def _():
            for c in weight_copies(e, slot):
                c.wait()
            nxt = lax.while_loop(lambda s: (s < nt) & (te_ref[jnp.minimum(s, last)] == e),
                                 lambda s: s + 1, t + 1)

            @pl.when(nxt < nt)
            def _():
                for c in weight_copies(te_ref[jnp.minimum(nxt, last)], 1 - slot):
                    c.start()

        lo, hi = _unpack_bf16_pair(hs_ref[...])
        k = lo.shape[1]

        def up(w_buf):
            return (jnp.dot(lo, w_buf[slot, :k, :].astype(BF16), preferred_element_type=F32)
                    + jnp.dot(hi, w_buf[slot, k:, :].astype(BF16), preferred_element_type=F32))

        a = up(wg_buf)
        u = up(wu_buf)
        act = (a * _sigmoid(a) * u).astype(BF16)
        ys_ref[...] = jnp.dot(act, wd_buf[slot].astype(BF16), preferred_element_type=F32)


def _experts(tile_expert, n_tiles, hs, wg, wu, wd):
    n_rows, hw = hs.shape
    d, f = wg.shape[1], wg.shape[2]
    row = lambda t, te, nt: (jnp.minimum(t, nt[0] - 1), 0)
    hbm = pl.BlockSpec(memory_space=pl.ANY)
    return pl.pallas_call(
        _expert_kernel,
        grid_spec=pltpu.PrefetchScalarGridSpec(
            num_scalar_prefetch=2,
            grid=(n_rows // MOE_TILE,),
            in_specs=[pl.BlockSpec((MOE_TILE, hw), row), hbm, hbm, hbm],
            out_specs=pl.BlockSpec((MOE_TILE, d), lambda t, te, nt: (t, 0)),
            scratch_shapes=[pltpu.VMEM((2, d, f), wg.dtype),
                            pltpu.VMEM((2, d, f), wu.dtype),
                            pltpu.VMEM((2, f, d), wd.dtype),
                            pltpu.SemaphoreType.DMA((3, 2)),
                            pltpu.SMEM((1,), I32)]),
        out_shape=jax.ShapeDtypeStruct((n_rows, d), F32),
        compiler_params=_cparams("arbitrary"),
        name="moe_experts",
    )(tile_expert, n_tiles, hs, wg, wu, wd)


def _combine_kernel(dest_ref, ys_ref, x1_ref, cw_ref, gt_ref, o_ref, buf, sem):
    tq = x1_ref.shape[1]

    def issue(r, c):
        for k in range(MOE_SLOTS):
            slot = dest_ref[0, 0, MOE_SLOTS * r + k]
            pltpu.make_async_copy(ys_ref.at[pl.ds(slot, 1)], buf.at[k, pl.ds(r, 1)], sem).start(priority=k)
        return c

    lax.fori_loop(0, tq, issue, 0, unroll=8)
    pltpu.make_async_copy(buf, buf, sem).wait()
    cw = cw_ref[0]
    y = cw[:, 0:1] * buf[0] + cw[:, 1:2] * buf[1]
    o_ref[0] = x1_ref[0] + gt_ref[0] * y


def _combine(dest3, ys, x1, cw, gt2):
    b, s, d = x1.shape
    tq = TOKEN_TILE
    spb = s // tq
    row = lambda i, j: (i, j, 0)
    return pl.pallas_call(
        _combine_kernel,
        grid=(b, spb),
        in_specs=[pl.BlockSpec((1, 1, MOE_SLOTS * tq), lambda i, j: (i * spb + j, 0, 0),
                               memory_space=pltpu.SMEM),
                  pl.BlockSpec(memory_space=pl.ANY),
                  pl.BlockSpec((1, tq, d), row),
                  pl.BlockSpec((1, tq, LANES), row),
                  pl.BlockSpec((1, 1, d), lambda i, j: (i, 0, 0))],
        out_specs=pl.BlockSpec((1, tq, d), row),
        out_shape=jax.ShapeDtypeStruct((b, s, d), F32),
        scratch_shapes=[pltpu.VMEM((MOE_SLOTS, tq, d), F32), pltpu.SemaphoreType.DMA(())],
        compiler_params=_cparams("arbitrary", "arbitrary"),
        name="moe_combine",
    )(dest3, ys, x1, cw, gt2)


def _moe(h2p, ids, cw, wg, wu, wd, x1, gt2):
    b, s, d = x1.shape
    n = b * s
    n_rows = _moe_max_tiles(n) * MOE_TILE
    dest, te = _plan(ids.reshape(n, LANES))
    dest3 = dest[:, :MOE_SLOTS].reshape(n // TOKEN_TILE, 1, MOE_SLOTS * TOKEN_TILE)
    hs = _dispatch(dest3, h2p.reshape(n, d // 2), n_rows)
    ys = _experts(te[:n_rows // MOE_TILE, 0], te[0, 1:2], hs, wg, wu, wd)
    return _combine(dest3, ys, x1, cw, gt2)


def _alibi_slopes():
    n = SWA_HEADS + DSA_HEADS
    i = jnp.arange(1, n + 1, dtype=F32)
    return jnp.exp2(-8.0 * i / n)


def _layer(x, mod, ln1_g, w_in, swa_q_norm, swa_k_norm, swa_sinks, dsa_q_norm, dsa_k_norm,
           w_branch_a, w_branch_b, w_out, ln2_g, w_group, b_group, w_expert, b_expert,
           w_gate_up, w_up, w_down, slopes):
    b, s, d = x.shape
    sh1, sc1, gt1, sh2, sc2, gt2 = [m[:, None, :] for m in jnp.split(mod, 6, axis=-1)]

    o = 0
    seg = {}
    for name, width in (("qa", 1024), ("ka", 256), ("va", 256), ("qb", 1024), ("kb", 64), ("vb", 64),
                        ("qi", 1024), ("ki", 64), ("wi", 16), ("g", 2 * d)):
        seg[name] = (o, o + width)
        o += width
    cols = lambda nm: w_in[:, seg[nm][0]:seg[nm][1]]
    pad = jnp.zeros((d, ATT_WIDTH - COL_SMALL - 208), w_in.dtype)
    w_att = jnp.concatenate([cols("qa"), cols("qb"), cols("qi"), cols("ka"), cols("va"),
                             cols("kb"), cols("vb"), cols("ki"), cols("wi"), pad], axis=1).astype(BF16)
    w_gate = cols("g").astype(BF16)

    h1 = _ln_mod(x, ln1_g[None, :], sc1, sh1)
    h1f = h1.reshape(b * s, d)
    proj = _mm(h1f, w_att, 1024, 768, F32).reshape(b, s, ATT_WIDTH)
    oa = _swa(proj, swa_sinks, slopes[:SWA_HEADS], swa_q_norm[None, :], swa_k_norm[None, :])
    ob = _dsa(proj, slopes[SWA_HEADS:], dsa_q_norm[None, :], dsa_k_norm[None, :])
    merged = _merge(h1f, oa.reshape(b * s, -1), ob.reshape(b * s, -1), w_gate,
                    w_branch_a.astype(BF16), w_branch_b.astype(BF16))

    w_r = jnp.concatenate([w_expert, w_group,
                           jnp.zeros((d, LANES - N_EXPERTS - N_GROUPS), F32)], axis=1)
    b_r = jnp.concatenate([b_expert, b_group,
                           jnp.zeros((LANES - N_EXPERTS - N_GROUPS,), F32)])[None, :]
    w_r_hi = _bf16_head(w_r)
    w_r_split = jnp.stack([w_r_hi.astype(BF16), (w_r - w_r_hi).astype(BF16)])
    x1, h2p, ids, cw = _outproj(merged.reshape(b, s, d), w_out.astype(BF16), x, gt1,
                                ln2_g[None, :], sc2, sh2, w_r_split, b_r)
    f = w_gate_up.shape[-1]
    return _moe(h2p, ids, cw, w_gate_up.reshape(N_EXPERTS, d, f), w_up.reshape(N_EXPERTS, d, f),
                w_down.reshape(N_EXPERTS, f, d), x1, gt2)


def kernel(x, c, w_ada, b_ada, ln1_g, w_in, swa_q_norm, swa_k_norm, swa_sinks, dsa_q_norm, dsa_k_norm,
           w_branch_a, w_branch_b, w_out, ln2_g, w_group, b_group, w_expert, b_expert,
           w_gate_up, w_up, w_down):
    slopes = _alibi_slopes()
    bsz = c.shape[0]
    c_pad = jnp.concatenate([c, jnp.zeros((8 - bsz, c.shape[1]), c.dtype)], axis=0)
    for l in range(w_ada.shape[0]):
        mod = _ada(c_pad, w_ada[l], b_ada[l][None, :])[:bsz]
        x = _layer(x, mod, ln1_g[l], w_in[l], swa_q_norm[l], swa_k_norm[l], swa_sinks[l],
                   dsa_q_norm[l], dsa_k_norm[l], w_branch_a[l], w_branch_b[l], w_out[l], ln2_g[l],
                   w_group[l], b_group[l], w_expert[l], b_expert[l],
                   w_gate_up[l], w_up[l], w_down[l], slopes)
    return x
```

```python
import functools

import jax
import jax.numpy as jnp
from jax import lax
from jax.experimental import pallas as pl
from jax.experimental.pallas import tpu as pltpu

F32 = jnp.float32
BF16 = jnp.bfloat16
I32 = jnp.int32

D_MODEL = 2048
HEAD_DIM = 64
SWA_HEADS = 16
SWA_KV_HEADS = 4
SWA_GROUP = SWA_HEADS // SWA_KV_HEADS
WINDOW = 128
DSA_HEADS = 16
IDX_HEADS = 16
IDX_DIM = 64
DSA_TOPK_MAX = 256
DSA_BLOCK = 128
N_GROUPS = 4
EXPERTS_PER_GROUP = 8
N_EXPERTS = N_GROUPS * EXPERTS_PER_GROUP
D_EXPERT = 512
RMS_EPS = 1e-6
NEG_INF = -1e30
INT_MIN = -(2 ** 31)

LANES = 128
VMEM_LIMIT = 56 * 1024 * 1024

COL_QA, COL_KA, COL_VA, COL_QB = 0, 1024, 1280, 1536
COL_KVB = 2560
COL_QI = 2688
COL_KWI = 3712
COL_GATE = 3792
ATT_WIDTH = 3840
KVB_K, KVB_V = 0, HEAD_DIM
KWI_K, KWI_W = 0, IDX_DIM

NT_DIMS = (((1,), (1,)), ((), ()))


def _cparams(*sem):
    return pltpu.CompilerParams(dimension_semantics=sem, vmem_limit_bytes=VMEM_LIMIT)


def _sigmoid(x):
    return 1.0 / (1.0 + jnp.exp(-x))


def _rms(x, g):
    return x * lax.rsqrt(jnp.mean(x * x, axis=-1, keepdims=True) + RMS_EPS) * g


def _ada_kernel(c_ref, w_ref, b_ref, o_ref):
    c = c_ref[...]
    cond = c * _sigmoid(c)
    o_ref[...] = jnp.dot(cond, w_ref[...], preferred_element_type=F32,
                         precision=lax.Precision.HIGHEST) + b_ref[...]


def _ada(c_pad, w, b):
    n = w.shape[1]
    tn = 1024
    return pl.pallas_call(
        _ada_kernel,
        grid=(n // tn,),
        in_specs=[pl.BlockSpec((8, D_MODEL), lambda j: (0, 0)),
                  pl.BlockSpec((D_MODEL, tn), lambda j: (0, j)),
                  pl.BlockSpec((1, tn), lambda j: (0, j))],
        out_specs=pl.BlockSpec((8, tn), lambda j: (0, j)),
        out_shape=jax.ShapeDtypeStruct((8, n), F32),
        compiler_params=_cparams("parallel"),
        name="ada_mod",
    )(c_pad, w, b)


def _ln_mod_kernel(x_ref, g_ref, sc_ref, sh_ref, o_ref):
    y = _rms(x_ref[0], g_ref[...])
    o_ref[0] = (y * (1.0 + sc_ref[0]) + sh_ref[0]).astype(o_ref.dtype)


def _ln_mod(x, g, sc, sh):
    b, s, d = x.shape
    ts = 512
    return pl.pallas_call(
        _ln_mod_kernel,
        grid=(b, s // ts),
        in_specs=[pl.BlockSpec((1, ts, d), lambda i, j: (i, j, 0)),
                  pl.BlockSpec((1, d), lambda i, j: (0, 0)),
                  pl.BlockSpec((1, 1, d), lambda i, j: (i, 0, 0)),
                  pl.BlockSpec((1, 1, d), lambda i, j: (i, 0, 0))],
        out_specs=pl.BlockSpec((1, ts, d), lambda i, j: (i, j, 0)),
        out_shape=jax.ShapeDtypeStruct((b, s, d), BF16),
        compiler_params=_cparams("parallel", "parallel"),
        name="ln_mod",
    )(x, g, sc, sh)


def _mm_kernel(a_ref, w_ref, o_ref):
    o_ref[...] = jnp.dot(a_ref[...], w_ref[...].astype(BF16), preferred_element_type=F32).astype(o_ref.dtype)


def _mm(a, w, n, tm, tn, out_dtype):
    m, k = a.shape
    return pl.pallas_call(
        _mm_kernel,
        grid=(m // tm, n // tn),
        in_specs=[pl.BlockSpec((tm, k), lambda i, j: (i, 0)),
                  pl.BlockSpec((k, tn), lambda i, j: (0, j))],
        out_specs=pl.BlockSpec((tm, tn), lambda i, j: (i, j)),
        out_shape=jax.ShapeDtypeStruct((m, n), out_dtype),
        compiler_params=_cparams("parallel", "parallel"),
        name="proj_att",
    )(a, w)


def _pair_rms(xp, half, gain2):
    sq = xp * xp
    ms_lo = jnp.sum(jnp.where(half, sq, 0.0), axis=-1, keepdims=True) / HEAD_DIM
    ms_hi = jnp.sum(jnp.where(half, 0.0, sq), axis=-1, keepdims=True) / HEAD_DIM
    return xp * lax.rsqrt(jnp.where(half, ms_lo, ms_hi) + RMS_EPS) * gain2


def _swa_kernel(sink_ref, slope_ref, q_ref, kp_ref, kc_ref, vp_ref, vc_ref, qg_ref, kg_ref, o_ref, lhs, e_s):
    n = pl.program_id(1)
    w, w2 = WINDOW, 2 * WINDOW
    q = q_ref[0]
    kw = jnp.concatenate([kp_ref[0], kc_ref[0]], axis=0)
    vw = jnp.concatenate([vp_ref[0], vc_ref[0]], axis=0)
    gq2 = jnp.concatenate([qg_ref[...], qg_ref[...]], axis=1)
    gk2 = jnp.concatenate([kg_ref[...], kg_ref[...]], axis=1)
    half_q = lax.broadcasted_iota(I32, (w, LANES), 1) < HEAD_DIM
    half_k = lax.broadcasted_iota(I32, (w2, LANES), 1) < HEAD_DIM
    row = lax.broadcasted_iota(I32, (w, w2), 0)
    col = lax.broadcasted_iota(I32, (w, w2), 1)
    rel = row + w - col
    valid = (rel >= 0) & (rel < w) & ((col >= w) | (n > 0))
    mbias = jnp.where(valid, 0.0, NEG_INF)
    relf = rel.astype(F32)

    ones_col = jnp.ones((w2, LANES), BF16)
    for g in range(SWA_KV_HEADS):
        if g % 2 == 0:
            grp = slice((g // 2) * LANES, (g // 2 + 1) * LANES)
            kn = _pair_rms(kw[:, grp], half_k, gk2)
            kn_sw = pltpu.roll(kn, HEAD_DIM, axis=1)
            vp = vw[:, grp]
            vp_sw = pltpu.roll(vp, HEAD_DIM, axis=1)
        kdup = jnp.where(half_k, kn, kn_sw) if g % 2 == 0 else jnp.where(half_k, kn_sw, kn)
        vdup = jnp.where(half_k, vp, vp_sw) if g % 2 == 0 else jnp.where(half_k, vp_sw, vp)
        vx = jnp.concatenate([vdup.astype(BF16), ones_col], axis=1)
        for r in range(SWA_GROUP):
            h = g * SWA_GROUP + r
            if r % 2 == 0:
                qn = _pair_rms(q[:, (h // 2) * LANES:(h // 2 + 1) * LANES], half_q, gq2) * (HEAD_DIM ** -0.5)
            keep = half_q if r % 2 == 0 else jnp.logical_not(half_q)
            lhs[r * w:(r + 1) * w, :] = jnp.where(keep, qn, 0.0).astype(BF16)
        s_all = lax.dot_general(lhs[...], kdup.astype(BF16), NT_DIMS, preferred_element_type=F32)
        sink_e = []
        for r in range(SWA_GROUP):
            h = g * SWA_GROUP + r
            rows = slice(r * w, (r + 1) * w)
            s = s_all[rows, :] + (mbias - slope_ref[h] * relf)
            sink = jnp.full((w, LANES), sink_ref[h], F32)
            m = jnp.maximum(jnp.max(s, axis=-1, keepdims=True), sink)
            e_s[rows, :] = jnp.exp(s - jnp.concatenate([m, m], axis=1)).astype(BF16)
            sink_e.append(jnp.exp(sink - m))
        pv = jnp.dot(e_s[...], vx, preferred_element_type=F32)
        for r in range(0, SWA_GROUP, 2):
            h = g * SWA_GROUP + r
            ev, od = pv[r * w:(r + 1) * w, :], pv[(r + 1) * w:(r + 2) * w, :]
            den = jnp.where(half_q, ev[:, LANES:] + sink_e[r], od[:, LANES:] + sink_e[r + 1])
            o = jnp.where(half_q, ev[:, :LANES], od[:, :LANES]) / den
            o_ref[0, :, (h // 2) * LANES:(h // 2 + 1) * LANES] = o.astype(o_ref.dtype)


def _swa(proj3, sinks, slopes, q_gain, k_gain):
    b, s, _ = proj3.shape
    nb = s // WINDOW
    kvw = SWA_KV_HEADS * HEAD_DIM
    qw = SWA_HEADS * HEAD_DIM
    smem = pl.BlockSpec(memory_space=pltpu.SMEM)
    fixed = lambda i, n: (0, 0)
    prev = lambda i, n: (i, jnp.maximum(n - 1, 0), COL_KA // kvw)
    cur = lambda i, n: (i, n, COL_KA // kvw)
    prev_v = lambda i, n: (i, jnp.maximum(n - 1, 0), COL_VA // kvw)
    cur_v = lambda i, n: (i, n, COL_VA // kvw)
    return pl.pallas_call(
        _swa_kernel,
        grid=(b, nb),
        in_specs=[smem, smem,
                  pl.BlockSpec((1, WINDOW, qw), lambda i, n: (i, n, COL_QA // qw)),
                  pl.BlockSpec((1, WINDOW, kvw), prev),
                  pl.BlockSpec((1, WINDOW, kvw), cur),
                  pl.BlockSpec((1, WINDOW, kvw), prev_v),
                  pl.BlockSpec((1, WINDOW, kvw), cur_v),
                  pl.BlockSpec((1, HEAD_DIM), fixed),
                  pl.BlockSpec((1, HEAD_DIM), fixed)],
        out_specs=pl.BlockSpec((1, WINDOW, qw), lambda i, n: (i, n, 0)),
        out_shape=jax.ShapeDtypeStruct((b, s, qw), BF16),
        scratch_shapes=[pltpu.VMEM((SWA_GROUP * WINDOW, LANES), BF16),
                        pltpu.VMEM((SWA_GROUP * WINDOW, 2 * WINDOW), BF16)],
        compiler_params=_cparams("parallel", "parallel"),
        name="swa_attn",
    )(sinks, slopes, proj3, proj3, proj3, proj3, proj3, q_gain, k_gain)


DSA_CHUNK = 256
ALIBI_SPLIT = 3
POS_LO_BITS = 7


def _bf16_head(x):
    return lax.bitcast_convert_type(lax.bitcast_convert_type(x, I32) & jnp.int32(-65536), F32)


def _alibi_columns(slopes):
    parts, rest = [], slopes.astype(F32)
    for _ in range(ALIBI_SPLIT):
        piece = _bf16_head(rest)
        parts.append(piece)
        rest = rest - piece
    cols = [p * float(2 ** POS_LO_BITS) for p in parts] + parts
    pad = jnp.zeros((slopes.shape[0], LANES - 2 * ALIBI_SPLIT), F32)
    return jnp.concatenate([jnp.stack(cols, axis=1), pad], axis=1)


def _dsa_kernel(*refs, topk):
    n_pairs = DSA_HEADS // 2
    qb_refs, qi_refs = refs[:n_pairs], refs[n_pairs:2 * n_pairs]
    (kwi_blk_ref, kvb_ref, kwi_ref, acol_ref, qg_ref, kg_ref, o_ref,
     ki2, kx, vx, lhs_i, lhs_q, wb, key_s, keyt_s, mb_s, thr_s, acc_s, m_s, e_s, al_s) = refs[2 * n_pairs:]
    i = pl.program_id(1)
    blk, ck = DSA_BLOCK, DSA_CHUNK
    nch = (i + 2) // 2
    seq = ki2.shape[0]
    max_ch = seq // ck
    half = lax.broadcasted_iota(I32, (blk, LANES), 1) < HEAD_DIM

    @pl.when(i == 0)
    def _():
        kvb = kvb_ref[0]
        kib = kwi_ref[0][:, KWI_K:KWI_K + IDX_DIM].astype(BF16)
        ki2[...] = jnp.concatenate([kib, kib], axis=1)
        kn = _rms(kvb[:, KVB_K:KVB_K + HEAD_DIM], kg_ref[...]).astype(BF16)
        pos = lax.broadcasted_iota(I32, (seq, LANES), 0)
        lane = lax.broadcasted_iota(I32, (seq, LANES), 1)
        hi = lax.shift_right_logical(pos, POS_LO_BITS).astype(F32)
        lo = (pos & (2 ** POS_LO_BITS - 1)).astype(F32)
        posc = jnp.where(lane < ALIBI_SPLIT, hi, jnp.where(lane < 2 * ALIBI_SPLIT, lo, 0.0)).astype(BF16)
        kx[...] = jnp.concatenate([kn, kn, posc], axis=1)
        ones = jnp.where(lax.broadcasted_iota(I32, (seq, HEAD_DIM), 1) == 0, 1.0, 0.0).astype(BF16)
        vx[...] = jnp.concatenate([kvb[:, KVB_V:KVB_V + HEAD_DIM].astype(BF16), ones], axis=1)

    g2 = jnp.concatenate([qg_ref[...], qg_ref[...]], axis=1)
    wi = kwi_blk_ref[0][:, KWI_W:KWI_W + IDX_HEADS] * (IDX_HEADS ** -0.5)
    for p in range(n_pairs):
        qn = _pair_rms(qb_refs[p][0], half, g2) * (HEAD_DIM ** -0.5)
        qip = qi_refs[p][0] * (IDX_DIM ** -0.5)
        for k in range(2):
            h = 2 * p + k
            keep = half if k == 0 else jnp.logical_not(half)
            rows = slice(h * blk, (h + 1) * blk)
            lhs_i[rows, :] = jnp.where(keep, qip, 0.0).astype(BF16)
            lhs_q[rows, :LANES] = jnp.where(keep, qn, 0.0).astype(BF16)
            lhs_q[rows, LANES:] = jnp.broadcast_to(acol_ref[h:h + 1, :], (blk, LANES)).astype(BF16)
            wb[h] = jnp.broadcast_to(wi[:, h:h + 1], (blk, LANES))

    row_t = i * blk + lax.broadcasted_iota(I32, (blk, ck), 0)
    col_k = lax.broadcasted_iota(I32, (blk, ck), 1)

    def score_chunk(c, carry):
        ks = pl.ds(pl.multiple_of(c * ck, ck), ck)
        logits = lax.dot_general(lhs_i[...], ki2[ks, :], NT_DIMS, preferred_element_type=F32)
        sc = jnp.zeros((blk, ck), F32)
        for h in range(IDX_HEADS):
            w = wb[h]
            lh = jnp.maximum(logits[h * blk:(h + 1) * blk, :], 0.0)
            sc = sc + lh * jnp.concatenate([w, w], axis=1)
        bits = lax.bitcast_convert_type(sc + 0.0, I32)
        key = jnp.where(bits < 0, bits ^ jnp.int32(0x7FFFFFFF), bits)
        key = jnp.where(c * ck + col_k <= row_t, key, jnp.int32(INT_MIN))
        key_s[c] = key
        keyt_s[c] = key.T
        return carry

    lax.fori_loop(0, nch, score_chunk, 0)

    thr_s[...] = jnp.full(thr_s.shape, INT_MIN, I32)
    for n in range(2, max_ch + 1):
        @pl.when(nch == n)
        def _(n=n):
            def step(it, thr):
                cand = thr + lax.shift_left(jnp.int32(1), 31 - it)
                cnt = jnp.zeros((1, blk), F32)
                for c in range(n):
                    cnt = cnt + jnp.sum(jnp.where(keyt_s[c] >= cand, 1.0, 0.0), axis=0, keepdims=True)
                return jnp.where(cnt >= float(topk), cand, thr)

            thr = lax.fori_loop(0, 32, step, jnp.full((1, blk), INT_MIN, I32))
            thr_s[...] = jnp.broadcast_to(thr, thr_s.shape).T

    thr = thr_s[...]
    thr2 = jnp.concatenate([thr, thr], axis=1)

    def count_gt(c, cnt):
        k = key_s[c]
        return cnt + jnp.where(k[:, :LANES] > thr, 1.0, 0.0) + jnp.where(k[:, LANES:] > thr, 1.0, 0.0)

    n_gt = jnp.sum(lax.fori_loop(0, nch, count_gt, jnp.zeros((blk, LANES), F32)), axis=-1, keepdims=True)
    need = float(topk) - n_gt
    upper = (lax.broadcasted_iota(I32, (ck, ck), 0) < lax.broadcasted_iota(I32, (ck, ck), 1)).astype(BF16)

    def mask_chunk(c, n_eq):
        k = key_s[c]
        eq = (k == thr2) & (c * ck + col_k <= row_t)
        eqf = eq.astype(BF16)
        before = jnp.dot(eqf, upper, preferred_element_type=F32) + n_eq
        sel = (k > thr2) | (eq & (before < need))
        mb_s[c] = jnp.where(sel, 0.0, NEG_INF)
        return n_eq + jnp.sum(eqf.astype(F32), axis=-1, keepdims=True)

    lax.fori_loop(0, nch, mask_chunk, jnp.zeros((blk, 1), F32))

    acc_s[...] = jnp.zeros_like(acc_s)
    m_s[...] = jnp.full(m_s.shape, -jnp.inf, F32)

    def attend(c, carry):
        ks = pl.ds(pl.multiple_of(c * ck, ck), ck)
        s_all = lax.dot_general(lhs_q[...], kx[ks, :], NT_DIMS, preferred_element_type=F32)
        mb = mb_s[c]
        for h in range(DSA_HEADS):
            rows = slice(h * blk, (h + 1) * blk)
            s = s_all[rows, :] + mb
            m_old = m_s[rows, :]
            m_new = jnp.maximum(m_old, jnp.max(s, axis=-1, keepdims=True))
            al_s[rows, :] = jnp.exp(m_old - m_new)
            e_s[rows, :] = jnp.exp(s - jnp.concatenate([m_new, m_new], axis=1)).astype(BF16)
            m_s[rows, :] = m_new
        pv = jnp.dot(e_s[...], vx[ks, :], preferred_element_type=F32)
        acc_s[...] = acc_s[...] * al_s[...] + pv
        return carry

    lax.fori_loop(0, nch, attend, 0)

    for h in range(DSA_HEADS):
        a = acc_s[h * blk:(h + 1) * blk, :]
        o = a[:, :HEAD_DIM] / a[:, HEAD_DIM:HEAD_DIM + 1]
        o_ref[0, :, h * HEAD_DIM:(h + 1) * HEAD_DIM] = o.astype(o_ref.dtype)


def _dsa(proj3, slopes, q_gain, k_gain):
    b, s, _ = proj3.shape
    nb = s // DSA_BLOCK
    qw = DSA_HEADS * HEAD_DIM
    n_pairs = DSA_HEADS // 2
    rows = DSA_HEADS * DSA_BLOCK
    fixed = lambda i, n: (0, 0)

    def group(col):
        return pl.BlockSpec((1, DSA_BLOCK, LANES), lambda i, n: (i, n, col // LANES))

    def group_all(col):
        return pl.BlockSpec((1, s, LANES), lambda i, n: (i, 0, col // LANES))

    return pl.pallas_call(
        functools.partial(_dsa_kernel, topk=min(DSA_TOPK_MAX, s // 4)),
        grid=(b, nb),
        in_specs=([group(COL_QB + p * LANES) for p in range(n_pairs)]
                  + [group(COL_QI + p * LANES) for p in range(n_pairs)]
                  + [group(COL_KWI), group_all(COL_KVB), group_all(COL_KWI),
                     pl.BlockSpec((DSA_HEADS, LANES), fixed),
                     pl.BlockSpec((1, HEAD_DIM), fixed),
                     pl.BlockSpec((1, HEAD_DIM), fixed)]),
        out_specs=pl.BlockSpec((1, DSA_BLOCK, qw), lambda i, n: (i, n, 0)),
        out_shape=jax.ShapeDtypeStruct((b, s, qw), BF16),
        scratch_shapes=[pltpu.VMEM((s, LANES), BF16),
                        pltpu.VMEM((s, 2 * LANES), BF16),
                        pltpu.VMEM((s, LANES), BF16),
                        pltpu.VMEM((rows, LANES), BF16),
                        pltpu.VMEM((rows, 2 * LANES), BF16),
                        pltpu.VMEM((IDX_HEADS, DSA_BLOCK, LANES), F32),
                        pltpu.VMEM((s // DSA_CHUNK, DSA_BLOCK, DSA_CHUNK), I32),
                        pltpu.VMEM((s // DSA_CHUNK, DSA_CHUNK, DSA_BLOCK), I32),
                        pltpu.VMEM((s // DSA_CHUNK, DSA_BLOCK, DSA_CHUNK), F32),
                        pltpu.VMEM((DSA_BLOCK, LANES), I32),
                        pltpu.VMEM((rows, LANES), F32),
                        pltpu.VMEM((rows, LANES), F32),
                        pltpu.VMEM((rows, DSA_CHUNK), BF16),
                        pltpu.VMEM((rows, LANES), F32)],
        compiler_params=_cparams("arbitrary", "arbitrary"),
        name="dsa_attn",
    )(*([proj3] * (2 * n_pairs + 3)), _alibi_columns(slopes), q_gain, k_gain)


def _merge_kernel(h_ref, oa_ref, ob_ref, wga_ref, wgb_ref, wa_ref, wb_ref, o_ref):
    h = h_ref[...]
    ga = jnp.dot(h, wga_ref[...], preferred_element_type=F32)
    gb = jnp.dot(h, wgb_ref[...], preferred_element_type=F32)
    a = jnp.dot(oa_ref[...], wa_ref[...], preferred_element_type=F32)
    b = jnp.dot(ob_ref[...], wb_ref[...], preferred_element_type=F32)
    o_ref[...] = (_sigmoid(ga) * a + _sigmoid(gb) * b).astype(o_ref.dtype)


def _merge(h, oa, ob, w_gate, wa, wb):
    m, d = h.shape
    tm, tn = 1024, 512
    ngt = d // tn
    return pl.pallas_call(
        _merge_kernel,
        grid=(m // tm, ngt),
        in_specs=[pl.BlockSpec((tm, d), lambda i, j: (i, 0)),
                  pl.BlockSpec((tm, oa.shape[1]), lambda i, j: (i, 0)),
                  pl.BlockSpec((tm, ob.shape[1]), lambda i, j: (i, 0)),
                  pl.BlockSpec((d, tn), lambda i, j: (0, j)),
                  pl.BlockSpec((d, tn), lambda i, j: (0, j + ngt)),
                  pl.BlockSpec((wa.shape[0], tn), lambda i, j: (0, j)),
                  pl.BlockSpec((wb.shape[0], tn), lambda i, j: (0, j))],
        out_specs=pl.BlockSpec((tm, tn), lambda i, j: (i, j)),
        out_shape=jax.ShapeDtypeStruct((m, d), BF16),
        compiler_params=_cparams("parallel", "parallel"),
        name="merge_branches",
    )(h, oa, ob, w_gate, w_gate, wa, wb)


MOE_TILE = 256
MOE_SLOTS = 2
TOKEN_TILE = 512


def _moe_max_tiles(n_tokens):
    return -(-(n_tokens * MOE_SLOTS + N_EXPERTS * (MOE_TILE - 1)) // MOE_TILE)


def _pack_bf16_pair(x):
    k = x.shape[1] // 2
    bits = lax.bitcast_convert_type(x.astype(BF16).astype(F32), I32)
    return lax.shift_right_logical(bits[:, :k], 16) | (bits[:, k:] & jnp.int32(-65536))


def _unpack_bf16_pair(w):
    lo = lax.bitcast_convert_type(lax.shift_left(w, 16), F32).astype(BF16)
    hi = lax.bitcast_convert_type(w & jnp.int32(-65536), F32).astype(BF16)
    return lo, hi


def _outproj_kernel(mg_ref, wo_ref, x_ref, gt_ref, g2_ref, sc_ref, sh_ref, wr_ref, br_ref,
                    x1_ref, h2p_ref, ids_ref, cw_ref):
    y = jnp.dot(mg_ref[0], wo_ref[...], preferred_element_type=F32)
    x1 = x_ref[0] + gt_ref[0] * y
    x1_ref[0] = x1
    h2 = _rms(x1, g2_ref[...]) * (1.0 + sc_ref[0]) + sh_ref[0]
    h2p_ref[0] = _pack_bf16_pair(h2)

    h_hi = h2.astype(BF16)
    h_lo = (h2 - h_hi.astype(F32)).astype(BF16)
    logits = (jnp.dot(h_hi, wr_ref[0], preferred_element_type=F32)
              + jnp.dot(h_lo, wr_ref[0], preferred_element_type=F32)
              + jnp.dot(h_hi, wr_ref[1], preferred_element_type=F32)) + br_ref[...]
    lane = lax.broadcasted_iota(I32, logits.shape, 1)
    is_g = (lane >= N_EXPERTS) & (lane < N_EXPERTS + N_GROUPS)
    neg = -jnp.inf
    gl = jnp.where(is_g, logits, neg)
    gmax = jnp.max(gl, axis=-1, keepdims=True)
    g_idx = jnp.min(jnp.where(gl == gmax, lane - N_EXPERTS, N_GROUPS), axis=-1, keepdims=True)
    p_g = 1.0 / jnp.sum(jnp.where(is_g, jnp.exp(gl - gmax), 0.0), axis=-1, keepdims=True)
    in_grp = (lane < N_EXPERTS) & ((lane // EXPERTS_PER_GROUP) == g_idx)
    el = jnp.where(in_grp, logits, neg)
    m1 = jnp.max(el, axis=-1, keepdims=True)
    i1 = jnp.min(jnp.where(el == m1, lane, LANES), axis=-1, keepdims=True)
    el2 = jnp.where(lane == i1, neg, el)
    m2 = jnp.max(el2, axis=-1, keepdims=True)
    i2 = jnp.min(jnp.where(el2 == m2, lane, LANES), axis=-1, keepdims=True)
    e2 = jnp.exp(m2 - m1)
    w1 = 1.0 / (1.0 + e2)
    w2 = e2 / (1.0 + e2)
    ids_ref[0] = jnp.where(lane == 0, i1, jnp.where(lane == 1, i2, 0))
    cw_ref[0] = p_g * jnp.where(lane == 0, w1, jnp.where(lane == 1, w2, 0.0))


def _outproj(merged3, w_out, x, gt1, g2, sc2, sh2, w_r, b_r):
    b, s, d = x.shape
    tm = 512
    row = lambda i, j: (i, j, 0)
    per_b = lambda i, j: (i, 0, 0)
    fixed = lambda i, j: (0, 0)
    return pl.pallas_call(
        _outproj_kernel,
        grid=(b, s // tm),
        in_specs=[pl.BlockSpec((1, tm, d), row),
                  pl.BlockSpec((d, d), fixed, pipeline_mode=pl.Buffered(1)),
                  pl.BlockSpec((1, tm, d), row),
                  pl.BlockSpec((1, 1, d), per_b),
                  pl.BlockSpec((1, d), fixed),
                  pl.BlockSpec((1, 1, d), per_b),
                  pl.BlockSpec((1, 1, d), per_b),
                  pl.BlockSpec((2, d, LANES), lambda i, j: (0, 0, 0)),
                  pl.BlockSpec((1, LANES), fixed)],
        out_specs=[pl.BlockSpec((1, tm, d), row),
                   pl.BlockSpec((1, tm, d // 2), row),
                   pl.BlockSpec((1, tm, LANES), row),
                   pl.BlockSpec((1, tm, LANES), row)],
        out_shape=[jax.ShapeDtypeStruct((b, s, d), F32),
                   jax.ShapeDtypeStruct((b, s, d // 2), I32),
                   jax.ShapeDtypeStruct((b, s, LANES), I32),
                   jax.ShapeDtypeStruct((b, s, LANES), F32)],
        compiler_params=_cparams("parallel", "parallel"),
        name="outproj_ln2_router",
    )(merged3, w_out, x, gt1, g2, sc2, sh2, w_r, b_r)


def _plan_kernel(ids_ref, dest_ref, te_ref, rank_ref):
    n = ids_ref.shape[0]
    cb = 256
    lane = lax.broadcasted_iota(I32, (cb, LANES), 1)
    lower = (lax.broadcasted_iota(I32, (cb, cb), 0) > lax.broadcasted_iota(I32, (cb, cb), 1)).astype(BF16)

    def count(bi, carry):
        rows = pl.ds(pl.multiple_of(bi * cb, cb), cb)
        ids = ids_ref[rows, :]
        m1 = lane == ids[:, 0:1]
        m2 = lane == ids[:, 1:2]
        onehot = m1 | m2
        before = jnp.dot(lower, onehot.astype(BF16), preferred_element_type=F32) + carry
        r1 = jnp.sum(jnp.where(m1, before, 0.0), axis=-1, keepdims=True)
        r2 = jnp.sum(jnp.where(m2, before, 0.0), axis=-1, keepdims=True)
        rank_ref[rows, :] = jnp.where(lane == 0, r1, jnp.where(lane == 1, r2, 0.0))
        return carry + jnp.sum(onehot.astype(F32), axis=0, keepdims=True)

    counts = lax.fori_loop(0, n // cb, count, jnp.zeros((1, LANES), F32))
    padded = jnp.floor((counts + (MOE_TILE - 1)) / MOE_TILE) * MOE_TILE
    upper = (lax.broadcasted_iota(I32, (LANES, LANES), 0)
             < lax.broadcasted_iota(I32, (LANES, LANES), 1)).astype(F32)
    offs = jnp.dot(jnp.broadcast_to(padded, (8, LANES)), upper, preferred_element_type=F32,
                   precision=lax.Precision.HIGHEST)[0:1]

    def place(bi, c):
        rows = pl.ds(pl.multiple_of(bi * cb, cb), cb)
        ids = ids_ref[rows, :]
        o1 = jnp.sum(jnp.where(lane == ids[:, 0:1], offs, 0.0), axis=-1, keepdims=True)
        o2 = jnp.sum(jnp.where(lane == ids[:, 1:2], offs, 0.0), axis=-1, keepdims=True)
        slot = rank_ref[rows, :] + jnp.where(lane == 0, o1, jnp.where(lane == 1, o2, 0.0))
        dest_ref[rows, :] = slot.astype(I32)
        return c

    lax.fori_loop(0, n // cb, place, 0)

    sq_row = lax.broadcasted_iota(I32, (LANES, LANES), 0)
    sq_lane = lax.broadcasted_iota(I32, (LANES, LANES), 1)
    tile_start = sq_row.astype(F32) * MOE_TILE
    finished = ((offs + padded) <= tile_start) & (sq_lane < N_EXPERTS)
    tile_expert = jnp.minimum(jnp.sum(finished.astype(F32), axis=-1, keepdims=True), N_EXPERTS - 1.0)
    n_tiles = jnp.sum(padded, axis=-1, keepdims=True) / MOE_TILE
    te_ref[...] = jnp.where(sq_lane == 1, n_tiles, tile_expert).astype(I32)


def _plan(ids):
    n = ids.shape[0]
    return pl.pallas_call(
        _plan_kernel,
        out_shape=[jax.ShapeDtypeStruct((n, LANES), I32),
                   jax.ShapeDtypeStruct((LANES, LANES), I32)],
        scratch_shapes=[pltpu.VMEM((n, LANES), F32)],
        compiler_params=pltpu.CompilerParams(vmem_limit_bytes=VMEM_LIMIT),
        name="moe_plan",
    )(ids)


def _dispatch_kernel(dest_ref, h_ref, hs_zero_ref, hs_ref, sem):
    del hs_zero_ref
    tq = h_ref.shape[0]

    def issue(r, c):
        for k in range(MOE_SLOTS):
            slot = dest_ref[0, 0, MOE_SLOTS * r + k]
            pltpu.make_async_copy(h_ref.at[pl.ds(r, 1)], hs_ref.at[pl.ds(slot, 1)], sem).start(priority=k)
        return c

    lax.fori_loop(0, tq, issue, 0, unroll=8)
    done = hs_ref.at[pl.ds(0, MOE_SLOTS * tq)]
    pltpu.make_async_copy(done, done, sem).wait()


def _dispatch(dest3, h2p, n_rows):
    n, w = h2p.shape
    tq = TOKEN_TILE
    hs_zero = jnp.zeros((n_rows, w), h2p.dtype)
    return pl.pallas_call(
        _dispatch_kernel,
        grid=(n // tq,),
        in_specs=[pl.BlockSpec((1, 1, MOE_SLOTS * tq), lambda i: (i, 0, 0), memory_space=pltpu.SMEM),
                  pl.BlockSpec((tq, w), lambda i: (i, 0)),
                  pl.BlockSpec(memory_space=pl.ANY)],
        out_specs=pl.BlockSpec(memory_space=pl.ANY),
        out_shape=jax.ShapeDtypeStruct((n_rows, w), h2p.dtype),
        scratch_shapes=[pltpu.SemaphoreType.DMA(())],
        input_output_aliases={2: 0},
        compiler_params=_cparams("arbitrary"),
        name="moe_dispatch",
    )(dest3, h2p, hs_zero)


def _expert_kernel(te_ref, nt_ref, hs_ref, wg_hbm, wu_hbm, wd_hbm, ys_ref,
                   wg_buf, wu_buf, wd_buf, sem, slot_ref):
    t = pl.program_id(0)
    nt = nt_ref[0]
    used = t < nt
    last = te_ref.shape[0] - 1

    def weight_copies(e, slot):
        return (pltpu.make_async_copy(wg_hbm.at[e], wg_buf.at[slot], sem.at[0, slot]),
                pltpu.make_async_copy(wu_hbm.at[e], wu_buf.at[slot], sem.at[1, slot]),
                pltpu.make_async_copy(wd_hbm.at[e], wd_buf.at[slot], sem.at[2, slot]))

    @pl.when(t == 0)
    def _():
        slot_ref[0] = 0
        for c in weight_copies(te_ref[0], 0):
            c.start()

    @pl.when(jnp.logical_not(used))
    def _():
        ys_ref[...] = jnp.zeros_like(ys_ref)

    @pl.when(used)
    def _():
        e = te_ref[t]
        first = (t == 0) | (te_ref[jnp.maximum(t - 1, 0)] != e)

        @pl.when(first & (t > 0))
        def _():
            slot_ref[0] = 1 - slot_ref[0]

        slot = slot_ref[0]

        @pl.when(first)
        def _():
            for c in weight_copies(e, slot):
                c.wait()
            nxt = lax.while_loop(lambda s: (s < nt) & (te_ref[jnp.minimum(s, last)] == e),
                                 lambda s: s + 1, t + 1)

            @pl.when(nxt < nt)
            def _():
                for c in weight_copies(te_ref[jnp.minimum(nxt, last)], 1 - slot):
                    c.start()

        lo, hi = _unpack_bf16_pair(hs_ref[...])
        k = lo.shape[1]

        def up(w_buf):
            return (jnp.dot(lo, w_buf[slot, :k, :].astype(BF16), preferred_element_type=F32)
                    + jnp.dot(hi, w_buf[slot, k:, :].astype(BF16), preferred_element_type=F32))

        a = up(wg_buf)
        u = up(wu_buf)
        act = (a * _sigmoid(a) * u).astype(BF16)
        ys_ref[...] = jnp.dot(act, wd_buf[slot].astype(BF16), preferred_element_type=F32)


def _experts(tile_expert, n_tiles, hs, wg, wu, wd):
    n_rows, hw = hs.shape
    d, f = wg.shape[1], wg.shape[2]
    row = lambda t, te, nt: (jnp.minimum(t, nt[0] - 1), 0)
    hbm = pl.BlockSpec(memory_space=pl.ANY)
    return pl.pallas_call(
        _expert_kernel,
        grid_spec=pltpu.PrefetchScalarGridSpec(
            num_scalar_prefetch=2,
            grid=(n_rows // MOE_TILE,),
            in_specs=[pl.BlockSpec((MOE_TILE, hw), row), hbm, hbm, hbm],
            out_specs=pl.BlockSpec((MOE_TILE, d), lambda t, te, nt: (t, 0)),
            scratch_shapes=[pltpu.VMEM((2, d, f), wg.dtype),
                            pltpu.VMEM((2, d, f), wu.dtype),
                            pltpu.VMEM((2, f, d), wd.dtype),
                            pltpu.SemaphoreType.DMA((3, 2)),
                            pltpu.SMEM((1,), I32)]),
        out_shape=jax.ShapeDtypeStruct((n_rows, d), F32),
        compiler_params=_cparams("arbitrary"),
        name="moe_experts",
    )(tile_expert, n_tiles, hs, wg, wu, wd)


def _combine_kernel(dest_ref, ys_ref, x1_ref, cw_ref, gt_ref, o_ref, buf, sem):
    tq = x1_ref.shape[1]

    def issue(r, c):
        for k in range(MOE_SLOTS):
            slot = dest_ref[0, 0, MOE_SLOTS * r + k]
            pltpu.make_async_copy(ys_ref.at[pl.ds(slot, 1)], buf.at[k, pl.ds(r, 1)], sem).start(priority=k)
        return c

    lax.fori_loop(0, tq, issue, 0, unroll=8)
    pltpu.make_async_copy(buf, buf, sem).wait()
    cw = cw_ref[0]
    y = cw[:, 0:1] * buf[0] + cw[:, 1:2] * buf[1]
    o_ref[0] = x1_ref[0] + gt_ref[0] * y


def _combine(dest3, ys, x1, cw, gt2):
    b, s, d = x1.shape
    tq = TOKEN_TILE
    spb = s // tq
    row = lambda i, j: (i, j, 0)
    return pl.pallas_call(
        _combine_kernel,
        grid=(b, spb),
        in_specs=[pl.BlockSpec((1, 1, MOE_SLOTS * tq), lambda i, j: (i * spb + j, 0, 0),
                               memory_space=pltpu.SMEM),
                  pl.BlockSpec(memory_space=pl.ANY),
                  pl.BlockSpec((1, tq, d), row),
                  pl.BlockSpec((1, tq, LANES), row),
                  pl.BlockSpec((1, 1, d), lambda i, j: (i, 0, 0))],
        out_specs=pl.BlockSpec((1, tq, d), row),
        out_shape=jax.ShapeDtypeStruct((b, s, d), F32),
        scratch_shapes=[pltpu.VMEM((MOE_SLOTS, tq, d), F32), pltpu.SemaphoreType.DMA(())],
        compiler_params=_cparams("arbitrary", "arbitrary"),
        name="moe_combine",
    )(dest3, ys, x1, cw, gt2)


def _moe(h2p, ids, cw, wg, wu, wd, x1, gt2):
    b, s, d = x1.shape
    n = b * s
    n_rows = _moe_max_tiles(n) * MOE_TILE
    dest, te = _plan(ids.reshape(n, LANES))
    dest3 = dest[:, :MOE_SLOTS].reshape(n // TOKEN_TILE, 1, MOE_SLOTS * TOKEN_TILE)
    hs = _dispatch(dest3, h2p.reshape(n, d // 2), n_rows)
    ys = _experts(te[:n_rows // MOE_TILE, 0], te[0, 1:2], hs, wg, wu, wd)
    return _combine(dest3, ys, x1, cw, gt2)


def _alibi_slopes():
    n = SWA_HEADS + DSA_HEADS
    i = jnp.arange(1, n + 1, dtype=F32)
    return jnp.exp2(-8.0 * i / n)


def _layer(x, mod, ln1_g, w_in, swa_q_norm, swa_k_norm, swa_sinks, dsa_q_norm, dsa_k_norm,
           w_branch_a, w_branch_b, w_out, ln2_g, w_group, b_group, w_expert, b_expert,
           w_gate_up, w_up, w_down, slopes):
    b, s, d = x.shape
    sh1, sc1, gt1, sh2, sc2, gt2 = [m[:, None, :] for m in jnp.split(mod, 6, axis=-1)]

    w_gate = w_in[:, COL_GATE:COL_GATE + 2 * d].astype(BF16)

    h1 = _ln_mod(x, ln1_g[None, :], sc1, sh1)
    h1f = h1.reshape(b * s, d)
    proj = _mm(h1f, w_in, ATT_WIDTH, 1024, 768, F32).reshape(b, s, ATT_WIDTH)
    oa = _swa(proj, swa_sinks, slopes[:SWA_HEADS], swa_q_norm[None, :], swa_k_norm[None, :])
    ob = _dsa(proj, slopes[SWA_HEADS:], dsa_q_norm[None, :], dsa_k_norm[None, :])
    merged = _merge(h1f, oa.reshape(b * s, -1), ob.reshape(b * s, -1), w_gate,
                    w_branch_a.astype(BF16), w_branch_b.astype(BF16))

    w_r = jnp.concatenate([w_expert, w_group,
                           jnp.zeros((d, LANES - N_EXPERTS - N_GROUPS), F32)], axis=1)
    b_r = jnp.concatenate([b_expert, b_group,
                           jnp.zeros((LANES - N_EXPERTS - N_GROUPS,), F32)])[None, :]
    w_r_hi = _bf16_head(w_r)
    w_r_split = jnp.stack([w_r_hi.astype(BF16), (w_r - w_r_hi).astype(BF16)])
    x1, h2p, ids, cw = _outproj(merged.reshape(b, s, d), w_out.astype(BF16), x, gt1,
                                ln2_g[None, :], sc2, sh2, w_r_split, b_r)
    f = w_gate_up.shape[-1]
    return _moe(h2p, ids, cw, w_gate_up.reshape(N_EXPERTS, d, f), w_up.reshape(N_EXPERTS, d, f),
                w_down.reshape(N_EXPERTS, f, d), x1, gt2)


def kernel(x, c, w_ada, b_ada, ln1_g, w_in, swa_q_norm, swa_k_norm, swa_sinks, dsa_q_norm, dsa_k_norm,
           w_branch_a, w_branch_b, w_out, ln2_g, w_group, b_group, w_expert, b_expert,
           w_gate_up, w_up, w_down):
    slopes = _alibi_slopes()
    bsz = c.shape[0]
    c_pad = jnp.concatenate([c, jnp.zeros((8 - bsz, c.shape[1]), c.dtype)], axis=0)
    for l in range(w_ada.shape[0]):
        mod = _ada(c_pad, w_ada[l], b_ada[l][None, :])[:bsz]
        x = _layer(x, mod, ln1_g[l], w_in[l], swa_q_norm[l], swa_k_norm[l], swa_sinks[l],
                   dsa_q_norm[l], dsa_k_norm[l], w_branch_a[l], w_branch_b[l], w_out[l], ln2_g[l],
                   w_group[l], b_group[l], w_expert[l], b_expert[l],
                   w_gate_up[l], w_up[l], w_down[l], slopes)
    return x
```

```python
import functools

import jax
import jax.numpy as jnp
from jax import lax
from jax.experimental import pallas as pl
from jax.experimental.pallas import tpu as pltpu

F32 = jnp.float32
BF16 = jnp.bfloat16
I32 = jnp.int32

D_MODEL = 2048
HEAD_DIM = 64
SWA_HEADS = 16
SWA_KV_HEADS = 4
SWA_GROUP = SWA_HEADS // SWA_KV_HEADS
WINDOW = 128
DSA_HEADS = 16
IDX_HEADS = 16
IDX_DIM = 64
DSA_TOPK_MAX = 256
DSA_BLOCK = 128
N_GROUPS = 4
EXPERTS_PER_GROUP = 8
N_EXPERTS = N_GROUPS * EXPERTS_PER_GROUP
D_EXPERT = 512
RMS_EPS = 1e-6
NEG_INF = -1e30
INT_MIN = -(2 ** 31)

LANES = 128
VMEM_LIMIT = 56 * 1024 * 1024

COL_QA, COL_KA, COL_VA, COL_QB = 0, 1024, 1280, 1536
COL_KVB = 2560
COL_QI = 2688
COL_KWI = 3712
COL_GATE = 3792
ATT_WIDTH = 3840
KVB_K, KVB_V = 0, HEAD_DIM
KWI_K, KWI_W = 0, IDX_DIM

NT_DIMS = (((1,), (1,)), ((), ()))


def _cparams(*sem):
    return pltpu.CompilerParams(dimension_semantics=sem, vmem_limit_bytes=VMEM_LIMIT)


def _sigmoid(x):
    return 1.0 / (1.0 + jnp.exp(-x))


def _rms(x, g):
    return x * lax.rsqrt(jnp.mean(x * x, axis=-1, keepdims=True) + RMS_EPS) * g


def _ada_kernel(c_ref, w_ref, b_ref, o_ref):
    c = c_ref[...]
    cond = c * _sigmoid(c)
    o_ref[...] = jnp.dot(cond, w_ref[...], preferred_element_type=F32,
                         precision=lax.Precision.HIGHEST) + b_ref[...]


def _ada(c_pad, w, b):
    n = w.shape[1]
    tn = 1024
    return pl.pallas_call(
        _ada_kernel,
        grid=(n // tn,),
        in_specs=[pl.BlockSpec((8, D_MODEL), lambda j: (0, 0)),
                  pl.BlockSpec((D_MODEL, tn), lambda j: (0, j)),
                  pl.BlockSpec((1, tn), lambda j: (0, j))],
        out_specs=pl.BlockSpec((8, tn), lambda j: (0, j)),
        out_shape=jax.ShapeDtypeStruct((8, n), F32),
        compiler_params=_cparams("parallel"),
        name="ada_mod",
    )(c_pad, w, b)


def _ln_mod_kernel(x_ref, g_ref, sc_ref, sh_ref, o_ref):
    y = _rms(x_ref[0], g_ref[...])
    o_ref[0] = (y * (1.0 + sc_ref[0]) + sh_ref[0]).astype(o_ref.dtype)


def _ln_mod(x, g, sc, sh):
    b, s, d = x.shape
    ts = 512
    return pl.pallas_call(
        _ln_mod_kernel,
        grid=(b, s // ts),
        in_specs=[pl.BlockSpec((1, ts, d), lambda i, j: (i, j, 0)),
                  pl.BlockSpec((1, d), lambda i, j: (0, 0)),
                  pl.BlockSpec((1, 1, d), lambda i, j: (i, 0, 0)),
                  pl.BlockSpec((1, 1, d), lambda i, j: (i, 0, 0))],
        out_specs=pl.BlockSpec((1, ts, d), lambda i, j: (i, j, 0)),
        out_shape=jax.ShapeDtypeStruct((b, s, d), BF16),
        compiler_params=_cparams("parallel", "parallel"),
        name="ln_mod",
    )(x, g, sc, sh)


def _mm_kernel(a_ref, wt_ref, o_ref):
    o_ref[...] = lax.dot_general(a_ref[...], wt_ref[...].astype(BF16), NT_DIMS,
                                 preferred_element_type=F32).astype(o_ref.dtype)


def _mm(a, wt, n, tm, tn, out_dtype):
    m, k = a.shape
    return pl.pallas_call(
        _mm_kernel,
        grid=(m // tm, n // tn),
        in_specs=[pl.BlockSpec((tm, k), lambda i, j: (i, 0)),
                  pl.BlockSpec((tn, k), lambda i, j: (j, 0))],
        out_specs=pl.BlockSpec((tm, tn), lambda i, j: (i, j)),
        out_shape=jax.ShapeDtypeStruct((m, n), out_dtype),
        compiler_params=_cparams("parallel", "parallel"),
        name="proj_att",
    )(a, wt)


def _pair_rms(xp, half, gain2):
    sq = xp * xp
    ms_lo = jnp.sum(jnp.where(half, sq, 0.0), axis=-1, keepdims=True) / HEAD_DIM
    ms_hi = jnp.sum(jnp.where(half, 0.0, sq), axis=-1, keepdims=True) / HEAD_DIM
    return xp * lax.rsqrt(jnp.where(half, ms_lo, ms_hi) + RMS_EPS) * gain2


def _swa_kernel(sink_ref, slope_ref, q_ref, kp_ref, kc_ref, vp_ref, vc_ref, qg_ref, kg_ref, o_ref, lhs, e_s):
    n = pl.program_id(1)
    w, w2 = WINDOW, 2 * WINDOW
    q = q_ref[0]
    kw = jnp.concatenate([kp_ref[0], kc_ref[0]], axis=0)
    vw = jnp.concatenate([vp_ref[0], vc_ref[0]], axis=0)
    gq2 = jnp.concatenate([qg_ref[...], qg_ref[...]], axis=1)
    gk2 = jnp.concatenate([kg_ref[...], kg_ref[...]], axis=1)
    half_q = lax.broadcasted_iota(I32, (w, LANES), 1) < HEAD_DIM
    half_k = lax.broadcasted_iota(I32, (w2, LANES), 1) < HEAD_DIM
    row = lax.broadcasted_iota(I32, (w, w2), 0)
    col = lax.broadcasted_iota(I32, (w, w2), 1)
    rel = row + w - col
    valid = (rel >= 0) & (rel < w) & ((col >= w) | (n > 0))
    mbias = jnp.where(valid, 0.0, NEG_INF)
    relf = rel.astype(F32)

    ones_col = jnp.ones((w2, LANES), BF16)
    for g in range(SWA_KV_HEADS):
        if g % 2 == 0:
            grp = slice((g // 2) * LANES, (g // 2 + 1) * LANES)
            kn = _pair_rms(kw[:, grp], half_k, gk2)
            kn_sw = pltpu.roll(kn, HEAD_DIM, axis=1)
            vp = vw[:, grp]
            vp_sw = pltpu.roll(vp, HEAD_DIM, axis=1)
        kdup = jnp.where(half_k, kn, kn_sw) if g % 2 == 0 else jnp.where(half_k, kn_sw, kn)
        vdup = jnp.where(half_k, vp, vp_sw) if g % 2 == 0 else jnp.where(half_k, vp_sw, vp)
        vx = jnp.concatenate([vdup.astype(BF16), ones_col], axis=1)
        for r in range(SWA_GROUP):
            h = g * SWA_GROUP + r
            if r % 2 == 0:
                qn = _pair_rms(q[:, (h // 2) * LANES:(h // 2 + 1) * LANES], half_q, gq2) * (HEAD_DIM ** -0.5)
            keep = half_q if r % 2 == 0 else jnp.logical_not(half_q)
            lhs[r * w:(r + 1) * w, :] = jnp.where(keep, qn, 0.0).astype(BF16)
        s_all = lax.dot_general(lhs[...], kdup.astype(BF16), NT_DIMS, preferred_element_type=F32)
        sink_e = []
        for r in range(SWA_GROUP):
            h = g * SWA_GROUP + r
            rows = slice(r * w, (r + 1) * w)
            s = s_all[rows, :] + (mbias - slope_ref[h] * relf)
            sink = jnp.full((w, LANES), sink_ref[h], F32)
            m = jnp.maximum(jnp.max(s, axis=-1, keepdims=True), sink)
            e_s[rows, :] = jnp.exp(s - jnp.concatenate([m, m], axis=1)).astype(BF16)
            sink_e.append(jnp.exp(sink - m))
        pv = jnp.dot(e_s[...], vx, preferred_element_type=F32)
        for r in range(0, SWA_GROUP, 2):
            h = g * SWA_GROUP + r
            ev, od = pv[r * w:(r + 1) * w, :], pv[(r + 1) * w:(r + 2) * w, :]
            den = jnp.where(half_q, ev[:, LANES:] + sink_e[r], od[:, LANES:] + sink_e[r + 1])
            o = jnp.where(half_q, ev[:, :LANES], od[:, :LANES]) / den
            o_ref[0, :, (h // 2) * LANES:(h // 2 + 1) * LANES] = o.astype(o_ref.dtype)


def _swa(proj3, sinks, slopes, q_gain, k_gain):
    b, s, _ = proj3.shape
    nb = s // WINDOW
    kvw = SWA_KV_HEADS * HEAD_DIM
    qw = SWA_HEADS * HEAD_DIM
    smem = pl.BlockSpec(memory_space=pltpu.SMEM)
    fixed = lambda i, n: (0, 0)
    prev = lambda i, n: (i, jnp.maximum(n - 1, 0), COL_KA // kvw)
    cur = lambda i, n: (i, n, COL_KA // kvw)
    prev_v = lambda i, n: (i, jnp.maximum(n - 1, 0), COL_VA // kvw)
    cur_v = lambda i, n: (i, n, COL_VA // kvw)
    return pl.pallas_call(
        _swa_kernel,
        grid=(b, nb),
        in_specs=[smem, smem,
                  pl.BlockSpec((1, WINDOW, qw), lambda i, n: (i, n, COL_QA // qw)),
                  pl.BlockSpec((1, WINDOW, kvw), prev),
                  pl.BlockSpec((1, WINDOW, kvw), cur),
                  pl.BlockSpec((1, WINDOW, kvw), prev_v),
                  pl.BlockSpec((1, WINDOW, kvw), cur_v),
                  pl.BlockSpec((1, HEAD_DIM), fixed),
                  pl.BlockSpec((1, HEAD_DIM), fixed)],
        out_specs=pl.BlockSpec((1, WINDOW, qw), lambda i, n: (i, n, 0)),
        out_shape=jax.ShapeDtypeStruct((b, s, qw), BF16),
        scratch_shapes=[pltpu.VMEM((SWA_GROUP * WINDOW, LANES), BF16),
                        pltpu.VMEM((SWA_GROUP * WINDOW, 2 * WINDOW), BF16)],
        compiler_params=_cparams("parallel", "parallel"),
        name="swa_attn",
    )(sinks, slopes, proj3, proj3, proj3, proj3, proj3, q_gain, k_gain)


DSA_CHUNK = 256
ALIBI_SPLIT = 3
POS_LO_BITS = 7


def _bf16_head(x):
    return lax.bitcast_convert_type(lax.bitcast_convert_type(x, I32) & jnp.int32(-65536), F32)


def _alibi_columns(slopes):
    parts, rest = [], slopes.astype(F32)
    for _ in range(ALIBI_SPLIT):
        piece = _bf16_head(rest)
        parts.append(piece)
        rest = rest - piece
    cols = [p * float(2 ** POS_LO_BITS) for p in parts] + parts
    pad = jnp.zeros((slopes.shape[0], LANES - 2 * ALIBI_SPLIT), F32)
    return jnp.concatenate([jnp.stack(cols, axis=1), pad], axis=1)


def _dsa_kernel(*refs, topk):
    n_pairs = DSA_HEADS // 2
    qb_refs, qi_refs = refs[:n_pairs], refs[n_pairs:2 * n_pairs]
    (kwi_blk_ref, kvb_ref, kwi_ref, acol_ref, qg_ref, kg_ref, o_ref,
     ki2, kx, vx, lhs_i, lhs_q, wb, key_s, keyt_s, mb_s, thr_s, acc_s, m_s, e_s, al_s) = refs[2 * n_pairs:]
    i = pl.program_id(1)
    blk, ck = DSA_BLOCK, DSA_CHUNK
    nch = (i + 2) // 2
    seq = ki2.shape[0]
    max_ch = seq // ck
    half = lax.broadcasted_iota(I32, (blk, LANES), 1) < HEAD_DIM

    @pl.when(i == 0)
    def _():
        kvb = kvb_ref[0]
        kib = kwi_ref[0][:, KWI_K:KWI_K + IDX_DIM].astype(BF16)
        ki2[...] = jnp.concatenate([kib, kib], axis=1)
        kn = _rms(kvb[:, KVB_K:KVB_K + HEAD_DIM], kg_ref[...]).astype(BF16)
        pos = lax.broadcasted_iota(I32, (seq, LANES), 0)
        lane = lax.broadcasted_iota(I32, (seq, LANES), 1)
        hi = lax.shift_right_logical(pos, POS_LO_BITS).astype(F32)
        lo = (pos & (2 ** POS_LO_BITS - 1)).astype(F32)
        posc = jnp.where(lane < ALIBI_SPLIT, hi, jnp.where(lane < 2 * ALIBI_SPLIT, lo, 0.0)).astype(BF16)
        kx[...] = jnp.concatenate([kn, kn, posc], axis=1)
        ones = jnp.where(lax.broadcasted_iota(I32, (seq, HEAD_DIM), 1) == 0, 1.0, 0.0).astype(BF16)
        vx[...] = jnp.concatenate([kvb[:, KVB_V:KVB_V + HEAD_DIM].astype(BF16), ones], axis=1)

    g2 = jnp.concatenate([qg_ref[...], qg_ref[...]], axis=1)
    wi = kwi_blk_ref[0][:, KWI_W:KWI_W + IDX_HEADS] * (IDX_HEADS ** -0.5)
    for p in range(n_pairs):
        qn = _pair_rms(qb_refs[p][0], half, g2) * (HEAD_DIM ** -0.5)
        qip = qi_refs[p][0] * (IDX_DIM ** -0.5)
        for k in range(2):
            h = 2 * p + k
            keep = half if k == 0 else jnp.logical_not(half)
            rows = slice(h * blk, (h + 1) * blk)
            lhs_i[rows, :] = jnp.where(keep, qip, 0.0).astype(BF16)
            lhs_q[rows, :LANES] = jnp.where(keep, qn, 0.0).astype(BF16)
            lhs_q[rows, LANES:] = jnp.broadcast_to(acol_ref[h:h + 1, :], (blk, LANES)).astype(BF16)
            wb[h] = jnp.broadcast_to(wi[:, h:h + 1], (blk, LANES))

    row_t = i * blk + lax.broadcasted_iota(I32, (blk, ck), 0)
    col_k = lax.broadcasted_iota(I32, (blk, ck), 1)

    def score_chunk(c, carry):
        ks = pl.ds(pl.multiple_of(c * ck, ck), ck)
        logits = lax.dot_general(lhs_i[...], ki2[ks, :], NT_DIMS, preferred_element_type=F32)
        sc = jnp.zeros((blk, ck), F32)
        for h in range(IDX_HEADS):
            w = wb[h]
            lh = jnp.maximum(logits[h * blk:(h + 1) * blk, :], 0.0)
            sc = sc + lh * jnp.concatenate([w, w], axis=1)
        bits = lax.bitcast_convert_type(sc + 0.0, I32)
        key = jnp.where(bits < 0, bits ^ jnp.int32(0x7FFFFFFF), bits)
        key = jnp.where(c * ck + col_k <= row_t, key, jnp.int32(INT_MIN))
        key_s[c] = key
        keyt_s[c] = key.T
        return carry

    lax.fori_loop(0, nch, score_chunk, 0)

    thr_s[...] = jnp.full(thr_s.shape, INT_MIN, I32)
    for n in range(2, max_ch + 1):
        @pl.when(nch == n)
        def _(n=n):
            def step(it, thr):
                cand = thr + lax.shift_left(jnp.int32(1), 31 - it)
                cnt = jnp.zeros((1, blk), F32)
                for c in range(n):
                    cnt = cnt + jnp.sum(jnp.where(keyt_s[c] >= cand, 1.0, 0.0), axis=0, keepdims=True)
                return jnp.where(cnt >= float(topk), cand, thr)

            thr = lax.fori_loop(0, 32, step, jnp.full((1, blk), INT_MIN, I32))
            thr_s[...] = jnp.broadcast_to(thr, thr_s.shape).T

    thr = thr_s[...]
    thr2 = jnp.concatenate([thr, thr], axis=1)

    def count_gt(c, cnt):
        k = key_s[c]
        return cnt + jnp.where(k[:, :LANES] > thr, 1.0, 0.0) + jnp.where(k[:, LANES:] > thr, 1.0, 0.0)

    n_gt = jnp.sum(lax.fori_loop(0, nch, count_gt, jnp.zeros((blk, LANES), F32)), axis=-1, keepdims=True)
    need = float(topk) - n_gt
    upper = (lax.broadcasted_iota(I32, (ck, ck), 0) < lax.broadcasted_iota(I32, (ck, ck), 1)).astype(BF16)

    def mask_chunk(c, n_eq):
        k = key_s[c]
        eq = (k == thr2) & (c * ck + col_k <= row_t)
        eqf = eq.astype(BF16)
        before = jnp.dot(eqf, upper, preferred_element_type=F32) + n_eq
        sel = (k > thr2) | (eq & (before < need))
        mb_s[c] = jnp.where(sel, 0.0, NEG_INF)
        return n_eq + jnp.sum(eqf.astype(F32), axis=-1, keepdims=True)

    lax.fori_loop(0, nch, mask_chunk, jnp.zeros((blk, 1), F32))

    acc_s[...] = jnp.zeros_like(acc_s)
    m_s[...] = jnp.full(m_s.shape, -jnp.inf, F32)

    def attend(c, carry):
        ks = pl.ds(pl.multiple_of(c * ck, ck), ck)
        s_all = lax.dot_general(lhs_q[...], kx[ks, :], NT_DIMS, preferred_element_type=F32)
        mb = mb_s[c]
        for h in range(DSA_HEADS):
            rows = slice(h * blk, (h + 1) * blk)
            s = s_all[rows, :] + mb
            m_old = m_s[rows, :]
            m_new = jnp.maximum(m_old, jnp.max(s, axis=-1, keepdims=True))
            al_s[rows, :] = jnp.exp(m_old - m_new)
            e_s[rows, :] = jnp.exp(s - jnp.concatenate([m_new, m_new], axis=1)).astype(BF16)
            m_s[rows, :] = m_new
        pv = jnp.dot(e_s[...], vx[ks, :], preferred_element_type=F32)
        acc_s[...] = acc_s[...] * al_s[...] + pv
        return carry

    lax.fori_loop(0, nch, attend, 0)

    for h in range(DSA_HEADS):
        a = acc_s[h * blk:(h + 1) * blk, :]
        o = a[:, :HEAD_DIM] / a[:, HEAD_DIM:HEAD_DIM + 1]
        o_ref[0, :, h * HEAD_DIM:(h + 1) * HEAD_DIM] = o.astype(o_ref.dtype)


def _dsa(proj3, slopes, q_gain, k_gain):
    b, s, _ = proj3.shape
    nb = s // DSA_BLOCK
    qw = DSA_HEADS * HEAD_DIM
    n_pairs = DSA_HEADS // 2
    rows = DSA_HEADS * DSA_BLOCK
    fixed = lambda i, n: (0, 0)

    def group(col):
        return pl.BlockSpec((1, DSA_BLOCK, LANES), lambda i, n: (i, n, col // LANES))

    def group_all(col):
        return pl.BlockSpec((1, s, LANES), lambda i, n: (i, 0, col // LANES))

    return pl.pallas_call(
        functools.partial(_dsa_kernel, topk=min(DSA_TOPK_MAX, s // 4)),
        grid=(b, nb),
        in_specs=([group(COL_QB + p * LANES) for p in range(n_pairs)]
                  + [group(COL_QI + p * LANES) for p in range(n_pairs)]
                  + [group(COL_KWI), group_all(COL_KVB), group_all(COL_KWI),
                     pl.BlockSpec((DSA_HEADS, LANES), fixed),
                     pl.BlockSpec((1, HEAD_DIM), fixed),
                     pl.BlockSpec((1, HEAD_DIM), fixed)]),
        out_specs=pl.BlockSpec((1, DSA_BLOCK, qw), lambda i, n: (i, n, 0)),
        out_shape=jax.ShapeDtypeStruct((b, s, qw), BF16),
        scratch_shapes=[pltpu.VMEM((s, LANES), BF16),
                        pltpu.VMEM((s, 2 * LANES), BF16),
                        pltpu.VMEM((s, LANES), BF16),
                        pltpu.VMEM((rows, LANES), BF16),
                        pltpu.VMEM((rows, 2 * LANES), BF16),
                        pltpu.VMEM((IDX_HEADS, DSA_BLOCK, LANES), F32),
                        pltpu.VMEM((s // DSA_CHUNK, DSA_BLOCK, DSA_CHUNK), I32),
                        pltpu.VMEM((s // DSA_CHUNK, DSA_CHUNK, DSA_BLOCK), I32),
                        pltpu.VMEM((s // DSA_CHUNK, DSA_BLOCK, DSA_CHUNK), F32),
                        pltpu.VMEM((DSA_BLOCK, LANES), I32),
                        pltpu.VMEM((rows, LANES), F32),
                        pltpu.VMEM((rows, LANES), F32),
                        pltpu.VMEM((rows, DSA_CHUNK), BF16),
                        pltpu.VMEM((rows, LANES), F32)],
        compiler_params=_cparams("arbitrary", "arbitrary"),
        name="dsa_attn",
    )(*([proj3] * (2 * n_pairs + 3)), _alibi_columns(slopes), q_gain, k_gain)


def _merge_kernel(h_ref, oa_ref, ob_ref, wt_hbm, wa_ref, wb_ref, o_ref, g32, gbf, sem):
    j, i = pl.program_id(0), pl.program_id(1)
    tn, d = o_ref.shape[1], h_ref.shape[1]

    @pl.when(i == 0)
    def _():
        copies = [pltpu.make_async_copy(
            wt_hbm.at[pl.ds(pl.multiple_of(COL_GATE + k * d + j * tn, 8), tn)], g32.at[k], sem.at[k])
            for k in range(2)]
        for c in copies:
            c.start()
        for c in copies:
            c.wait()
        gbf[...] = g32[...].astype(BF16)

    h = h_ref[...]
    ga = lax.dot_general(h, gbf[0], NT_DIMS, preferred_element_type=F32)
    gb = lax.dot_general(h, gbf[1], NT_DIMS, preferred_element_type=F32)
    a = jnp.dot(oa_ref[...], wa_ref[...], preferred_element_type=F32)
    b = jnp.dot(ob_ref[...], wb_ref[...], preferred_element_type=F32)
    o_ref[...] = (_sigmoid(ga) * a + _sigmoid(gb) * b).astype(o_ref.dtype)


def _merge(h, oa, ob, w_in_t, wa, wb):
    m, d = h.shape
    tm, tn = 1024, 512
    return pl.pallas_call(
        _merge_kernel,
        grid=(d // tn, m // tm),
        in_specs=[pl.BlockSpec((tm, d), lambda j, i: (i, 0)),
                  pl.BlockSpec((tm, oa.shape[1]), lambda j, i: (i, 0)),
                  pl.BlockSpec((tm, ob.shape[1]), lambda j, i: (i, 0)),
                  pl.BlockSpec(memory_space=pl.ANY),
                  pl.BlockSpec((wa.shape[0], tn), lambda j, i: (0, j)),
                  pl.BlockSpec((wb.shape[0], tn), lambda j, i: (0, j))],
        out_specs=pl.BlockSpec((tm, tn), lambda j, i: (i, j)),
        out_shape=jax.ShapeDtypeStruct((m, d), BF16),
        scratch_shapes=[pltpu.VMEM((2, tn, d), w_in_t.dtype),
                        pltpu.VMEM((2, tn, d), BF16),
                        pltpu.SemaphoreType.DMA((2,))],
        compiler_params=_cparams("arbitrary", "arbitrary"),
        name="merge_branches",
    )(h, oa, ob, w_in_t, wa, wb)


MOE_TILE = 256
MOE_SLOTS = 2
TOKEN_TILE = 512


def _moe_max_tiles(n_tokens):
    return -(-(n_tokens * MOE_SLOTS + N_EXPERTS * (MOE_TILE - 1)) // MOE_TILE)


def _pack_bf16_pair(x):
    k = x.shape[1] // 2
    bits = lax.bitcast_convert_type(x.astype(BF16).astype(F32), I32)
    return lax.shift_right_logical(bits[:, :k], 16) | (bits[:, k:] & jnp.int32(-65536))


def _unpack_bf16_pair(w):
    lo = lax.bitcast_convert_type(lax.shift_left(w, 16), F32).astype(BF16)
    hi = lax.bitcast_convert_type(w & jnp.int32(-65536), F32).astype(BF16)
    return lo, hi


def _outproj_kernel(mg_ref, wo_ref, x_ref, gt_ref, g2_ref, sc_ref, sh_ref, wr_ref, br_ref,
                    x1_ref, h2p_ref, ids_ref, cw_ref):
    y = jnp.dot(mg_ref[0], wo_ref[...], preferred_element_type=F32)
    x1 = x_ref[0] + gt_ref[0] * y
    x1_ref[0] = x1
    h2 = _rms(x1, g2_ref[...]) * (1.0 + sc_ref[0]) + sh_ref[0]
    h2p_ref[0] = _pack_bf16_pair(h2)

    h_hi = h2.astype(BF16)
    h_lo = (h2 - h_hi.astype(F32)).astype(BF16)
    logits = (jnp.dot(h_hi, wr_ref[0], preferred_element_type=F32)
              + jnp.dot(h_lo, wr_ref[0], preferred_element_type=F32)
              + jnp.dot(h_hi, wr_ref[1], preferred_element_type=F32)) + br_ref[...]
    lane = lax.broadcasted_iota(I32, logits.shape, 1)
    is_g = (lane >= N_EXPERTS) & (lane < N_EXPERTS + N_GROUPS)
    neg = -jnp.inf
    gl = jnp.where(is_g, logits, neg)
    gmax = jnp.max(gl, axis=-1, keepdims=True)
    g_idx = jnp.min(jnp.where(gl == gmax, lane - N_EXPERTS, N_GROUPS), axis=-1, keepdims=True)
    p_g = 1.0 / jnp.sum(jnp.where(is_g, jnp.exp(gl - gmax), 0.0), axis=-1, keepdims=True)
    in_grp = (lane < N_EXPERTS) & ((lane // EXPERTS_PER_GROUP) == g_idx)
    el = jnp.where(in_grp, logits, neg)
    m1 = jnp.max(el, axis=-1, keepdims=True)
    i1 = jnp.min(jnp.where(el == m1, lane, LANES), axis=-1, keepdims=True)
    el2 = jnp.where(lane == i1, neg, el)
    m2 = jnp.max(el2, axis=-1, keepdims=True)
    i2 = jnp.min(jnp.where(el2 == m2, lane, LANES), axis=-1, keepdims=True)
    e2 = jnp.exp(m2 - m1)
    w1 = 1.0 / (1.0 + e2)
    w2 = e2 / (1.0 + e2)
    ids_ref[0] = jnp.where(lane == 0, i1, jnp.where(lane == 1, i2, 0))
    cw_ref[0] = p_g * jnp.where(lane == 0, w1, jnp.where(lane == 1, w2, 0.0))


def _outproj(merged3, w_out, x, gt1, g2, sc2, sh2, w_r, b_r):
    b, s, d = x.shape
    tm = 512
    row = lambda i, j: (i, j, 0)
    per_b = lambda i, j: (i, 0, 0)
    fixed = lambda i, j: (0, 0)
    return pl.pallas_call(
        _outproj_kernel,
        grid=(b, s // tm),
        in_specs=[pl.BlockSpec((1, tm, d), row),
                  pl.BlockSpec((d, d), fixed, pipeline_mode=pl.Buffered(1)),
                  pl.BlockSpec((1, tm, d), row),
                  pl.BlockSpec((1, 1, d), per_b),
                  pl.BlockSpec((1, d), fixed),
                  pl.BlockSpec((1, 1, d), per_b),
                  pl.BlockSpec((1, 1, d), per_b),
                  pl.BlockSpec((2, d, LANES), lambda i, j: (0, 0, 0)),
                  pl.BlockSpec((1, LANES), fixed)],
        out_specs=[pl.BlockSpec((1, tm, d), row),
                   pl.BlockSpec((1, tm, d // 2), row),
                   pl.BlockSpec((1, tm, LANES), row),
                   pl.BlockSpec((1, tm, LANES), row)],
        out_shape=[jax.ShapeDtypeStruct((b, s, d), F32),
                   jax.ShapeDtypeStruct((b, s, d // 2), I32),
                   jax.ShapeDtypeStruct((b, s, LANES), I32),
                   jax.ShapeDtypeStruct((b, s, LANES), F32)],
        compiler_params=_cparams("parallel", "parallel"),
        name="outproj_ln2_router",
    )(merged3, w_out, x, gt1, g2, sc2, sh2, w_r, b_r)


def _plan_kernel(ids_ref, dest_ref, te_ref, rank_ref):
    n = ids_ref.shape[0]
    cb = 256
    lane = lax.broadcasted_iota(I32, (cb, LANES), 1)
    lower = (lax.broadcasted_iota(I32, (cb, cb), 0) > lax.broadcasted_iota(I32, (cb, cb), 1)).astype(BF16)

    def count(bi, carry):
        rows = pl.ds(pl.multiple_of(bi * cb, cb), cb)
        ids = ids_ref[rows, :]
        m1 = lane == ids[:, 0:1]
        m2 = lane == ids[:, 1:2]
        onehot = m1 | m2
        before = jnp.dot(lower, onehot.astype(BF16), preferred_element_type=F32) + carry
        r1 = jnp.sum(jnp.where(m1, before, 0.0), axis=-1, keepdims=True)
        r2 = jnp.sum(jnp.where(m2, before, 0.0), axis=-1, keepdims=True)
        rank_ref[rows, :] = jnp.where(lane == 0, r1, jnp.where(lane == 1, r2, 0.0))
        return carry + jnp.sum(onehot.astype(F32), axis=0, keepdims=True)

    counts = lax.fori_loop(0, n // cb, count, jnp.zeros((1, LANES), F32))
    padded = jnp.floor((counts + (MOE_TILE - 1)) / MOE_TILE) * MOE_TILE
    upper = (lax.broadcasted_iota(I32, (LANES, LANES), 0)
             < lax.broadcasted_iota(I32, (LANES, LANES), 1)).astype(F32)
    offs = jnp.dot(jnp.broadcast_to(padded, (8, LANES)), upper, preferred_element_type=F32,
                   precision=lax.Precision.HIGHEST)[0:1]

    def place(bi, c):
        rows = pl.ds(pl.multiple_of(bi * cb, cb), cb)
        ids = ids_ref[rows, :]
        o1 = jnp.sum(jnp.where(lane == ids[:, 0:1], offs, 0.0), axis=-1, keepdims=True)
        o2 = jnp.sum(jnp.where(lane == ids[:, 1:2], offs, 0.0), axis=-1, keepdims=True)
        slot = rank_ref[rows, :] + jnp.where(lane == 0, o1, jnp.where(lane == 1, o2, 0.0))
        dest_ref[rows, :] = slot.astype(I32)
        return c

    lax.fori_loop(0, n // cb, place, 0)

    sq_row = lax.broadcasted_iota(I32, (LANES, LANES), 0)
    sq_lane = lax.broadcasted_iota(I32, (LANES, LANES), 1)
    tile_start = sq_row.astype(F32) * MOE_TILE
    finished = ((offs + padded) <= tile_start) & (sq_lane < N_EXPERTS)
    tile_expert = jnp.minimum(jnp.sum(finished.astype(F32), axis=-1, keepdims=True), N_EXPERTS - 1.0)
    n_tiles = jnp.sum(padded, axis=-1, keepdims=True) / MOE_TILE
    te_ref[...] = jnp.where(sq_lane == 1, n_tiles, tile_expert).astype(I32)


def _plan(ids):
    n = ids.shape[0]
    return pl.pallas_call(
        _plan_kernel,
        out_shape=[jax.ShapeDtypeStruct((n, LANES), I32),
                   jax.ShapeDtypeStruct((LANES, LANES), I32)],
        scratch_shapes=[pltpu.VMEM((n, LANES), F32)],
        compiler_params=pltpu.CompilerParams(vmem_limit_bytes=VMEM_LIMIT),
        name="moe_plan",
    )(ids)


def _dispatch_kernel(dest_ref, h_ref, hs_zero_ref, hs_ref, sem):
    del hs_zero_ref
    tq = h_ref.shape[0]

    def issue(r, c):
        for k in range(MOE_SLOTS):
            slot = dest_ref[0, 0, MOE_SLOTS * r + k]
            pltpu.make_async_copy(h_ref.at[pl.ds(r, 1)], hs_ref.at[pl.ds(slot, 1)], sem).start(priority=k)
        return c

    lax.fori_loop(0, tq, issue, 0, unroll=8)
    done = hs_ref.at[pl.ds(0, MOE_SLOTS * tq)]
    pltpu.make_async_copy(done, done, sem).wait()


def _dispatch(dest3, h2p, n_rows):
    n, w = h2p.shape
    tq = TOKEN_TILE
    hs_zero = jnp.zeros((n_rows, w), h2p.dtype)
    return pl.pallas_call(
        _dispatch_kernel,
        grid=(n // tq,),
        in_specs=[pl.BlockSpec((1, 1, MOE_SLOTS * tq), lambda i: (i, 0, 0), memory_space=pltpu.SMEM),
                  pl.BlockSpec((tq, w), lambda i: (i, 0)),
                  pl.BlockSpec(memory_space=pl.ANY)],
        out_specs=pl.BlockSpec(memory_space=pl.ANY),
        out_shape=jax.ShapeDtypeStruct((n_rows, w), h2p.dtype),
        scratch_shapes=[pltpu.SemaphoreType.DMA(())],
        input_output_aliases={2: 0},
        compiler_params=_cparams("arbitrary"),
        name="moe_dispatch",
    )(dest3, h2p, hs_zero)


def _expert_kernel(te_ref, nt_ref, hs_ref, wg_hbm, wu_hbm, wd_hbm, ys_ref,
                   wg_buf, wu_buf, wd_buf, sem, slot_ref):
    t = pl.program_id(0)
    nt = nt_ref[0]
    used = t < nt
    last = te_ref.shape[0] - 1

    def weight_copies(e, slot):
        return (pltpu.make_async_copy(wg_hbm.at[e], wg_buf.at[slot], sem.at[0, slot]),
                pltpu.make_async_copy(wu_hbm.at[e], wu_buf.at[slot], sem.at[1, slot]),
                pltpu.make_async_copy(wd_hbm.at[e], wd_buf.at[slot], sem.at[2, slot]))

    @pl.when(t == 0)
    def _():
        slot_ref[0] = 0
        for c in weight_copies(te_ref[0], 0):
            c.start()

    @pl.when(jnp.logical_not(used))
    def _():
        ys_ref[...] = jnp.zeros_like(ys_ref)

    @pl.when(used)
    def _():
        e = te_ref[t]
        first = (t == 0) | (te_ref[jnp.maximum(t - 1, 0)] != e)

        @pl.when(first & (t > 0))
        def _():
            slot_ref[0] = 1 - slot_ref[0]

        slot = slot_ref[0]

        @pl.when(first)
        def _():
            for c in weight_copies(e, slot):
                c.wait()
            nxt = lax.while_loop(lambda s: (s < nt) & (te_ref[jnp.minimum(s, last)] == e),
                                 lambda s: s + 1, t + 1)

            @pl.when(nxt < nt)
            def _():
                for c in weight_copies(te_ref[jnp.minimum(nxt, last)], 1 - slot):
                    c.start()

        lo, hi = _unpack_bf16_pair(hs_ref[...])
        k = lo.shape[1]

        def up(w_buf):
            return (jnp.dot(lo, w_buf[slot, :k, :].astype(BF16), preferred_element_type=F32)
                    + jnp.dot(hi, w_buf[slot, k:, :].astype(BF16), preferred_element_type=F32))

        a = up(wg_buf)
        u = up(wu_buf)
        act = (a * _sigmoid(a) * u).astype(BF16)
        ys_ref[...] = jnp.dot(act, wd_buf[slot].astype(BF16), preferred_element_type=F32)


def _experts(tile_expert, n_tiles, hs, wg, wu, wd):
    n_rows, hw = hs.shape
    d, f = wg.shape[1], wg.shape[2]
    row = lambda t, te, nt: (jnp.minimum(t, nt[0] - 1), 0)
    hbm = pl.BlockSpec(memory_space=pl.ANY)
    return pl.pallas_call(
        _expert_kernel,
        grid_spec=pltpu.PrefetchScalarGridSpec(
            num_scalar_prefetch=2,
            grid=(n_rows // MOE_TILE,),
            in_specs=[pl.BlockSpec((MOE_TILE, hw), row), hbm, hbm, hbm],
            out_specs=pl.BlockSpec((MOE_TILE, d), lambda t, te, nt: (t, 0)),
            scratch_shapes=[pltpu.VMEM((2, d, f), wg.dtype),
                            pltpu.VMEM((2, d, f), wu.dtype),
                            pltpu.VMEM((2, f, d), wd.dtype),
                            pltpu.SemaphoreType.DMA((3, 2)),
                            pltpu.SMEM((1,), I32)]),
        out_shape=jax.ShapeDtypeStruct((n_rows, d), F32),
        compiler_params=_cparams("arbitrary"),
        name="moe_experts",
    )(tile_expert, n_tiles, hs, wg, wu, wd)


def _combine_kernel(dest_ref, ys_ref, x1_ref, cw_ref, gt_ref, o_ref, buf, sem):
    tq = x1_ref.shape[1]

    def issue(r, c):
        for k in range(MOE_SLOTS):
            slot = dest_ref[0, 0, MOE_SLOTS * r + k]
            pltpu.make_async_copy(ys_ref.at[pl.ds(slot, 1)], buf.at[k, pl.ds(r, 1)], sem).start(priority=k)
        return c

    lax.fori_loop(0, tq, issue, 0, unroll=8)
    pltpu.make_async_copy(buf, buf, sem).wait()
    cw = cw_ref[0]
    y = cw[:, 0:1] * buf[0] + cw[:, 1:2] * buf[1]
    o_ref[0] = x1_ref[0] + gt_ref[0] * y


def _combine(dest3, ys, x1, cw, gt2):
    b, s, d = x1.shape
    tq = TOKEN_TILE
    spb = s // tq
    row = lambda i, j: (i, j, 0)
    return pl.pallas_call(
        _combine_kernel,
        grid=(b, spb),
        in_specs=[pl.BlockSpec((1, 1, MOE_SLOTS * tq), lambda i, j: (i * spb + j, 0, 0),
                               memory_space=pltpu.SMEM),
                  pl.BlockSpec(memory_space=pl.ANY),
                  pl.BlockSpec((1, tq, d), row),
                  pl.BlockSpec((1, tq, LANES), row),
                  pl.BlockSpec((1, 1, d), lambda i, j: (i, 0, 0))],
        out_specs=pl.BlockSpec((1, tq, d), row),
        out_shape=jax.ShapeDtypeStruct((b, s, d), F32),
        scratch_shapes=[pltpu.VMEM((MOE_SLOTS, tq, d), F32), pltpu.SemaphoreType.DMA(())],
        compiler_params=_cparams("arbitrary", "arbitrary"),
        name="moe_combine",
    )(dest3, ys, x1, cw, gt2)


def _moe(h2p, ids, cw, wg, wu, wd, x1, gt2):
    b, s, d = x1.shape
    n = b * s
    n_rows = _moe_max_tiles(n) * MOE_TILE
    dest, te = _plan(ids.reshape(n, LANES))
    dest3 = dest[:, :MOE_SLOTS].reshape(n // TOKEN_TILE, 1, MOE_SLOTS * TOKEN_TILE)
    hs = _dispatch(dest3, h2p.reshape(n, d // 2), n_rows)
    ys = _experts(te[:n_rows // MOE_TILE, 0], te[0, 1:2], hs, wg, wu, wd)
    return _combine(dest3, ys, x1, cw, gt2)


def _alibi_slopes():
    n = SWA_HEADS + DSA_HEADS
    i = jnp.arange(1, n + 1, dtype=F32)
    return jnp.exp2(-8.0 * i / n)


def _layer(x, mod, ln1_g, w_in, swa_q_norm, swa_k_norm, swa_sinks, dsa_q_norm, dsa_k_norm,
           w_branch_a, w_branch_b, w_out, ln2_g, w_group, b_group, w_expert, b_expert,
           w_gate_up, w_up, w_down, slopes):
    b, s, d = x.shape
    sh1, sc1, gt1, sh2, sc2, gt2 = [m[:, None, :] for m in jnp.split(mod, 6, axis=-1)]

    w_in_t = jnp.swapaxes(w_in, 0, 1)

    h1 = _ln_mod(x, ln1_g[None, :], sc1, sh1)
    h1f = h1.reshape(b * s, d)
    proj = _mm(h1f, w_in_t, ATT_WIDTH, 1024, 768, F32).reshape(b, s, ATT_WIDTH)
    oa = _swa(proj, swa_sinks, slopes[:SWA_HEADS], swa_q_norm[None, :], swa_k_norm[None, :])
    ob = _dsa(proj, slopes[SWA_HEADS:], dsa_q_norm[None, :], dsa_k_norm[None, :])
    merged = _merge(h1f, oa.reshape(b * s, -1), ob.reshape(b * s, -1), w_in_t,
                    w_branch_a.astype(BF16), w_branch_b.astype(BF16))

    w_r = jnp.concatenate([w_expert, w_group,
                           jnp.zeros((d, LANES - N_EXPERTS - N_GROUPS), F32)], axis=1)
    b_r = jnp.concatenate([b_expert, b_group,
                           jnp.zeros((LANES - N_EXPERTS - N_GROUPS,), F32)])[None, :]
    w_r_hi = _bf16_head(w_r)
    w_r_split = jnp.stack([w_r_hi.astype(BF16), (w_r - w_r_hi).astype(BF16)])
    x1, h2p, ids, cw = _outproj(merged.reshape(b, s, d), w_out.astype(BF16), x, gt1,
                                ln2_g[None, :], sc2, sh2, w_r_split, b_r)
    f = w_gate_up.shape[-1]
    return _moe(h2p, ids, cw, w_gate_up.reshape(N_EXPERTS, d, f), w_up.reshape(N_EXPERTS, d, f),
                w_down.reshape(N_EXPERTS, f, d), x1, gt2)


def kernel(x, c, w_ada, b_ada, ln1_g, w_in, swa_q_norm, swa_k_norm, swa_sinks, dsa_q_norm, dsa_k_norm,
           w_branch_a, w_branch_b, w_out, ln2_g, w_group, b_group, w_expert, b_expert,
           w_gate_up, w_up, w_down):
    slopes = _alibi_slopes()
    bsz = c.shape[0]
    c_pad = jnp.concatenate([c, jnp.zeros((8 - bsz, c.shape[1]), c.dtype)], axis=0)
    for l in range(w_ada.shape[0]):
        mod = _ada(c_pad, w_ada[l], b_ada[l][None, :])[:bsz]
        x = _layer(x, mod, ln1_g[l], w_in[l], swa_q_norm[l], swa_k_norm[l], swa_sinks[l],
                   dsa_q_norm[l], dsa_k_norm[l], w_branch_a[l], w_branch_b[l], w_out[l], ln2_g[l],
                   w_group[l], b_group[l], w_expert[l], b_expert[l],
                   w_gate_up[l], w_up[l], w_down[l], slopes)
    return x
```

```python
import functools

import jax
import jax.numpy as jnp
from jax import lax
from jax.experimental import pallas as pl
from jax.experimental.pallas import tpu as pltpu

F32 = jnp.float32
BF16 = jnp.bfloat16
I32 = jnp.int32

D_MODEL = 2048
HEAD_DIM = 64
SWA_HEADS = 16
SWA_KV_HEADS = 4
SWA_GROUP = SWA_HEADS // SWA_KV_HEADS
WINDOW = 128
DSA_HEADS = 16
IDX_HEADS = 16
IDX_DIM = 64
DSA_TOPK_MAX = 256
DSA_BLOCK = 128
N_GROUPS = 4
EXPERTS_PER_GROUP = 8
N_EXPERTS = N_GROUPS * EXPERTS_PER_GROUP
D_EXPERT = 512
RMS_EPS = 1e-6
NEG_INF = -1e30
INT_MIN = -(2 ** 31)

LANES = 128
VMEM_LIMIT = 56 * 1024 * 1024

COL_QA, COL_KA, COL_VA, COL_QB = 0, 1024, 1280, 1536
COL_KVB = 2560
COL_QI = 2688
COL_KWI = 3712
COL_GATE = 3792
ATT_WIDTH = 3840
KVB_K, KVB_V = 0, HEAD_DIM
KWI_K, KWI_W = 0, IDX_DIM

NT_DIMS = (((1,), (1,)), ((), ()))


def _cparams(*sem):
    return pltpu.CompilerParams(dimension_semantics=sem, vmem_limit_bytes=VMEM_LIMIT)


def _sigmoid(x):
    return 1.0 / (1.0 + jnp.exp(-x))


def _rms(x, g):
    return x * lax.rsqrt(jnp.mean(x * x, axis=-1, keepdims=True) + RMS_EPS) * g


def _ada_kernel(c_ref, w_ref, b_ref, o_ref):
    c = c_ref[...]
    cond = c * _sigmoid(c)
    o_ref[...] = jnp.dot(cond, w_ref[...], preferred_element_type=F32,
                         precision=lax.Precision.HIGHEST) + b_ref[...]


def _ada(c_pad, w, b):
    n = w.shape[1]
    tn = 1024
    return pl.pallas_call(
        _ada_kernel,
        grid=(n // tn,),
        in_specs=[pl.BlockSpec((8, D_MODEL), lambda j: (0, 0)),
                  pl.BlockSpec((D_MODEL, tn), lambda j: (0, j)),
                  pl.BlockSpec((1, tn), lambda j: (0, j))],
        out_specs=pl.BlockSpec((8, tn), lambda j: (0, j)),
        out_shape=jax.ShapeDtypeStruct((8, n), F32),
        compiler_params=_cparams("parallel"),
        name="ada_mod",
    )(c_pad, w, b)


def _ln_mod_kernel(x_ref, g_ref, sc_ref, sh_ref, o_ref):
    y = _rms(x_ref[0], g_ref[...])
    o_ref[0] = (y * (1.0 + sc_ref[0]) + sh_ref[0]).astype(o_ref.dtype)


def _ln_mod(x, g, sc, sh):
    b, s, d = x.shape
    ts = 512
    return pl.pallas_call(
        _ln_mod_kernel,
        grid=(b, s // ts),
        in_specs=[pl.BlockSpec((1, ts, d), lambda i, j: (i, j, 0)),
                  pl.BlockSpec((1, d), lambda i, j: (0, 0)),
                  pl.BlockSpec((1, 1, d), lambda i, j: (i, 0, 0)),
                  pl.BlockSpec((1, 1, d), lambda i, j: (i, 0, 0))],
        out_specs=pl.BlockSpec((1, ts, d), lambda i, j: (i, j, 0)),
        out_shape=jax.ShapeDtypeStruct((b, s, d), BF16),
        compiler_params=_cparams("parallel", "parallel"),
        name="ln_mod",
    )(x, g, sc, sh)


def _mm_kernel(a_ref, wt_ref, o_ref):
    o_ref[...] = lax.dot_general(a_ref[...], wt_ref[...].astype(BF16), NT_DIMS,
                                 preferred_element_type=F32).astype(o_ref.dtype)


def _mm(a, wt, n, tm, tn, out_dtype):
    m, k = a.shape
    return pl.pallas_call(
        _mm_kernel,
        grid=(m // tm, n // tn),
        in_specs=[pl.BlockSpec((tm, k), lambda i, j: (i, 0)),
                  pl.BlockSpec((tn, k), lambda i, j: (j, 0))],
        out_specs=pl.BlockSpec((tm, tn), lambda i, j: (i, j)),
        out_shape=jax.ShapeDtypeStruct((m, n), out_dtype),
        compiler_params=_cparams("parallel", "parallel"),
        name="proj_att",
    )(a, wt)


def _pair_rms(xp, half, gain2):
    sq = xp * xp
    ms_lo = jnp.sum(jnp.where(half, sq, 0.0), axis=-1, keepdims=True) / HEAD_DIM
    ms_hi = jnp.sum(jnp.where(half, 0.0, sq), axis=-1, keepdims=True) / HEAD_DIM
    return xp * lax.rsqrt(jnp.where(half, ms_lo, ms_hi) + RMS_EPS) * gain2


def _swa_kernel(sink_ref, slope_ref, q_ref, kp_ref, kc_ref, vp_ref, vc_ref, qg_ref, kg_ref, o_ref, lhs, e_s):
    n = pl.program_id(1)
    w, w2 = WINDOW, 2 * WINDOW
    q = q_ref[0]
    kw = jnp.concatenate([kp_ref[0], kc_ref[0]], axis=0)
    vw = jnp.concatenate([vp_ref[0], vc_ref[0]], axis=0)
    gq2 = jnp.concatenate([qg_ref[...], qg_ref[...]], axis=1)
    gk2 = jnp.concatenate([kg_ref[...], kg_ref[...]], axis=1)
    half_q = lax.broadcasted_iota(I32, (w, LANES), 1) < HEAD_DIM
    half_k = lax.broadcasted_iota(I32, (w2, LANES), 1) < HEAD_DIM
    row = lax.broadcasted_iota(I32, (w, w2), 0)
    col = lax.broadcasted_iota(I32, (w, w2), 1)
    rel = row + w - col
    valid = (rel >= 0) & (rel < w) & ((col >= w) | (n > 0))
    mbias = jnp.where(valid, 0.0, NEG_INF)
    relf = rel.astype(F32)

    ones_col = jnp.ones((w2, LANES), BF16)
    for g in range(SWA_KV_HEADS):
        if g % 2 == 0:
            grp = slice((g // 2) * LANES, (g // 2 + 1) * LANES)
            kn = _pair_rms(kw[:, grp], half_k, gk2)
            kn_sw = pltpu.roll(kn, HEAD_DIM, axis=1)
            vp = vw[:, grp]
            vp_sw = pltpu.roll(vp, HEAD_DIM, axis=1)
        kdup = jnp.where(half_k, kn, kn_sw) if g % 2 == 0 else jnp.where(half_k, kn_sw, kn)
        vdup = jnp.where(half_k, vp, vp_sw) if g % 2 == 0 else jnp.where(half_k, vp_sw, vp)
        vx = jnp.concatenate([vdup.astype(BF16), ones_col], axis=1)
        for r in range(SWA_GROUP):
            h = g * SWA_GROUP + r
            if r % 2 == 0:
                qn = _pair_rms(q[:, (h // 2) * LANES:(h // 2 + 1) * LANES], half_q, gq2) * (HEAD_DIM ** -0.5)
            keep = half_q if r % 2 == 0 else jnp.logical_not(half_q)
            lhs[r * w:(r + 1) * w, :] = jnp.where(keep, qn, 0.0).astype(BF16)
        s_all = lax.dot_general(lhs[...], kdup.astype(BF16), NT_DIMS, preferred_element_type=F32)
        sink_e = []
        for r in range(SWA_GROUP):
            h = g * SWA_GROUP + r
            rows = slice(r * w, (r + 1) * w)
            s = s_all[rows, :] + (mbias - slope_ref[h] * relf)
            sink = jnp.full((w, LANES), sink_ref[h], F32)
            m = jnp.maximum(jnp.max(s, axis=-1, keepdims=True), sink)
            e_s[rows, :] = jnp.exp(s - jnp.concatenate([m, m], axis=1)).astype(BF16)
            sink_e.append(jnp.exp(sink - m))
        pv = jnp.dot(e_s[...], vx, preferred_element_type=F32)
        for r in range(0, SWA_GROUP, 2):
            h = g * SWA_GROUP + r
            ev, od = pv[r * w:(r + 1) * w, :], pv[(r + 1) * w:(r + 2) * w, :]
            den = jnp.where(half_q, ev[:, LANES:] + sink_e[r], od[:, LANES:] + sink_e[r + 1])
            o = jnp.where(half_q, ev[:, :LANES], od[:, :LANES]) / den
            o_ref[0, :, (h // 2) * LANES:(h // 2 + 1) * LANES] = o.astype(o_ref.dtype)


def _swa(proj3, sinks, slopes, q_gain, k_gain):
    b, s, _ = proj3.shape
    nb = s // WINDOW
    kvw = SWA_KV_HEADS * HEAD_DIM
    qw = SWA_HEADS * HEAD_DIM
    smem = pl.BlockSpec(memory_space=pltpu.SMEM)
    fixed = lambda i, n: (0, 0)
    prev = lambda i, n: (i, jnp.maximum(n - 1, 0), COL_KA // kvw)
    cur = lambda i, n: (i, n, COL_KA // kvw)
    prev_v = lambda i, n: (i, jnp.maximum(n - 1, 0), COL_VA // kvw)
    cur_v = lambda i, n: (i, n, COL_VA // kvw)
    return pl.pallas_call(
        _swa_kernel,
        grid=(b, nb),
        in_specs=[smem, smem,
                  pl.BlockSpec((1, WINDOW, qw), lambda i, n: (i, n, COL_QA // qw)),
                  pl.BlockSpec((1, WINDOW, kvw), prev),
                  pl.BlockSpec((1, WINDOW, kvw), cur),
                  pl.BlockSpec((1, WINDOW, kvw), prev_v),
                  pl.BlockSpec((1, WINDOW, kvw), cur_v),
                  pl.BlockSpec((1, HEAD_DIM), fixed),
                  pl.BlockSpec((1, HEAD_DIM), fixed)],
        out_specs=pl.BlockSpec((1, WINDOW, qw), lambda i, n: (i, n, 0)),
        out_shape=jax.ShapeDtypeStruct((b, s, qw), BF16),
        scratch_shapes=[pltpu.VMEM((SWA_GROUP * WINDOW, LANES), BF16),
                        pltpu.VMEM((SWA_GROUP * WINDOW, 2 * WINDOW), BF16)],
        compiler_params=_cparams("parallel", "parallel"),
        name="swa_attn",
    )(sinks, slopes, proj3, proj3, proj3, proj3, proj3, q_gain, k_gain)


DSA_CHUNK = 256
ALIBI_SPLIT = 3
POS_LO_BITS = 7


def _bf16_head(x):
    return lax.bitcast_convert_type(lax.bitcast_convert_type(x, I32) & jnp.int32(-65536), F32)


def _alibi_columns(slopes):
    parts, rest = [], slopes.astype(F32)
    for _ in range(ALIBI_SPLIT):
        piece = _bf16_head(rest)
        parts.append(piece)
        rest = rest - piece
    cols = [p * float(2 ** POS_LO_BITS) for p in parts] + parts
    pad = jnp.zeros((slopes.shape[0], LANES - 2 * ALIBI_SPLIT), F32)
    return jnp.concatenate([jnp.stack(cols, axis=1), pad], axis=1)


def _dsa_kernel(*refs, topk):
    n_pairs = DSA_HEADS // 2
    qb_refs, qi_refs = refs[:n_pairs], refs[n_pairs:2 * n_pairs]
    (kwi_blk_ref, kvb_ref, kwi_ref, acol_ref, qg_ref, kg_ref, o_ref,
     ki2, kx, vx, lhs_i, lhs_q, wb, key_s, keyt_s, s_s, thr_s, acc_s, m_s, e_s) = refs[2 * n_pairs:]
    i = pl.program_id(1)
    blk, ck = DSA_BLOCK, DSA_CHUNK
    nch = (i + 2) // 2
    seq = ki2.shape[0]
    max_ch = seq // ck
    half = lax.broadcasted_iota(I32, (blk, LANES), 1) < HEAD_DIM

    @pl.when(i == 0)
    def _():
        kvb = kvb_ref[0]
        kib = kwi_ref[0][:, KWI_K:KWI_K + IDX_DIM].astype(BF16)
        ki2[...] = jnp.concatenate([kib, kib], axis=1)
        kn = _rms(kvb[:, KVB_K:KVB_K + HEAD_DIM], kg_ref[...]).astype(BF16)
        pos = lax.broadcasted_iota(I32, (seq, LANES), 0)
        lane = lax.broadcasted_iota(I32, (seq, LANES), 1)
        hi = lax.shift_right_logical(pos, POS_LO_BITS).astype(F32)
        lo = (pos & (2 ** POS_LO_BITS - 1)).astype(F32)
        posc = jnp.where(lane < ALIBI_SPLIT, hi, jnp.where(lane < 2 * ALIBI_SPLIT, lo, 0.0)).astype(BF16)
        kx[...] = jnp.concatenate([kn, kn, posc], axis=1)
        ones = jnp.where(lax.broadcasted_iota(I32, (seq, HEAD_DIM), 1) == 0, 1.0, 0.0).astype(BF16)
        vx[...] = jnp.concatenate([kvb[:, KVB_V:KVB_V + HEAD_DIM].astype(BF16), ones], axis=1)

    g2 = jnp.concatenate([qg_ref[...], qg_ref[...]], axis=1)
    wi = kwi_blk_ref[0][:, KWI_W:KWI_W + IDX_HEADS] * (IDX_HEADS ** -0.5)
    for p in range(n_pairs):
        qn = _pair_rms(qb_refs[p][0], half, g2) * (HEAD_DIM ** -0.5)
        qip = qi_refs[p][0] * (IDX_DIM ** -0.5)
        for k in range(2):
            h = 2 * p + k
            keep = half if k == 0 else jnp.logical_not(half)
            rows = slice(h * blk, (h + 1) * blk)
            lhs_i[rows, :] = jnp.where(keep, qip, 0.0).astype(BF16)
            lhs_q[rows, :LANES] = jnp.where(keep, qn, 0.0).astype(BF16)
            lhs_q[rows, LANES:] = jnp.broadcast_to(acol_ref[h:h + 1, :], (blk, LANES)).astype(BF16)
            wb[h] = jnp.broadcast_to(wi[:, h:h + 1], (blk, LANES))

    row_t = i * blk + lax.broadcasted_iota(I32, (blk, ck), 0)
    col_k = lax.broadcasted_iota(I32, (blk, ck), 1)

    def score_chunk(c, carry):
        ks = pl.ds(pl.multiple_of(c * ck, ck), ck)
        logits = lax.dot_general(lhs_i[...], ki2[ks, :], NT_DIMS, preferred_element_type=F32)
        sc = jnp.zeros((blk, ck), F32)
        for h in range(IDX_HEADS):
            w = wb[h]
            lh = jnp.maximum(logits[h * blk:(h + 1) * blk, :], 0.0)
            sc = sc + lh * jnp.concatenate([w, w], axis=1)
        bits = lax.bitcast_convert_type(sc + 0.0, I32)
        key = jnp.where(bits < 0, bits ^ jnp.int32(0x7FFFFFFF), bits)
        key = jnp.where(c * ck + col_k <= row_t, key, jnp.int32(INT_MIN))
        key_s[c] = key
        keyt_s[c] = key.T
        return carry

    lax.fori_loop(0, nch, score_chunk, 0)

    thr_s[...] = jnp.full(thr_s.shape, INT_MIN, I32)
    for n in range(2, max_ch + 1):
        @pl.when(nch == n)
        def _(n=n):
            def step(it, thr):
                cand = thr + lax.shift_left(jnp.int32(1), 31 - it)
                cnt = jnp.zeros((1, blk), F32)
                for c in range(n):
                    cnt = cnt + jnp.sum(jnp.where(keyt_s[c] >= cand, 1.0, 0.0), axis=0, keepdims=True)
                return jnp.where(cnt >= float(topk), cand, thr)

            thr = lax.fori_loop(0, 32, step, jnp.full((1, blk), INT_MIN, I32))
            thr_s[...] = jnp.broadcast_to(thr, thr_s.shape).T

    thr = thr_s[...]
    thr2 = jnp.concatenate([thr, thr], axis=1)

    def count_gt(c, cnt):
        k = key_s[c]
        return cnt + jnp.where(k[:, :LANES] > thr, 1.0, 0.0) + jnp.where(k[:, LANES:] > thr, 1.0, 0.0)

    n_gt = jnp.sum(lax.fori_loop(0, nch, count_gt, jnp.zeros((blk, LANES), F32)), axis=-1, keepdims=True)
    need = float(topk) - n_gt
    upper = (lax.broadcasted_iota(I32, (ck, ck), 0) < lax.broadcasted_iota(I32, (ck, ck), 1)).astype(BF16)

    m_s[...] = jnp.full(m_s.shape, -jnp.inf, F32)

    def masked_scores(c, n_eq):
        ks = pl.ds(pl.multiple_of(c * ck, ck), ck)
        k = key_s[c]
        eq = (k == thr2) & (c * ck + col_k <= row_t)
        eqf = eq.astype(BF16)
        before = jnp.dot(eqf, upper, preferred_element_type=F32) + n_eq
        sel = (k > thr2) | (eq & (before < need))
        mb = jnp.where(sel, 0.0, NEG_INF)
        s_all = lax.dot_general(lhs_q[...], kx[ks, :], NT_DIMS, preferred_element_type=F32)
        for h in range(DSA_HEADS):
            rows = slice(h * blk, (h + 1) * blk)
            s = s_all[rows, :] + mb
            s_s[c, rows, :] = s
            m_s[rows, :] = jnp.maximum(m_s[rows, :], jnp.maximum(s[:, :LANES], s[:, LANES:]))
        return n_eq + jnp.sum(eqf.astype(F32), axis=-1, keepdims=True)

    lax.fori_loop(0, nch, masked_scores, jnp.zeros((blk, 1), F32))

    m_s[...] = jnp.broadcast_to(jnp.max(m_s[...], axis=-1, keepdims=True), m_s.shape)
    acc_s[...] = jnp.zeros_like(acc_s)

    def attend(c, carry):
        ks = pl.ds(pl.multiple_of(c * ck, ck), ck)
        for h in range(DSA_HEADS):
            rows = slice(h * blk, (h + 1) * blk)
            m = m_s[rows, :]
            e_s[rows, :] = jnp.exp(s_s[c, rows, :] - jnp.concatenate([m, m], axis=1)).astype(BF16)
        acc_s[...] += jnp.dot(e_s[...], vx[ks, :], preferred_element_type=F32)
        return carry

    lax.fori_loop(0, nch, attend, 0)

    for h in range(DSA_HEADS):
        a = acc_s[h * blk:(h + 1) * blk, :]
        o = a[:, :HEAD_DIM] / a[:, HEAD_DIM:HEAD_DIM + 1]
        o_ref[0, :, h * HEAD_DIM:(h + 1) * HEAD_DIM] = o.astype(o_ref.dtype)


def _dsa(proj3, slopes, q_gain, k_gain):
    b, s, _ = proj3.shape
    nb = s // DSA_BLOCK
    qw = DSA_HEADS * HEAD_DIM
    n_pairs = DSA_HEADS // 2
    rows = DSA_HEADS * DSA_BLOCK
    fixed = lambda i, n: (0, 0)

    def group(col):
        return pl.BlockSpec((1, DSA_BLOCK, LANES), lambda i, n: (i, n, col // LANES))

    def group_all(col):
        return pl.BlockSpec((1, s, LANES), lambda i, n: (i, 0, col // LANES))

    return pl.pallas_call(
        functools.partial(_dsa_kernel, topk=min(DSA_TOPK_MAX, s // 4)),
        grid=(b, nb),
        in_specs=([group(COL_QB + p * LANES) for p in range(n_pairs)]
                  + [group(COL_QI + p * LANES) for p in range(n_pairs)]
                  + [group(COL_KWI), group_all(COL_KVB), group_all(COL_KWI),
                     pl.BlockSpec((DSA_HEADS, LANES), fixed),
                     pl.BlockSpec((1, HEAD_DIM), fixed),
                     pl.BlockSpec((1, HEAD_DIM), fixed)]),
        out_specs=pl.BlockSpec((1, DSA_BLOCK, qw), lambda i, n: (i, n, 0)),
        out_shape=jax.ShapeDtypeStruct((b, s, qw), BF16),
        scratch_shapes=[pltpu.VMEM((s, LANES), BF16),
                        pltpu.VMEM((s, 2 * LANES), BF16),
                        pltpu.VMEM((s, LANES), BF16),
                        pltpu.VMEM((rows, LANES), BF16),
                        pltpu.VMEM((rows, 2 * LANES), BF16),
                        pltpu.VMEM((IDX_HEADS, DSA_BLOCK, LANES), F32),
                        pltpu.VMEM((s // DSA_CHUNK, DSA_BLOCK, DSA_CHUNK), I32),
                        pltpu.VMEM((s // DSA_CHUNK, DSA_CHUNK, DSA_BLOCK), I32),
                        pltpu.VMEM((s // DSA_CHUNK, rows, DSA_CHUNK), F32),
                        pltpu.VMEM((DSA_BLOCK, LANES), I32),
                        pltpu.VMEM((rows, LANES), F32),
                        pltpu.VMEM((rows, LANES), F32),
                        pltpu.VMEM((rows, DSA_CHUNK), BF16)],
        compiler_params=_cparams("arbitrary", "arbitrary"),
        name="dsa_attn",
    )(*([proj3] * (2 * n_pairs + 3)), _alibi_columns(slopes), q_gain, k_gain)


def _merge_kernel(h_ref, oa_ref, ob_ref, wt_hbm, wa_ref, wb_ref, o_ref, g32, gbf, sem):
    j, i = pl.program_id(0), pl.program_id(1)
    tn, d = o_ref.shape[1], h_ref.shape[1]

    def gate_copies(jj, slot):
        return [pltpu.make_async_copy(
            wt_hbm.at[pl.ds(pl.multiple_of(COL_GATE + k * d + jj * tn, 8), tn)], g32.at[slot, k], sem.at[slot, k])
            for k in range(2)]

    @pl.when((j == 0) & (i == 0))
    def _():
        for c in gate_copies(0, 0):
            c.start()

    @pl.when(i == 0)
    def _():
        slot = j % 2
        for c in gate_copies(j, slot):
            c.wait()
        gbf[...] = g32[slot].astype(BF16)

        @pl.when(j + 1 < pl.num_programs(0))
        def _():
            for c in gate_copies(j + 1, 1 - slot):
                c.start()

    h = h_ref[...]
    ga = lax.dot_general(h, gbf[0], NT_DIMS, preferred_element_type=F32)
    gb = lax.dot_general(h, gbf[1], NT_DIMS, preferred_element_type=F32)
    a = jnp.dot(oa_ref[...], wa_ref[...], preferred_element_type=F32)
    b = jnp.dot(ob_ref[...], wb_ref[...], preferred_element_type=F32)
    o_ref[...] = (_sigmoid(ga) * a + _sigmoid(gb) * b).astype(o_ref.dtype)


def _merge(h, oa, ob, w_in_t, wa, wb):
    m, d = h.shape
    tm, tn = 1024, 512
    return pl.pallas_call(
        _merge_kernel,
        grid=(d // tn, m // tm),
        in_specs=[pl.BlockSpec((tm, d), lambda j, i: (i, 0)),
                  pl.BlockSpec((tm, oa.shape[1]), lambda j, i: (i, 0)),
                  pl.BlockSpec((tm, ob.shape[1]), lambda j, i: (i, 0)),
                  pl.BlockSpec(memory_space=pl.ANY),
                  pl.BlockSpec((wa.shape[0], tn), lambda j, i: (0, j)),
                  pl.BlockSpec((wb.shape[0], tn), lambda j, i: (0, j))],
        out_specs=pl.BlockSpec((tm, tn), lambda j, i: (i, j)),
        out_shape=jax.ShapeDtypeStruct((m, d), BF16),
        scratch_shapes=[pltpu.VMEM((2, 2, tn, d), w_in_t.dtype),
                        pltpu.VMEM((2, tn, d), BF16),
                        pltpu.SemaphoreType.DMA((2, 2))],
        compiler_params=_cparams("arbitrary", "arbitrary"),
        name="merge_branches",
    )(h, oa, ob, w_in_t, wa, wb)


MOE_TILE = 256
MOE_SLOTS = 2
TOKEN_TILE = 512


def _moe_max_tiles(n_tokens):
    return -(-(n_tokens * MOE_SLOTS + N_EXPERTS * (MOE_TILE - 1)) // MOE_TILE)


def _pack_bf16_pair(x):
    k = x.shape[1] // 2
    bits = lax.bitcast_convert_type(x.astype(BF16).astype(F32), I32)
    return lax.shift_right_logical(bits[:, :k], 16) | (bits[:, k:] & jnp.int32(-65536))


def _unpack_bf16_pair(w):
    lo = lax.bitcast_convert_type(lax.shift_left(w, 16), F32).astype(BF16)
    hi = lax.bitcast_convert_type(w & jnp.int32(-65536), F32).astype(BF16)
    return lo, hi


def _outproj_kernel(mg_ref, wo_ref, x_ref, gt_ref, g2_ref, sc_ref, sh_ref, wr_ref, br_ref,
                    x1_ref, h2p_ref, ids_ref, cw_ref):
    y = jnp.dot(mg_ref[0], wo_ref[...], preferred_element_type=F32)
    x1 = x_ref[0] + gt_ref[0] * y
    x1_ref[0] = x1
    h2 = _rms(x1, g2_ref[...]) * (1.0 + sc_ref[0]) + sh_ref[0]
    h2p_ref[0] = _pack_bf16_pair(h2)

    h_hi = h2.astype(BF16)
    h_lo = (h2 - h_hi.astype(F32)).astype(BF16)
    logits = (jnp.dot(h_hi, wr_ref[0], preferred_element_type=F32)
              + jnp.dot(h_lo, wr_ref[0], preferred_element_type=F32)
              + jnp.dot(h_hi, wr_ref[1], preferred_element_type=F32)) + br_ref[...]
    lane = lax.broadcasted_iota(I32, logits.shape, 1)
    is_g = (lane >= N_EXPERTS) & (lane < N_EXPERTS + N_GROUPS)
    neg = -jnp.inf
    gl = jnp.where(is_g, logits, neg)
    gmax = jnp.max(gl, axis=-1, keepdims=True)
    g_idx = jnp.min(jnp.where(gl == gmax, lane - N_EXPERTS, N_GROUPS), axis=-1, keepdims=True)
    p_g = 1.0 / jnp.sum(jnp.where(is_g, jnp.exp(gl - gmax), 0.0), axis=-1, keepdims=True)
    in_grp = (lane < N_EXPERTS) & ((lane // EXPERTS_PER_GROUP) == g_idx)
    el = jnp.where(in_grp, logits, neg)
    m1 = jnp.max(el, axis=-1, keepdims=True)
    i1 = jnp.min(jnp.where(el == m1, lane, LANES), axis=-1, keepdims=True)
    el2 = jnp.where(lane == i1, neg, el)
    m2 = jnp.max(el2, axis=-1, keepdims=True)
    i2 = jnp.min(jnp.where(el2 == m2, lane, LANES), axis=-1, keepdims=True)
    e2 = jnp.exp(m2 - m1)
    w1 = 1.0 / (1.0 + e2)
    w2 = e2 / (1.0 + e2)
    ids_ref[0] = jnp.where(lane == 0, i1, jnp.where(lane == 1, i2, 0))
    cw_ref[0] = p_g * jnp.where(lane == 0, w1, jnp.where(lane == 1, w2, 0.0))


def _outproj(merged3, w_out, x, gt1, g2, sc2, sh2, w_r, b_r):
    b, s, d = x.shape
    tm = 512
    row = lambda i, j: (i, j, 0)
    per_b = lambda i, j: (i, 0, 0)
    fixed = lambda i, j: (0, 0)
    return pl.pallas_call(
        _outproj_kernel,
        grid=(b, s // tm),
        in_specs=[pl.BlockSpec((1, tm, d), row),
                  pl.BlockSpec((d, d), fixed, pipeline_mode=pl.Buffered(1)),
                  pl.BlockSpec((1, tm, d), row),
                  pl.BlockSpec((1, 1, d), per_b),
                  pl.BlockSpec((1, d), fixed),
                  pl.BlockSpec((1, 1, d), per_b),
                  pl.BlockSpec((1, 1, d), per_b),
                  pl.BlockSpec((2, d, LANES), lambda i, j: (0, 0, 0)),
                  pl.BlockSpec((1, LANES), fixed)],
        out_specs=[pl.BlockSpec((1, tm, d), row),
                   pl.BlockSpec((1, tm, d // 2), row),
                   pl.BlockSpec((1, tm, LANES), row),
                   pl.BlockSpec((1, tm, LANES), row)],
        out_shape=[jax.ShapeDtypeStruct((b, s, d), F32),
                   jax.ShapeDtypeStruct((b, s, d // 2), I32),
                   jax.ShapeDtypeStruct((b, s, LANES), I32),
                   jax.ShapeDtypeStruct((b, s, LANES), F32)],
        compiler_params=_cparams("parallel", "parallel"),
        name="outproj_ln2_router",
    )(merged3, w_out, x, gt1, g2, sc2, sh2, w_r, b_r)


def _plan_kernel(ids_ref, dest_ref, te_ref, rank_ref):
    n = ids_ref.shape[0]
    cb = 256
    lane = lax.broadcasted_iota(I32, (cb, LANES), 1)
    lower = (lax.broadcasted_iota(I32, (cb, cb), 0) > lax.broadcasted_iota(I32, (cb, cb), 1)).astype(BF16)

    def count(bi, carry):
        rows = pl.ds(pl.multiple_of(bi * cb, cb), cb)
        ids = ids_ref[rows, :]
        m1 = lane == ids[:, 0:1]
        m2 = lane == ids[:, 1:2]
        onehot = m1 | m2
        before = jnp.dot(lower, onehot.astype(BF16), preferred_element_type=F32) + carry
        r1 = jnp.sum(jnp.where(m1, before, 0.0), axis=-1, keepdims=True)
        r2 = jnp.sum(jnp.where(m2, before, 0.0), axis=-1, keepdims=True)
        rank_ref[rows, :] = jnp.where(lane == 0, r1, jnp.where(lane == 1, r2, 0.0))
        return carry + jnp.sum(onehot.astype(F32), axis=0, keepdims=True)

    counts = lax.fori_loop(0, n // cb, count, jnp.zeros((1, LANES), F32))
    padded = jnp.floor((counts + (MOE_TILE - 1)) / MOE_TILE) * MOE_TILE
    upper = (lax.broadcasted_iota(I32, (LANES, LANES), 0)
             < lax.broadcasted_iota(I32, (LANES, LANES), 1)).astype(F32)
    offs = jnp.dot(jnp.broadcast_to(padded, (8, LANES)), upper, preferred_element_type=F32,
                   precision=lax.Precision.HIGHEST)[0:1]

    def place(bi, c):
        rows = pl.ds(pl.multiple_of(bi * cb, cb), cb)
        ids = ids_ref[rows, :]
        o1 = jnp.sum(jnp.where(lane == ids[:, 0:1], offs, 0.0), axis=-1, keepdims=True)
        o2 = jnp.sum(jnp.where(lane == ids[:, 1:2], offs, 0.0), axis=-1, keepdims=True)
        slot = rank_ref[rows, :] + jnp.where(lane == 0, o1, jnp.where(lane == 1, o2, 0.0))
        dest_ref[rows, :] = slot.astype(I32)
        return c

    lax.fori_loop(0, n // cb, place, 0)

    sq_row = lax.broadcasted_iota(I32, (LANES, LANES), 0)
    sq_lane = lax.broadcasted_iota(I32, (LANES, LANES), 1)
    tile_start = sq_row.astype(F32) * MOE_TILE
    finished = ((offs + padded) <= tile_start) & (sq_lane < N_EXPERTS)
    tile_expert = jnp.minimum(jnp.sum(finished.astype(F32), axis=-1, keepdims=True), N_EXPERTS - 1.0)
    n_tiles = jnp.sum(padded, axis=-1, keepdims=True) / MOE_TILE
    te_ref[...] = jnp.where(sq_lane == 1, n_tiles, tile_expert).astype(I32)


def _plan(ids):
    n = ids.shape[0]
    return pl.pallas_call(
        _plan_kernel,
        out_shape=[jax.ShapeDtypeStruct((n, LANES), I32),
                   jax.ShapeDtypeStruct((LANES, LANES), I32)],
        scratch_shapes=[pltpu.VMEM((n, LANES), F32)],
        compiler_params=pltpu.CompilerParams(vmem_limit_bytes=VMEM_LIMIT),
        name="moe_plan",
    )(ids)


def _dispatch_kernel(dest_ref, h_ref, hs_zero_ref, hs_ref, sem):
    del hs_zero_ref
    tq = h_ref.shape[0]

    def issue(r, c):
        for k in range(MOE_SLOTS):
            slot = dest_ref[0, 0, MOE_SLOTS * r + k]
            pltpu.make_async_copy(h_ref.at[pl.ds(r, 1)], hs_ref.at[pl.ds(slot, 1)], sem).start(priority=k)
        return c

    lax.fori_loop(0, tq, issue, 0, unroll=8)
    done = hs_ref.at[pl.ds(0, MOE_SLOTS * tq)]
    pltpu.make_async_copy(done, done, sem).wait()


def _dispatch(dest3, h2p, n_rows):
    n, w = h2p.shape
    tq = TOKEN_TILE
    hs_zero = jnp.zeros((n_rows, w), h2p.dtype)
    return pl.pallas_call(
        _dispatch_kernel,
        grid=(n // tq,),
        in_specs=[pl.BlockSpec((1, 1, MOE_SLOTS * tq), lambda i: (i, 0, 0), memory_space=pltpu.SMEM),
                  pl.BlockSpec((tq, w), lambda i: (i, 0)),
                  pl.BlockSpec(memory_space=pl.ANY)],
        out_specs=pl.BlockSpec(memory_space=pl.ANY),
        out_shape=jax.ShapeDtypeStruct((n_rows, w), h2p.dtype),
        scratch_shapes=[pltpu.SemaphoreType.DMA(())],
        input_output_aliases={2: 0},
        compiler_params=_cparams("arbitrary"),
        name="moe_dispatch",
    )(dest3, h2p, hs_zero)


def _expert_kernel(te_ref, nt_ref, hs_ref, wg_hbm, wu_hbm, wd_hbm, ys_ref,
                   wg_buf, wu_buf, wd_buf, sem, slot_ref):
    t = pl.program_id(0)
    nt = nt_ref[0]
    used = t < nt
    last = te_ref.shape[0] - 1

    def weight_copies(e, slot):
        return (pltpu.make_async_copy(wg_hbm.at[e], wg_buf.at[slot], sem.at[0, slot]),
                pltpu.make_async_copy(wu_hbm.at[e], wu_buf.at[slot], sem.at[1, slot]),
                pltpu.make_async_copy(wd_hbm.at[e], wd_buf.at[slot], sem.at[2, slot]))

    @pl.when(t == 0)
    def _():
        slot_ref[0] = 0
        for c in weight_copies(te_ref[0], 0):
            c.start()

    @pl.when(jnp.logical_not(used))
    def _():
        ys_ref[...] = jnp.zeros_like(ys_ref)

    @pl.when(used)
    def _():
        e = te_ref[t]
        first = (t == 0) | (te_ref[jnp.maximum(t - 1, 0)] != e)

        @pl.when(first & (t > 0))
        def _():
            slot_ref[0] = 1 - slot_ref[0]

        slot = slot_ref[0]

        @pl.when(first)
        def _():
            for c in weight_copies(e, slot):
                c.wait()
            nxt = lax.while_loop(lambda s: (s < nt) & (te_ref[jnp.minimum(s, last)] == e),
                                 lambda s: s + 1, t + 1)

            @pl.when(nxt < nt)
            def _():
                for c in weight_copies(te_ref[jnp.minimum(nxt, last)], 1 - slot):
                    c.start()

        lo, hi = _unpack_bf16_pair(hs_ref[...])
        k = lo.shape[1]

        def up(w_buf):
            return (jnp.dot(lo, w_buf[slot, :k, :].astype(BF16), preferred_element_type=F32)
                    + jnp.dot(hi, w_buf[slot, k:, :].astype(BF16), preferred_element_type=F32))

        a = up(wg_buf)
        u = up(wu_buf)
        act = (a * _sigmoid(a) * u).astype(BF16)
        ys_ref[...] = jnp.dot(act, wd_buf[slot].astype(BF16), preferred_element_type=F32)


def _experts(tile_expert, n_tiles, hs, wg, wu, wd):
    n_rows, hw = hs.shape
    d, f = wg.shape[1], wg.shape[2]
    row = lambda t, te, nt: (jnp.minimum(t, nt[0] - 1), 0)
    hbm = pl.BlockSpec(memory_space=pl.ANY)
    return pl.pallas_call(
        _expert_kernel,
        grid_spec=pltpu.PrefetchScalarGridSpec(
            num_scalar_prefetch=2,
            grid=(n_rows // MOE_TILE,),
            in_specs=[pl.BlockSpec((MOE_TILE, hw), row), hbm, hbm, hbm],
            out_specs=pl.BlockSpec((MOE_TILE, d), lambda t, te, nt: (t, 0)),
            scratch_shapes=[pltpu.VMEM((2, d, f), wg.dtype),
                            pltpu.VMEM((2, d, f), wu.dtype),
                            pltpu.VMEM((2, f, d), wd.dtype),
                            pltpu.SemaphoreType.DMA((3, 2)),
                            pltpu.SMEM((1,), I32)]),
        out_shape=jax.ShapeDtypeStruct((n_rows, d), F32),
        compiler_params=_cparams("arbitrary"),
        name="moe_experts",
    )(tile_expert, n_tiles, hs, wg, wu, wd)


def _combine_kernel(dest_ref, ys_ref, x1_ref, cw_ref, gt_ref, o_ref, buf, sem):
    tq = x1_ref.shape[1]

    def issue(r, c):
        for k in range(MOE_SLOTS):
            slot = dest_ref[0, 0, MOE_SLOTS * r + k]
            pltpu.make_async_copy(ys_ref.at[pl.ds(slot, 1)], buf.at[k, pl.ds(r, 1)], sem).start(priority=k)
        return c

    lax.fori_loop(0, tq, issue, 0, unroll=8)
    pltpu.make_async_copy(buf, buf, sem).wait()
    cw = cw_ref[0]
    y = cw[:, 0:1] * buf[0] + cw[:, 1:2] * buf[1]
    o_ref[0] = x1_ref[0] + gt_ref[0] * y


def _combine(dest3, ys, x1, cw, gt2):
    b, s, d = x1.shape
    tq = TOKEN_TILE
    spb = s // tq
    row = lambda i, j: (i, j, 0)
    return pl.pallas_call(
        _combine_kernel,
        grid=(b, spb),
        in_specs=[pl.BlockSpec((1, 1, MOE_SLOTS * tq), lambda i, j: (i * spb + j, 0, 0),
                               memory_space=pltpu.SMEM),
                  pl.BlockSpec(memory_space=pl.ANY),
                  pl.BlockSpec((1, tq, d), row),
                  pl.BlockSpec((1, tq, LANES), row),
                  pl.BlockSpec((1, 1, d), lambda i, j: (i, 0, 0))],
        out_specs=pl.BlockSpec((1, tq, d), row),
        out_shape=jax.ShapeDtypeStruct((b, s, d), F32),
        scratch_shapes=[pltpu.VMEM((MOE_SLOTS, tq, d), F32), pltpu.SemaphoreType.DMA(())],
        compiler_params=_cparams("arbitrary", "arbitrary"),
        name="moe_combine",
    )(dest3, ys, x1, cw, gt2)


def _moe(h2p, ids, cw, wg, wu, wd, x1, gt2):
    b, s, d = x1.shape
    n = b * s
    n_rows = _moe_max_tiles(n) * MOE_TILE
    dest, te = _plan(ids.reshape(n, LANES))
    dest3 = dest[:, :MOE_SLOTS].reshape(n // TOKEN_TILE, 1, MOE_SLOTS * TOKEN_TILE)
    hs = _dispatch(dest3, h2p.reshape(n, d // 2), n_rows)
    ys = _experts(te[:n_rows // MOE_TILE, 0], te[0, 1:2], hs, wg, wu, wd)
    return _combine(dest3, ys, x1, cw, gt2)


def _alibi_slopes():
    n = SWA_HEADS + DSA_HEADS
    i = jnp.arange(1, n + 1, dtype=F32)
    return jnp.exp2(-8.0 * i / n)


def _layer(x, mod, ln1_g, w_in, swa_q_norm, swa_k_norm, swa_sinks, dsa_q_norm, dsa_k_norm,
           w_branch_a, w_branch_b, w_out, ln2_g, w_group, b_group, w_expert, b_expert,
           w_gate_up, w_up, w_down, slopes):
    b, s, d = x.shape
    sh1, sc1, gt1, sh2, sc2, gt2 = [m[:, None, :] for m in jnp.split(mod, 6, axis=-1)]

    w_in_t = jnp.swapaxes(w_in, 0, 1)

    h1 = _ln_mod(x, ln1_g[None, :], sc1, sh1)
    h1f = h1.reshape(b * s, d)
    proj = _mm(h1f, w_in_t, ATT_WIDTH, 1024, 768, F32).reshape(b, s, ATT_WIDTH)
    oa = _swa(proj, swa_sinks, slopes[:SWA_HEADS], swa_q_norm[None, :], swa_k_norm[None, :])
    ob = _dsa(proj, slopes[SWA_HEADS:], dsa_q_norm[None, :], dsa_k_norm[None, :])
    merged = _merge(h1f, oa.reshape(b * s, -1), ob.reshape(b * s, -1), w_in_t,
                    w_branch_a.astype(BF16), w_branch_b.astype(BF16))

    w_r = jnp.concatenate([w_expert, w_group,
                           jnp.zeros((d, LANES - N_EXPERTS - N_GROUPS), F32)], axis=1)
    b_r = jnp.concatenate([b_expert, b_group,
                           jnp.zeros((LANES - N_EXPERTS - N_GROUPS,), F32)])[None, :]
    w_r_hi = _bf16_head(w_r)
    w_r_split = jnp.stack([w_r_hi.astype(BF16), (w_r - w_r_hi).astype(BF16)])
    x1, h2p, ids, cw = _outproj(merged.reshape(b, s, d), w_out.astype(BF16), x, gt1,
                                ln2_g[None, :], sc2, sh2, w_r_split, b_r)
    f = w_gate_up.shape[-1]
    return _moe(h2p, ids, cw, w_gate_up.reshape(N_EXPERTS, d, f), w_up.reshape(N_EXPERTS, d, f),
                w_down.reshape(N_EXPERTS, f, d), x1, gt2)


def kernel(x, c, w_ada, b_ada, ln1_g, w_in, swa_q_norm, swa_k_norm, swa_sinks, dsa_q_norm, dsa_k_norm,
           w_branch_a, w_branch_b, w_out, ln2_g, w_group, b_group, w_expert, b_expert,
           w_gate_up, w_up, w_down):
    slopes = _alibi_slopes()
    bsz = c.shape[0]
    c_pad = jnp.concatenate([c, jnp.zeros((8 - bsz, c.shape[1]), c.dtype)], axis=0)
    for l in range(w_ada.shape[0]):
        mod = _ada(c_pad, w_ada[l], b_ada[l][None, :])[:bsz]
        x = _layer(x, mod, ln1_g[l], w_in[l], swa_q_norm[l], swa_k_norm[l], swa_sinks[l],
                   dsa_q_norm[l], dsa_k_norm[l], w_branch_a[l], w_branch_b[l], w_out[l], ln2_g[l],
                   w_group[l], b_group[l], w_expert[l], b_expert[l],
                   w_gate_up[l], w_up[l], w_down[l], slopes)
    return x
```

```python
import functools

import jax
import jax.numpy as jnp
from jax import lax
from jax.experimental import pallas as pl
from jax.experimental.pallas import tpu as pltpu

F32 = jnp.float32
BF16 = jnp.bfloat16
I32 = jnp.int32

D_MODEL = 2048
HEAD_DIM = 64
SWA_HEADS = 16
SWA_KV_HEADS = 4
SWA_GROUP = SWA_HEADS // SWA_KV_HEADS
WINDOW = 128
DSA_HEADS = 16
IDX_HEADS = 16
IDX_DIM = 64
DSA_TOPK_MAX = 256
DSA_BLOCK = 128
N_GROUPS = 4
EXPERTS_PER_GROUP = 8
N_EXPERTS = N_GROUPS * EXPERTS_PER_GROUP
D_EXPERT = 512
RMS_EPS = 1e-6
NEG_INF = -1e30
INT_MIN = -(2 ** 31)

LANES = 128
VMEM_LIMIT = 56 * 1024 * 1024

COL_QA, COL_KA, COL_VA, COL_QB = 0, 1024, 1280, 1536
COL_KVB = 2560
COL_QI = 2688
COL_KWI = 3712
COL_GATE = 3792
ATT_WIDTH = 3840
KVB_K, KVB_V = 0, HEAD_DIM
KWI_K, KWI_W = 0, IDX_DIM

NT_DIMS = (((1,), (1,)), ((), ()))


def _cparams(*sem):
    return pltpu.CompilerParams(dimension_semantics=sem, vmem_limit_bytes=VMEM_LIMIT)


def _sigmoid(x):
    return 1.0 / (1.0 + jnp.exp(-x))


def _rms(x, g):
    return x * lax.rsqrt(jnp.mean(x * x, axis=-1, keepdims=True) + RMS_EPS) * g


def _ada_kernel(c_ref, w_ref, b_ref, o_ref):
    c = c_ref[...]
    cond = c * _sigmoid(c)
    c_hi = cond.astype(BF16)
    c_lo = (cond - c_hi.astype(F32)).astype(BF16)
    w = w_ref[...]
    w_hi = w.astype(BF16)
    w_lo = (w - w_hi.astype(F32)).astype(BF16)
    o_ref[...] = (jnp.dot(c_hi, w_hi, preferred_element_type=F32)
                  + jnp.dot(c_lo, w_hi, preferred_element_type=F32)
                  + jnp.dot(c_hi, w_lo, preferred_element_type=F32)) + b_ref[...]


ADA_ROWS = 16


def _ada(c_pad, w, b):
    n = w.shape[1]
    tn = 1024
    return pl.pallas_call(
        _ada_kernel,
        grid=(n // tn,),
        in_specs=[pl.BlockSpec((ADA_ROWS, D_MODEL), lambda j: (0, 0)),
                  pl.BlockSpec((D_MODEL, tn), lambda j: (0, j)),
                  pl.BlockSpec((1, tn), lambda j: (0, j))],
        out_specs=pl.BlockSpec((ADA_ROWS, tn), lambda j: (0, j)),
        out_shape=jax.ShapeDtypeStruct((ADA_ROWS, n), F32),
        compiler_params=_cparams("parallel"),
        name="ada_mod",
    )(c_pad, w, b)


def _ln_mod_kernel(x_ref, g_ref, sc_ref, sh_ref, o_ref):
    y = _rms(x_ref[0], g_ref[...])
    o_ref[0] = (y * (1.0 + sc_ref[0]) + sh_ref[0]).astype(o_ref.dtype)


def _ln_mod(x, g, sc, sh):
    b, s, d = x.shape
    ts = 512
    return pl.pallas_call(
        _ln_mod_kernel,
        grid=(b, s // ts),
        in_specs=[pl.BlockSpec((1, ts, d), lambda i, j: (i, j, 0)),
                  pl.BlockSpec((1, d), lambda i, j: (0, 0)),
                  pl.BlockSpec((1, 1, d), lambda i, j: (i, 0, 0)),
                  pl.BlockSpec((1, 1, d), lambda i, j: (i, 0, 0))],
        out_specs=pl.BlockSpec((1, ts, d), lambda i, j: (i, j, 0)),
        out_shape=jax.ShapeDtypeStruct((b, s, d), BF16),
        compiler_params=_cparams("parallel", "parallel"),
        name="ln_mod",
    )(x, g, sc, sh)


def _proj_kernel(a_ref, wt_ref, mult_ref, normed_ref, o_ref):
    y = lax.dot_general(a_ref[...], wt_ref[...].astype(BF16), NT_DIMS, preferred_element_type=F32)
    tm, tn = y.shape
    half = lax.broadcasted_iota(I32, (tm, LANES), 1) < HEAD_DIM
    for g in range(tn // LANES):
        cols = slice(g * LANES, (g + 1) * LANES)
        yg = y[:, cols]
        sq = yg * yg
        ms_lo = jnp.sum(jnp.where(half, sq, 0.0), axis=-1, keepdims=True) / HEAD_DIM
        ms_hi = jnp.sum(jnp.where(half, 0.0, sq), axis=-1, keepdims=True) / HEAD_DIM
        r = lax.rsqrt(jnp.where(half, ms_lo, ms_hi) + RMS_EPS)
        o_ref[:, cols] = yg * jnp.where(normed_ref[:, cols] > 0.0, r, 1.0) * mult_ref[:, cols]


def _proj(a, wt, mult, normed, tm, tn):
    m, k = a.shape
    n = mult.shape[1]
    vec = pl.BlockSpec((1, tn), lambda i, j: (0, j))
    return pl.pallas_call(
        _proj_kernel,
        grid=(m // tm, n // tn),
        in_specs=[pl.BlockSpec((tm, k), lambda i, j: (i, 0)),
                  pl.BlockSpec((tn, k), lambda i, j: (j, 0)),
                  vec, vec],
        out_specs=pl.BlockSpec((tm, tn), lambda i, j: (i, j)),
        out_shape=jax.ShapeDtypeStruct((m, n), F32),
        compiler_params=_cparams("parallel", "parallel"),
        name="proj_att",
    )(a, wt, mult, normed)


def _swa_kernel(sink_ref, slope_ref, q_ref, kp_ref, kc_ref, vp_ref, vc_ref, o_ref, lhs, e_s):
    n = pl.program_id(1)
    w, w2 = WINDOW, 2 * WINDOW
    q = q_ref[0]
    kw = jnp.concatenate([kp_ref[0], kc_ref[0]], axis=0)
    vw = jnp.concatenate([vp_ref[0], vc_ref[0]], axis=0)
    half_q = lax.broadcasted_iota(I32, (w, LANES), 1) < HEAD_DIM
    half_k = lax.broadcasted_iota(I32, (w2, LANES), 1) < HEAD_DIM
    row = lax.broadcasted_iota(I32, (w, w2), 0)
    col = lax.broadcasted_iota(I32, (w, w2), 1)
    rel = row + w - col
    valid = (rel >= 0) & (rel < w) & ((col >= w) | (n > 0))
    mbias = jnp.where(valid, 0.0, NEG_INF)
    relf = rel.astype(F32)

    ones_col = jnp.ones((w2, LANES), BF16)
    for g in range(SWA_KV_HEADS):
        if g % 2 == 0:
            grp = slice((g // 2) * LANES, (g // 2 + 1) * LANES)
            kn = kw[:, grp]
            kn_sw = pltpu.roll(kn, HEAD_DIM, axis=1)
            vp = vw[:, grp]
            vp_sw = pltpu.roll(vp, HEAD_DIM, axis=1)
        kdup = jnp.where(half_k, kn, kn_sw) if g % 2 == 0 else jnp.where(half_k, kn_sw, kn)
        vdup = jnp.where(half_k, vp, vp_sw) if g % 2 == 0 else jnp.where(half_k, vp_sw, vp)
        vx = jnp.concatenate([vdup.astype(BF16), ones_col], axis=1)
        for r in range(SWA_GROUP):
            h = g * SWA_GROUP + r
            if r % 2 == 0:
                qn = q[:, (h // 2) * LANES:(h // 2 + 1) * LANES]
            keep = half_q if r % 2 == 0 else jnp.logical_not(half_q)
            lhs[r * w:(r + 1) * w, :] = jnp.where(keep, qn, 0.0).astype(BF16)
        s_all = lax.dot_general(lhs[...], kdup.astype(BF16), NT_DIMS, preferred_element_type=F32)
        sink_e = []
        for r in range(SWA_GROUP):
            h = g * SWA_GROUP + r
            rows = slice(r * w, (r + 1) * w)
            s = s_all[rows, :] + (mbias - slope_ref[h] * relf)
            sink = jnp.full((w, LANES), sink_ref[h], F32)
            m = jnp.maximum(jnp.max(s, axis=-1, keepdims=True), sink)
            e_s[rows, :] = jnp.exp(s - jnp.concatenate([m, m], axis=1)).astype(BF16)
            sink_e.append(jnp.exp(sink - m))
        pv = jnp.dot(e_s[...], vx, preferred_element_type=F32)
        for r in range(0, SWA_GROUP, 2):
            h = g * SWA_GROUP + r
            ev, od = pv[r * w:(r + 1) * w, :], pv[(r + 1) * w:(r + 2) * w, :]
            den = jnp.where(half_q, ev[:, LANES:] + sink_e[r], od[:, LANES:] + sink_e[r + 1])
            o = jnp.where(half_q, ev[:, :LANES], od[:, :LANES]) / den
            o_ref[0, :, (h // 2) * LANES:(h // 2 + 1) * LANES] = o.astype(o_ref.dtype)


def _swa(proj3, sinks, slopes):
    b, s, _ = proj3.shape
    nb = s // WINDOW
    kvw = SWA_KV_HEADS * HEAD_DIM
    qw = SWA_HEADS * HEAD_DIM
    smem = pl.BlockSpec(memory_space=pltpu.SMEM)
    prev =lambda i, n: (i, jnp.maximum(n - 1, 0), COL_KA // kvw)
    cur = lambda i, n: (i, n, COL_KA // kvw)
    prev_v = lambda i, n: (i, jnp.maximum(n - 1, 0), COL_VA // kvw)
    cur_v = lambda i, n: (i, n, COL_VA // kvw)
    return pl.pallas_call(
        _swa_kernel,
        grid=(b, nb),
        in_specs=[smem, smem,
                  pl.BlockSpec((1, WINDOW, qw), lambda i, n: (i, n, COL_QA // qw)),
                  pl.BlockSpec((1, WINDOW, kvw), prev),
                  pl.BlockSpec((1, WINDOW, kvw), cur),
                  pl.BlockSpec((1, WINDOW, kvw), prev_v),
                  pl.BlockSpec((1, WINDOW, kvw), cur_v)],
        out_specs=pl.BlockSpec((1, WINDOW, qw), lambda i, n: (i, n, 0)),
        out_shape=jax.ShapeDtypeStruct((b, s, qw), BF16),
        scratch_shapes=[pltpu.VMEM((SWA_GROUP * WINDOW, LANES), BF16),
                        pltpu.VMEM((SWA_GROUP * WINDOW, 2 * WINDOW), BF16)],
        compiler_params=_cparams("parallel", "parallel"),
        name="swa_attn",
    )(sinks, slopes, proj3, proj3, proj3, proj3, proj3)


DSA_CHUNK = 256
ALIBI_SPLIT = 3
POS_LO_BITS = 7


def _bf16_head(x):
    return lax.bitcast_convert_type(lax.bitcast_convert_type(x, I32) & jnp.int32(-65536), F32)


def _alibi_columns(slopes):
    parts, rest = [], slopes.astype(F32)
    for _ in range(ALIBI_SPLIT):
        piece = _bf16_head(rest)
        parts.append(piece)
        rest = rest - piece
    cols = [p * float(2 ** POS_LO_BITS) for p in parts] + parts
    pad = jnp.zeros((slopes.shape[0], LANES - 2 * ALIBI_SPLIT), F32)
    return jnp.concatenate([jnp.stack(cols, axis=1), pad], axis=1)


def _dsa_kernel(*refs, topk):
    n_pairs = DSA_HEADS // 2
    qb_refs, qi_refs = refs[:n_pairs], refs[n_pairs:2 * n_pairs]
    (kwi_blk_ref, kvb_ref, kwi_ref, acol_ref, o_ref,
     ki2, kx, vx, lhs_i, lhs_q, wb, key_s, keyt_s, s_s, thr_s, acc_s, m_s, e_s) = refs[2 * n_pairs:]
    i = pl.program_id(1)
    blk, ck = DSA_BLOCK, DSA_CHUNK
    nch = (i + 2) // 2
    seq = ki2.shape[0]
    max_ch = seq // ck
    half = lax.broadcasted_iota(I32, (blk, LANES), 1) < HEAD_DIM

    @pl.when(i == 0)
    def _():
        kvb = kvb_ref[0]
        kib = kwi_ref[0][:, KWI_K:KWI_K + IDX_DIM].astype(BF16)
        ki2[...] = jnp.concatenate([kib, kib], axis=1)
        kn = kvb[:, KVB_K:KVB_K + HEAD_DIM].astype(BF16)
        pos = lax.broadcasted_iota(I32, (seq, LANES), 0)
        lane = lax.broadcasted_iota(I32, (seq, LANES), 1)
        hi = lax.shift_right_logical(pos, POS_LO_BITS).astype(F32)
        lo = (pos & (2 ** POS_LO_BITS - 1)).astype(F32)
        posc = jnp.where(lane < ALIBI_SPLIT, hi, jnp.where(lane < 2 * ALIBI_SPLIT, lo, 0.0)).astype(BF16)
        kx[...] = jnp.concatenate([kn, kn, posc], axis=1)
        ones = jnp.where(lax.broadcasted_iota(I32, (seq, HEAD_DIM), 1) == 0, 1.0, 0.0).astype(BF16)
        vx[...] = jnp.concatenate([kvb[:, KVB_V:KVB_V + HEAD_DIM].astype(BF16), ones], axis=1)

    wi = kwi_blk_ref[0][:, KWI_W:KWI_W + IDX_HEADS]
    for p in range(n_pairs):
        qn = qb_refs[p][0]
        qip = qi_refs[p][0]
        for k in range(2):
            h = 2 * p + k
            keep = half if k == 0 else jnp.logical_not(half)
            rows = slice(h * blk, (h + 1) * blk)
            lhs_i[rows, :] = jnp.where(keep, qip, 0.0).astype(BF16)
            lhs_q[rows, :LANES] = jnp.where(keep, qn, 0.0).astype(BF16)
            lhs_q[rows, LANES:] = jnp.broadcast_to(acol_ref[h:h + 1, :], (blk, LANES)).astype(BF16)
            wb[h] = jnp.broadcast_to(wi[:, h:h + 1], (blk, LANES))

    row_t = i * blk + lax.broadcasted_iota(I32, (blk, ck), 0)
    col_k = lax.broadcasted_iota(I32, (blk, ck), 1)

    def score_chunk(c, carry):
        ks = pl.ds(pl.multiple_of(c * ck, ck), ck)
        logits = lax.dot_general(lhs_i[...], ki2[ks, :], NT_DIMS, preferred_element_type=F32)
        sc = jnp.zeros((blk, ck), F32)
        for h in range(IDX_HEADS):
            w = wb[h]
            lh = jnp.maximum(logits[h * blk:(h + 1) * blk, :], 0.0)
            sc = sc + lh * jnp.concatenate([w, w], axis=1)
        bits = lax.bitcast_convert_type(sc + 0.0, I32)
        key = jnp.where(bits < 0, bits ^ jnp.int32(0x7FFFFFFF), bits)
        key = jnp.where(c * ck + col_k <= row_t, key, jnp.int32(INT_MIN))
        key_s[c] = key
        keyt_s[c] = key.T
        return carry

    lax.fori_loop(0, nch, score_chunk, 0)

    thr_s[...] = jnp.full(thr_s.shape, INT_MIN, I32)
    for n in range(2, max_ch + 1):
        @pl.when(nch == n)
        def _(n=n):
            def step(it, thr):
                cand = thr + lax.shift_left(jnp.int32(1), 31 - it)
                cnt = jnp.zeros((1, blk), F32)
                for c in range(n):
                    cnt = cnt + jnp.sum(jnp.where(keyt_s[c] >= cand, 1.0, 0.0), axis=0, keepdims=True)
                return jnp.where(cnt >= float(topk), cand, thr)

            thr = lax.fori_loop(0, 32, step, jnp.full((1, blk), INT_MIN, I32))
            thr_s[...] = jnp.broadcast_to(thr, thr_s.shape).T

    thr = thr_s[...]
    thr2 = jnp.concatenate([thr, thr], axis=1)

    def count_gt(c, cnt):
        k = key_s[c]
        return cnt + jnp.where(k[:, :LANES] > thr, 1.0, 0.0) + jnp.where(k[:, LANES:] > thr, 1.0, 0.0)

    n_gt = jnp.sum(lax.fori_loop(0, nch, count_gt, jnp.zeros((blk, LANES), F32)), axis=-1, keepdims=True)
    need = float(topk) - n_gt
    upper = (lax.broadcasted_iota(I32, (ck, ck), 0) < lax.broadcasted_iota(I32, (ck, ck), 1)).astype(BF16)

    m_s[...] = jnp.full(m_s.shape, -jnp.inf, F32)

    def masked_scores(c, n_eq):
        ks = pl.ds(pl.multiple_of(c * ck, ck), ck)
        k = key_s[c]
        eq = (k == thr2) & (c * ck + col_k <= row_t)
        eqf = eq.astype(BF16)
        before = jnp.dot(eqf, upper, preferred_element_type=F32) + n_eq
        sel = (k > thr2) | (eq & (before < need))
        mb = jnp.where(sel, 0.0, NEG_INF)
        s_all = lax.dot_general(lhs_q[...], kx[ks, :], NT_DIMS, preferred_element_type=F32)
        for h in range(DSA_HEADS):
            rows = slice(h * blk, (h + 1) * blk)
            s = s_all[rows, :] + mb
            s_s[c, rows, :] = s
            m_s[rows, :] = jnp.maximum(m_s[rows, :], jnp.maximum(s[:, :LANES], s[:, LANES:]))
        return n_eq + jnp.sum(eqf.astype(F32), axis=-1, keepdims=True)

    lax.fori_loop(0, nch, masked_scores, jnp.zeros((blk, 1), F32))

    m_s[...] = jnp.broadcast_to(jnp.max(m_s[...], axis=-1, keepdims=True), m_s.shape)
    acc_s[...] = jnp.zeros_like(acc_s)

    def attend(c, carry):
        ks = pl.ds(pl.multiple_of(c * ck, ck), ck)
        for h in range(DSA_HEADS):
            rows = slice(h * blk, (h + 1) * blk)
            m = m_s[rows, :]
            e_s[rows, :] = jnp.exp(s_s[c, rows, :] - jnp.concatenate([m, m], axis=1)).astype(BF16)
        acc_s[...] += jnp.dot(e_s[...], vx[ks, :], preferred_element_type=F32)
        return carry

    lax.fori_loop(0, nch, attend, 0)

    for h in range(DSA_HEADS):
        a = acc_s[h * blk:(h + 1) * blk, :]
        o = a[:, :HEAD_DIM] / a[:, HEAD_DIM:HEAD_DIM + 1]
        o_ref[0, :, h * HEAD_DIM:(h + 1) * HEAD_DIM] = o.astype(o_ref.dtype)


def _dsa(proj3, slopes):
    b, s, _ = proj3.shape
    nb = s // DSA_BLOCK
    qw = DSA_HEADS * HEAD_DIM
    n_pairs = DSA_HEADS // 2
    rows = DSA_HEADS * DSA_BLOCK
    fixed = lambda i, n: (0, 0)

    def group(col):
        return pl.BlockSpec((1, DSA_BLOCK, LANES), lambda i, n: (i, n, col // LANES))

    def group_all(col):
        return pl.BlockSpec((1, s, LANES), lambda i, n: (i, 0, col // LANES))

    return pl.pallas_call(
        functools.partial(_dsa_kernel, topk=min(DSA_TOPK_MAX, s // 4)),
        grid=(b, nb),
        in_specs=([group(COL_QB + p * LANES) for p in range(n_pairs)]
                  + [group(COL_QI + p * LANES) for p in range(n_pairs)]
                  + [group(COL_KWI), group_all(COL_KVB), group_all(COL_KWI),
                     pl.BlockSpec((DSA_HEADS, LANES), fixed)]),
        out_specs=pl.BlockSpec((1, DSA_BLOCK, qw), lambda i, n: (i, n, 0)),
        out_shape=jax.ShapeDtypeStruct((b, s, qw), BF16),
        scratch_shapes=[pltpu.VMEM((s, LANES), BF16),
                        pltpu.VMEM((s, 2 * LANES), BF16),
                        pltpu.VMEM((s, LANES), BF16),
                        pltpu.VMEM((rows, LANES), BF16),
                        pltpu.VMEM((rows, 2 * LANES), BF16),
                        pltpu.VMEM((IDX_HEADS, DSA_BLOCK, LANES), F32),
                        pltpu.VMEM((s // DSA_CHUNK, DSA_BLOCK, DSA_CHUNK), I32),
                        pltpu.VMEM((s // DSA_CHUNK, DSA_CHUNK, DSA_BLOCK), I32),
                        pltpu.VMEM((s // DSA_CHUNK, rows, DSA_CHUNK), F32),
                        pltpu.VMEM((DSA_BLOCK, LANES), I32),
                        pltpu.VMEM((rows, LANES), F32),
                        pltpu.VMEM((rows, LANES), F32),
                        pltpu.VMEM((rows, DSA_CHUNK), BF16)],
        compiler_params=_cparams("arbitrary", "arbitrary"),
        name="dsa_attn",
    )(*([proj3] * (2 * n_pairs + 3)), _alibi_columns(slopes))


def _merge_kernel(h_ref, oa_ref, ob_ref, wt_hbm, wa_ref, wb_ref, o_ref, g32, gbf, sem):
    j, i = pl.program_id(0), pl.program_id(1)
    tn, d = o_ref.shape[1], h_ref.shape[1]

    def gate_copies(jj, slot):
        return [pltpu.make_async_copy(
            wt_hbm.at[pl.ds(pl.multiple_of(COL_GATE + k * d + jj * tn, 8), tn)], g32.at[slot, k], sem.at[slot, k])
            for k in range(2)]

    @pl.when((j == 0) & (i == 0))
    def _():
        for c in gate_copies(0, 0):
            c.start()

    @pl.when(i == 0)
    def _():
        slot = j % 2
        for c in gate_copies(j, slot):
            c.wait()
        gbf[...] = g32[slot].astype(BF16)

        @pl.when(j + 1 < pl.num_programs(0))
        def _():
            for c in gate_copies(j + 1, 1 - slot):
                c.start()

    h = h_ref[...]
    ga = lax.dot_general(h, gbf[0], NT_DIMS, preferred_element_type=F32)
    gb = lax.dot_general(h, gbf[1], NT_DIMS, preferred_element_type=F32)
    a = jnp.dot(oa_ref[...], wa_ref[...], preferred_element_type=F32)
    b = jnp.dot(ob_ref[...], wb_ref[...], preferred_element_type=F32)
    o_ref[...] = (_sigmoid(ga) * a + _sigmoid(gb) * b).astype(o_ref.dtype)


def _merge(h, oa, ob, w_in_t, wa, wb):
    m, d = h.shape
    tm, tn = 1024, 512
    return pl.pallas_call(
        _merge_kernel,
        grid=(d // tn, m // tm),
        in_specs=[pl.BlockSpec((tm, d), lambda j, i: (i, 0)),
                  pl.BlockSpec((tm, oa.shape[1]), lambda j, i: (i, 0)),
                  pl.BlockSpec((tm, ob.shape[1]), lambda j, i: (i, 0)),
                  pl.BlockSpec(memory_space=pl.ANY),
                  pl.BlockSpec((wa.shape[0], tn), lambda j, i: (0, j)),
                  pl.BlockSpec((wb.shape[0], tn), lambda j, i: (0, j))],
        out_specs=pl.BlockSpec((tm, tn), lambda j, i: (i, j)),
        out_shape=jax.ShapeDtypeStruct((m, d), BF16),
        scratch_shapes=[pltpu.VMEM((2, 2, tn, d), w_in_t.dtype),
                        pltpu.VMEM((2, tn, d), BF16),
                        pltpu.SemaphoreType.DMA((2, 2))],
        compiler_params=_cparams("arbitrary", "arbitrary"),
        name="merge_branches",
    )(h, oa, ob, w_in_t, wa, wb)


MOE_TILE = 256
MOE_SLOTS = 2
TOKEN_TILE = 512


def _moe_max_tiles(n_tokens):
    return -(-(n_tokens * MOE_SLOTS + N_EXPERTS * (MOE_TILE - 1)) // MOE_TILE)


def _pack_bf16_pair(x):
    k = x.shape[1] // 2
    bits = lax.bitcast_convert_type(x.astype(BF16).astype(F32), I32)
    return lax.shift_right_logical(bits[:, :k], 16) | (bits[:, k:] & jnp.int32(-65536))


def _unpack_bf16_pair(w):
    lo = lax.bitcast_convert_type(lax.shift_left(w, 16), F32).astype(BF16)
    hi = lax.bitcast_convert_type(w & jnp.int32(-65536), F32).astype(BF16)
    return lo, hi


def _outproj_kernel(mg_ref, wo_ref, x_ref, gt_ref, g2_ref, sc_ref, sh_ref, wr_ref, br_ref,
                    x1_ref, h2p_ref, ids_ref, cw_ref):
    y = jnp.dot(mg_ref[0], wo_ref[...], preferred_element_type=F32)
    x1 = x_ref[0] + gt_ref[0] * y
    x1_ref[0] = x1
    h2 = _rms(x1, g2_ref[...]) * (1.0 + sc_ref[0]) + sh_ref[0]
    h2p_ref[0] = _pack_bf16_pair(h2)

    h_hi = h2.astype(BF16)
    h_lo = (h2 - h_hi.astype(F32)).astype(BF16)
    logits = (jnp.dot(h_hi, wr_ref[0], preferred_element_type=F32)
              + jnp.dot(h_lo, wr_ref[0], preferred_element_type=F32)
              + jnp.dot(h_hi, wr_ref[1], preferred_element_type=F32)) + br_ref[...]
    lane = lax.broadcasted_iota(I32, logits.shape, 1)
    is_g = (lane >= N_EXPERTS) & (lane < N_EXPERTS + N_GROUPS)
    neg = -jnp.inf
    gl = jnp.where(is_g, logits, neg)
    gmax = jnp.max(gl, axis=-1, keepdims=True)
    g_idx = jnp.min(jnp.where(gl == gmax, lane - N_EXPERTS, N_GROUPS), axis=-1, keepdims=True)
    p_g = 1.0 / jnp.sum(jnp.where(is_g, jnp.exp(gl - gmax), 0.0), axis=-1, keepdims=True)
    in_grp = (lane < N_EXPERTS) & ((lane // EXPERTS_PER_GROUP) == g_idx)
    el = jnp.where(in_grp, logits, neg)
    m1 = jnp.max(el, axis=-1, keepdims=True)
    i1 = jnp.min(jnp.where(el == m1, lane, LANES), axis=-1, keepdims=True)
    el2 = jnp.where(lane == i1, neg, el)
    m2 = jnp.max(el2, axis=-1, keepdims=True)
    i2 = jnp.min(jnp.where(el2 == m2, lane, LANES), axis=-1, keepdims=True)
    e2 = jnp.exp(m2 - m1)
    w1 = 1.0 / (1.0 + e2)
    w2 = e2 / (1.0 + e2)
    ids_ref[0] = jnp.where(lane == 0, i1, jnp.where(lane == 1, i2, 0))
    cw_ref[0] = p_g * jnp.where(lane == 0, w1, jnp.where(lane == 1, w2, 0.0))


def _outproj(merged3, w_out, x, gt1, g2, sc2, sh2, w_r, b_r):
    b, s, d = x.shape
    tm = 512
    row = lambda i, j: (i, j, 0)
    per_b = lambda i, j: (i, 0, 0)
    fixed = lambda i, j: (0, 0)
    return pl.pallas_call(
        _outproj_kernel,
        grid=(b, s // tm),
        in_specs=[pl.BlockSpec((1, tm, d), row),
                  pl.BlockSpec((d, d), fixed, pipeline_mode=pl.Buffered(1)),
                  pl.BlockSpec((1, tm, d), row),
                  pl.BlockSpec((1, 1, d), per_b),
                  pl.BlockSpec((1, d), fixed),
                  pl.BlockSpec((1, 1, d), per_b),
                  pl.BlockSpec((1, 1, d), per_b),
                  pl.BlockSpec((2, d, LANES), lambda i, j: (0, 0, 0)),
                  pl.BlockSpec((1, LANES), fixed)],
        out_specs=[pl.BlockSpec((1, tm, d), row),
                   pl.BlockSpec((1, tm, d // 2), row),
                   pl.BlockSpec((1, tm, LANES), row),
                   pl.BlockSpec((1, tm, LANES), row)],
        out_shape=[jax.ShapeDtypeStruct((b, s, d), F32),
                   jax.ShapeDtypeStruct((b, s, d // 2), I32),
                   jax.ShapeDtypeStruct((b, s, LANES), I32),
                   jax.ShapeDtypeStruct((b, s, LANES), F32)],
        compiler_params=_cparams("parallel", "parallel"),
        name="outproj_ln2_router",
    )(merged3, w_out, x, gt1, g2, sc2, sh2, w_r, b_r)


def _plan_kernel(ids_ref, dest_ref, te_ref, rank_ref):
    n = ids_ref.shape[0]
    cb = 256
    lane = lax.broadcasted_iota(I32, (cb, LANES), 1)
    lower = (lax.broadcasted_iota(I32, (cb, cb), 0) > lax.broadcasted_iota(I32, (cb, cb), 1)).astype(BF16)

    def count(bi, carry):
        rows = pl.ds(pl.multiple_of(bi * cb, cb), cb)
        ids = ids_ref[rows, :]
        m1 = lane == ids[:, 0:1]
        m2 = lane == ids[:, 1:2]
        onehot = m1 | m2
        before = jnp.dot(lower, onehot.astype(BF16), preferred_element_type=F32) + carry
        r1 = jnp.sum(jnp.where(m1, before, 0.0), axis=-1, keepdims=True)
        r2 = jnp.sum(jnp.where(m2, before, 0.0), axis=-1, keepdims=True)
        rank_ref[rows, :] = jnp.where(lane == 0, r1, jnp.where(lane == 1, r2, 0.0))
        return carry + jnp.sum(onehot.astype(F32), axis=0, keepdims=True)

    counts = lax.fori_loop(0, n // cb, count, jnp.zeros((1, LANES), F32))
    padded = jnp.floor((counts + (MOE_TILE - 1)) / MOE_TILE) * MOE_TILE
    upper = (lax.broadcasted_iota(I32, (LANES, LANES), 0)
             < lax.broadcasted_iota(I32, (LANES, LANES), 1)).astype(F32)
    offs = jnp.dot(jnp.broadcast_to(padded, (8, LANES)), upper, preferred_element_type=F32,
                   precision=lax.Precision.HIGHEST)[0:1]

    def place(bi, c):
        rows = pl.ds(pl.multiple_of(bi * cb, cb), cb)
        ids = ids_ref[rows, :]
        o1 = jnp.sum(jnp.where(lane == ids[:, 0:1], offs, 0.0), axis=-1, keepdims=True)
        o2 = jnp.sum(jnp.where(lane == ids[:, 1:2], offs, 0.0), axis=-1, keepdims=True)
        slot = rank_ref[rows, :] + jnp.where(lane == 0, o1, jnp.where(lane == 1, o2, 0.0))
        dest_ref[rows, :] = slot.astype(I32)
        return c

    lax.fori_loop(0, n // cb, place, 0)

    sq_row = lax.broadcasted_iota(I32, (LANES, LANES), 0)
    sq_lane = lax.broadcasted_iota(I32, (LANES, LANES), 1)
    tile_start = sq_row.astype(F32) * MOE_TILE
    finished = ((offs + padded) <= tile_start) & (sq_lane < N_EXPERTS)
    tile_expert = jnp.minimum(jnp.sum(finished.astype(F32), axis=-1, keepdims=True), N_EXPERTS - 1.0)
    n_tiles = jnp.sum(padded, axis=-1, keepdims=True) / MOE_TILE
    te_ref[...] = jnp.where(sq_lane == 1, n_tiles, tile_expert).astype(I32)


def _plan(ids):
    n = ids.shape[0]
    return pl.pallas_call(
        _plan_kernel,
        out_shape=[jax.ShapeDtypeStruct((n, LANES), I32),
                   jax.ShapeDtypeStruct((LANES, LANES), I32)],
        scratch_shapes=[pltpu.VMEM((n, LANES), F32)],
        compiler_params=pltpu.CompilerParams(vmem_limit_bytes=VMEM_LIMIT),
        name="moe_plan",
    )(ids)


def _dispatch_kernel(dest_ref, h_ref, hs_zero_ref, hs_ref, sem):
    del hs_zero_ref
    tq = h_ref.shape[0]

    def issue(r, c):
        for k in range(MOE_SLOTS):
            slot = dest_ref[0, 0, MOE_SLOTS * r + k]
            pltpu.make_async_copy(h_ref.at[pl.ds(r, 1)], hs_ref.at[pl.ds(slot, 1)], sem).start(priority=k)
        return c

    lax.fori_loop(0, tq, issue, 0, unroll=8)
    done = hs_ref.at[pl.ds(0, MOE_SLOTS * tq)]
    pltpu.make_async_copy(done, done, sem).wait()


def _dispatch(dest3, h2p, n_rows):
    n, w = h2p.shape
    tq = TOKEN_TILE
    hs_zero = jnp.zeros((n_rows, w), h2p.dtype)
    return pl.pallas_call(
        _dispatch_kernel,
        grid=(n // tq,),
        in_specs=[pl.BlockSpec((1, 1, MOE_SLOTS * tq), lambda i: (i, 0, 0), memory_space=pltpu.SMEM),
                  pl.BlockSpec((tq, w), lambda i: (i, 0)),
                  pl.BlockSpec(memory_space=pl.ANY)],
        out_specs=pl.BlockSpec(memory_space=pl.ANY),
        out_shape=jax.ShapeDtypeStruct((n_rows, w), h2p.dtype),
        scratch_shapes=[pltpu.SemaphoreType.DMA(())],
        input_output_aliases={2: 0},
        compiler_params=_cparams("arbitrary"),
        name="moe_dispatch",
    )(dest3, h2p, hs_zero)


def _expert_kernel(te_ref, nt_ref, hs_ref, wg_hbm, wu_hbm, wd_hbm, ys_ref,
                   wg_buf, wu_buf, wd_buf, sem, slot_ref):
    t = pl.program_id(0)
    nt = nt_ref[0]
    used = t < nt
    last = te_ref.shape[0] - 1

    def weight_copies(e, slot):
        return (pltpu.make_async_copy(wg_hbm.at[e], wg_buf.at[slot], sem.at[0, slot]),
                pltpu.make_async_copy(wu_hbm.at[e], wu_buf.at[slot], sem.at[1, slot]),
                pltpu.make_async_copy(wd_hbm.at[e], wd_buf.at[slot], sem.at[2, slot]))

    @pl.when(t == 0)
    def _():
        slot_ref[0] = 0
        for c in weight_copies(te_ref[0], 0):
            c.start()

    @pl.when(jnp.logical_not(used))
    def _():
        ys_ref[...] = jnp.zeros_like(ys_ref)

    @pl.when(used)
    def _():
        e = te_ref[t]
        first = (t == 0) | (te_ref[jnp.maximum(t - 1, 0)] != e)

        @pl.when(first & (t > 0))
        def _():
            slot_ref[0] = 1 - slot_ref[0]

        slot = slot_ref[0]

        @pl.when(first)
        def _():
            for c in weight_copies(e, slot):
                c.wait()
            nxt = lax.while_loop(lambda s: (s < nt) & (te_ref[jnp.minimum(s, last)] == e),
                                 lambda s: s + 1, t + 1)

            @pl.when(nxt < nt)
            def _():
                for c in weight_copies(te_ref[jnp.minimum(nxt, last)], 1 - slot):
                    c.start()

        lo, hi = _unpack_bf16_pair(hs_ref[...])
        k = lo.shape[1]

        def up(w_buf):
            return (jnp.dot(lo, w_buf[slot, :k, :].astype(BF16), preferred_element_type=F32)
                    + jnp.dot(hi, w_buf[slot, k:, :].astype(BF16), preferred_element_type=F32))

        a = up(wg_buf)
        u = up(wu_buf)
        act = (a * _sigmoid(a) * u).astype(BF16)
        ys_ref[...] = jnp.dot(act, wd_buf[slot].astype(BF16), preferred_element_type=F32)


def _experts(tile_expert, n_tiles, hs, wg, wu, wd):
    n_rows, hw = hs.shape
    d, f = wg.shape[1], wg.shape[2]
    row = lambda t, te, nt: (jnp.minimum(t, nt[0] - 1), 0)
    hbm = pl.BlockSpec(memory_space=pl.ANY)
    return pl.pallas_call(
        _expert_kernel,
        grid_spec=pltpu.PrefetchScalarGridSpec(
            num_scalar_prefetch=2,
            grid=(n_rows // MOE_TILE,),
            in_specs=[pl.BlockSpec((MOE_TILE, hw), row), hbm, hbm, hbm],
            out_specs=pl.BlockSpec((MOE_TILE, d), lambda t, te, nt: (t, 0)),
            scratch_shapes=[pltpu.VMEM((2, d, f), wg.dtype),
                            pltpu.VMEM((2, d, f), wu.dtype),
                            pltpu.VMEM((2, f, d), wd.dtype),
                            pltpu.SemaphoreType.DMA((3, 2)),
                            pltpu.SMEM((1,), I32)]),
        out_shape=jax.ShapeDtypeStruct((n_rows, d), F32),
        compiler_params=_cparams("arbitrary"),
        name="moe_experts",
    )(tile_expert, n_tiles, hs, wg, wu, wd)


def _combine_kernel(dest_ref, ys_ref, x1_ref, cw_ref, gt_ref, o_ref, buf, sem):
    tq = x1_ref.shape[1]

    def issue(r, c):
        for k in range(MOE_SLOTS):
            slot = dest_ref[0, 0, MOE_SLOTS * r + k]
            pltpu.make_async_copy(ys_ref.at[pl.ds(slot, 1)], buf.at[k, pl.ds(r, 1)], sem).start(priority=k)
        return c

    lax.fori_loop(0, tq, issue, 0, unroll=8)
    pltpu.make_async_copy(buf, buf, sem).wait()
    cw = cw_ref[0]
    y = cw[:, 0:1] * buf[0] + cw[:, 1:2] * buf[1]
    o_ref[0] = x1_ref[0] + gt_ref[0] * y


def _combine(dest3, ys, x1, cw, gt2):
    b, s, d = x1.shape
    tq = TOKEN_TILE
    spb = s // tq
    row = lambda i, j: (i, j, 0)
    return pl.pallas_call(
        _combine_kernel,
        grid=(b, spb),
        in_specs=[pl.BlockSpec((1, 1, MOE_SLOTS * tq), lambda i, j: (i * spb + j, 0, 0),
                               memory_space=pltpu.SMEM),
                  pl.BlockSpec(memory_space=pl.ANY),
                  pl.BlockSpec((1, tq, d), row),
                  pl.BlockSpec((1, tq, LANES), row),
                  pl.BlockSpec((1, 1, d), lambda i, j: (i, 0, 0))],
        out_specs=pl.BlockSpec((1, tq, d), row),
        out_shape=jax.ShapeDtypeStruct((b, s, d), F32),
        scratch_shapes=[pltpu.VMEM((MOE_SLOTS, tq, d), F32), pltpu.SemaphoreType.DMA(())],
        compiler_params=_cparams("arbitrary", "arbitrary"),
        name="moe_combine",
    )(dest3, ys, x1, cw, gt2)


def _moe(h2p, ids, cw, wg, wu, wd, x1, gt2):
    b, s, d = x1.shape
    n = b * s
    n_rows = _moe_max_tiles(n) * MOE_TILE
    dest, te = _plan(ids.reshape(n, LANES))
    dest3 = dest[:, :MOE_SLOTS].reshape(n // TOKEN_TILE, 1, MOE_SLOTS * TOKEN_TILE)
    hs = _dispatch(dest3, h2p.reshape(n, d // 2), n_rows)
    ys = _experts(te[:n_rows // MOE_TILE, 0], te[0, 1:2], hs, wg, wu, wd)
    return _combine(dest3, ys, x1, cw, gt2)


def _alibi_slopes():
    n = SWA_HEADS + DSA_HEADS
    i = jnp.arange(1, n + 1, dtype=F32)
    return jnp.exp2(-8.0 * i / n)


def _layer(x, mod, ln1_g, w_in, swa_q_norm, swa_k_norm, swa_sinks, dsa_q_norm, dsa_k_norm,
           w_branch_a, w_branch_b, w_out, ln2_g, w_group, b_group, w_expert, b_expert,
           w_gate_up, w_up, w_down, slopes):
    b, s, d = x.shape
    sh1, sc1, gt1, sh2, sc2, gt2 = [m[:, None, :] for m in jnp.split(mod, 6, axis=-1)]

    w_in_t = jnp.swapaxes(w_in, 0, 1)

    h1 = _ln_mod(x, ln1_g[None, :], sc1, sh1)
    h1f = h1.reshape(b * s, d)
    ones = lambda n: jnp.ones((n,), F32)
    mult = jnp.concatenate([
        jnp.tile(swa_q_norm, SWA_HEADS) * HEAD_DIM ** -0.5, jnp.tile(swa_k_norm, SWA_KV_HEADS),
        ones(SWA_KV_HEADS * HEAD_DIM),
        jnp.tile(dsa_q_norm, DSA_HEADS) * HEAD_DIM ** -0.5, dsa_k_norm, ones(HEAD_DIM),
        ones(IDX_HEADS * IDX_DIM) * IDX_DIM ** -0.5, ones(IDX_DIM), ones(IDX_HEADS) * IDX_HEADS ** -0.5,
        ones(ATT_WIDTH - COL_GATE)])[None, :]
    col = jnp.arange(ATT_WIDTH)
    normed = (((col >= COL_QA) & (col < COL_VA)) | ((col >= COL_QB) & (col < COL_KVB + KVB_V))
              ).astype(F32)[None, :]
    proj = _proj(h1f, w_in_t, mult, normed, 1024, 768).reshape(b, s, ATT_WIDTH)
    oa = _swa(proj, swa_sinks, slopes[:SWA_HEADS])
    ob = _dsa(proj, slopes[SWA_HEADS:])
    merged = _merge(h1f, oa.reshape(b * s, -1), ob.reshape(b * s, -1), w_in_t,
                    w_branch_a.astype(BF16), w_branch_b.astype(BF16))

    w_r = jnp.concatenate([w_expert, w_group,
                           jnp.zeros((d, LANES - N_EXPERTS - N_GROUPS), F32)], axis=1)
    b_r = jnp.concatenate([b_expert, b_group,
                           jnp.zeros((LANES - N_EXPERTS - N_GROUPS,), F32)])[None, :]
    w_r_hi = _bf16_head(w_r)
    w_r_split = jnp.stack([w_r_hi.astype(BF16), (w_r - w_r_hi).astype(BF16)])
    x1, h2p, ids, cw = _outproj(merged.reshape(b, s, d), w_out.astype(BF16), x, gt1,
                                ln2_g[None, :], sc2, sh2, w_r_split, b_r)
    f = w_gate_up.shape[-1]
    return _moe(h2p, ids, cw, w_gate_up.reshape(N_EXPERTS, d, f), w_up.reshape(N_EXPERTS, d, f),
                w_down.reshape(N_EXPERTS, f, d), x1, gt2)


def kernel(x, c, w_ada, b_ada, ln1_g, w_in, swa_q_norm, swa_k_norm, swa_sinks, dsa_q_norm, dsa_k_norm,
           w_branch_a, w_branch_b, w_out, ln2_g, w_group, b_group, w_expert, b_expert,
           w_gate_up, w_up, w_down):
    slopes = _alibi_slopes()
    bsz = c.shape[0]
    c_pad = jnp.concatenate([c, jnp.zeros((ADA_ROWS - bsz, c.shape[1]), c.dtype)], axis=0)
    for l in range(w_ada.shape[0]):
        mod = _ada(c_pad, w_ada[l], b_ada[l][None, :])[:bsz]
        x = _layer(x, mod, ln1_g[l], w_in[l], swa_q_norm[l], swa_k_norm[l], swa_sinks[l],
                   dsa_q_norm[l], dsa_k_norm[l], w_branch_a[l], w_branch_b[l], w_out[l], ln2_g[l],
                   w_group[l], b_group[l], w_expert[l], b_expert[l],
                   w_gate_up[l], w_up[l], w_down[l], slopes)
    return x
```

```python
import functools

import jax
import jax.numpy as jnp
from jax import lax
from jax.experimental import pallas as pl
from jax.experimental.pallas import tpu as pltpu

F32 = jnp.float32
BF16 = jnp.bfloat16
I32 = jnp.int32

D_MODEL = 2048
HEAD_DIM = 64
SWA_HEADS = 16
SWA_KV_HEADS = 4
SWA_GROUP = SWA_HEADS // SWA_KV_HEADS
WINDOW = 128
DSA_HEADS = 16
IDX_HEADS = 16
IDX_DIM = 64
DSA_TOPK_MAX = 256
DSA_BLOCK = 128
N_GROUPS = 4
EXPERTS_PER_GROUP = 8
N_EXPERTS = N_GROUPS * EXPERTS_PER_GROUP
D_EXPERT = 512
RMS_EPS = 1e-6
NEG_INF = -1e30
INT_MIN = -(2 ** 31)

LANES = 128
VMEM_LIMIT = 56 * 1024 * 1024

COL_QA, COL_KA, COL_VA, COL_QB = 0, 1024, 1280, 1536
COL_KVB = 2560
COL_QI = 2688
COL_KWI = 3712
COL_GATE = 3792
ATT_WIDTH = 3840
KVB_K, KVB_V = 0, HEAD_DIM
KWI_K, KWI_W = 0, IDX_DIM

NT_DIMS = (((1,), (1,)), ((), ()))


def _cparams(*sem):
    return pltpu.CompilerParams(dimension_semantics=sem, vmem_limit_bytes=VMEM_LIMIT)


def _sigmoid(x):
    return 1.0 / (1.0 + jnp.exp(-x))


def _rms(x, g):
    return x * lax.rsqrt(jnp.mean(x * x, axis=-1, keepdims=True) + RMS_EPS) * g


def _ada_kernel(c_ref, w_ref, b_ref, o_ref):
    c = c_ref[...]
    cond = c * _sigmoid(c)
    c_hi = cond.astype(BF16)
    c_lo = (cond - c_hi.astype(F32)).astype(BF16)
    w = w_ref[...]
    w_hi = w.astype(BF16)
    w_lo = (w - w_hi.astype(F32)).astype(BF16)
    o_ref[...] = (jnp.dot(c_hi, w_hi, preferred_element_type=F32)
                  + jnp.dot(c_lo, w_hi, preferred_element_type=F32)
                  + jnp.dot(c_hi, w_lo, preferred_element_type=F32)) + b_ref[...]


ADA_ROWS = 16


def _ada(c_pad, w, b):
    n = w.shape[1]
    tn = 1024
    return pl.pallas_call(
        _ada_kernel,
        grid=(n // tn,),
        in_specs=[pl.BlockSpec((ADA_ROWS, D_MODEL), lambda j: (0, 0)),
                  pl.BlockSpec((D_MODEL, tn), lambda j: (0, j)),
                  pl.BlockSpec((1, tn), lambda j: (0, j))],
        out_specs=pl.BlockSpec((ADA_ROWS, tn), lambda j: (0, j)),
        out_shape=jax.ShapeDtypeStruct((ADA_ROWS, n), F32),
        compiler_params=_cparams("parallel"),
        name="ada_mod",
    )(c_pad, w, b)


def _ln_mod_kernel(x_ref, g_ref, sc_ref, sh_ref, o_ref):
    y = _rms(x_ref[0], g_ref[...])
    o_ref[0] = (y * (1.0 + sc_ref[0]) + sh_ref[0]).astype(o_ref.dtype)


def _ln_mod(x, g, sc, sh):
    b, s, d = x.shape
    ts = 512
    return pl.pallas_call(
        _ln_mod_kernel,
        grid=(b, s // ts),
        in_specs=[pl.BlockSpec((1, ts, d), lambda i, j: (i, j, 0)),
                  pl.BlockSpec((1, d), lambda i, j: (0, 0)),
                  pl.BlockSpec((1, 1, d), lambda i, j: (i, 0, 0)),
                  pl.BlockSpec((1, 1, d), lambda i, j: (i, 0, 0))],
        out_specs=pl.BlockSpec((1, ts, d), lambda i, j: (i, j, 0)),
        out_shape=jax.ShapeDtypeStruct((b, s, d), BF16),
        compiler_params=_cparams("parallel", "parallel"),
        name="ln_mod",
    )(x, g, sc, sh)


def _proj_kernel(a_ref, wt_ref, mult_ref, normed_ref, o_ref):
    y = lax.dot_general(a_ref[...], wt_ref[...].astype(BF16), NT_DIMS, preferred_element_type=F32)
    tm, tn = y.shape
    half = lax.broadcasted_iota(I32, (tm, LANES), 1) < HEAD_DIM
    for g in range(tn // LANES):
        cols = slice(g * LANES, (g + 1) * LANES)
        yg = y[:, cols]
        sq = yg * yg
        ms_lo = jnp.sum(jnp.where(half, sq, 0.0), axis=-1, keepdims=True) / HEAD_DIM
        ms_hi = jnp.sum(jnp.where(half, 0.0, sq), axis=-1, keepdims=True) / HEAD_DIM
        r = lax.rsqrt(jnp.where(half, ms_lo, ms_hi) + RMS_EPS)
        o_ref[:, cols] = yg * jnp.where(normed_ref[:, cols] > 0.0, r, 1.0) * mult_ref[:, cols]


def _proj(a, wt, mult, normed, tm, tn):
    m, k = a.shape
    n = mult.shape[1]
    vec = pl.BlockSpec((1, tn), lambda i, j: (0, j))
    return pl.pallas_call(
        _proj_kernel,
        grid=(m // tm, n // tn),
        in_specs=[pl.BlockSpec((tm, k), lambda i, j: (i, 0)),
                  pl.BlockSpec((tn, k), lambda i, j: (j, 0)),
                  vec, vec],
        out_specs=pl.BlockSpec((tm, tn), lambda i, j: (i, j)),
        out_shape=jax.ShapeDtypeStruct((m, n), F32),
        compiler_params=_cparams("parallel", "parallel"),
        name="proj_att",
    )(a, wt, mult, normed)


def _swa_kernel(sink_ref, slope_ref, q_ref, kp_ref, kc_ref, vp_ref, vc_ref, o_ref, lhs, e_s):
    n = pl.program_id(1)
    w, w2 = WINDOW, 2 * WINDOW
    q = q_ref[0]
    kw = jnp.concatenate([kp_ref[0], kc_ref[0]], axis=0)
    vw = jnp.concatenate([vp_ref[0], vc_ref[0]], axis=0)
    half_q = lax.broadcasted_iota(I32, (w, LANES), 1) < HEAD_DIM
    half_k = lax.broadcasted_iota(I32, (w2, LANES), 1) < HEAD_DIM
    row = lax.broadcasted_iota(I32, (w, w2), 0)
    col = lax.broadcasted_iota(I32, (w, w2), 1)
    rel = row + w - col
    valid = (rel >= 0) & (rel < w) & ((col >= w) | (n > 0))
    mbias = jnp.where(valid, 0.0, NEG_INF)
    relf = rel.astype(F32)

    ones_col = jnp.ones((w2, LANES), BF16)
    for g in range(SWA_KV_HEADS):
        if g % 2 == 0:
            grp = slice((g // 2) * LANES, (g // 2 + 1) * LANES)
            kn = kw[:, grp]
            kn_sw = pltpu.roll(kn, HEAD_DIM, axis=1)
            vp = vw[:, grp]
            vp_sw = pltpu.roll(vp, HEAD_DIM, axis=1)
        kdup = jnp.where(half_k, kn, kn_sw) if g % 2 == 0 else jnp.where(half_k, kn_sw, kn)
        vdup = jnp.where(half_k, vp, vp_sw) if g % 2 == 0 else jnp.where(half_k, vp_sw, vp)
        vx = jnp.concatenate([vdup.astype(BF16), ones_col], axis=1)
        for r in range(SWA_GROUP):
            h = g * SWA_GROUP + r
            if r % 2 == 0:
                qn = q[:, (h // 2) * LANES:(h // 2 + 1) * LANES]
            keep = half_q if r % 2 == 0 else jnp.logical_not(half_q)
            lhs[r * w:(r + 1) * w, :] = jnp.where(keep, qn, 0.0).astype(BF16)
        s_all = lax.dot_general(lhs[...], kdup.astype(BF16), NT_DIMS, preferred_element_type=F32)
        sink_e = []
        for r in range(SWA_GROUP):
            h = g * SWA_GROUP + r
            rows = slice(r * w, (r + 1) * w)
            s = s_all[rows, :] + (mbias - slope_ref[h] * relf)
            sink = jnp.full((w, LANES), sink_ref[h], F32)
            m = jnp.maximum(jnp.max(s, axis=-1, keepdims=True), sink)
            e_s[rows, :] = jnp.exp(s - jnp.concatenate([m, m], axis=1)).astype(BF16)
            sink_e.append(jnp.exp(sink - m))
        pv = jnp.dot(e_s[...], vx, preferred_element_type=F32)
        for r in range(0, SWA_GROUP, 2):
            h = g * SWA_GROUP + r
            ev, od = pv[r * w:(r + 1) * w, :], pv[(r + 1) * w:(r + 2) * w, :]
            den = jnp.where(half_q, ev[:, LANES:] + sink_e[r], od[:, LANES:] + sink_e[r + 1])
            o = jnp.where(half_q, ev[:, :LANES], od[:, :LANES]) / den
            o_ref[0, :, (h // 2) * LANES:(h // 2 + 1) * LANES] = o.astype(o_ref.dtype)


def _swa(proj3, sinks, slopes):
    b, s, _ = proj3.shape
    nb = s // WINDOW
    kvw = SWA_KV_HEADS * HEAD_DIM
    qw = SWA_HEADS * HEAD_DIM
    smem = pl.BlockSpec(memory_space=pltpu.SMEM)
    prev =lambda i, n: (i, jnp.maximum(n - 1, 0), COL_KA // kvw)
    cur = lambda i, n: (i, n, COL_KA // kvw)
    prev_v = lambda i, n: (i, jnp.maximum(n - 1, 0), COL_VA // kvw)
    cur_v = lambda i, n: (i, n, COL_VA // kvw)
    return pl.pallas_call(
        _swa_kernel,
        grid=(b, nb),
        in_specs=[smem, smem,
                  pl.BlockSpec((1, WINDOW, qw), lambda i, n: (i, n, COL_QA // qw)),
                  pl.BlockSpec((1, WINDOW, kvw), prev),
                  pl.BlockSpec((1, WINDOW, kvw), cur),
                  pl.BlockSpec((1, WINDOW, kvw), prev_v),
                  pl.BlockSpec((1, WINDOW, kvw), cur_v)],
        out_specs=pl.BlockSpec((1, WINDOW, qw), lambda i, n: (i, n, 0)),
        out_shape=jax.ShapeDtypeStruct((b, s, qw), BF16),
        scratch_shapes=[pltpu.VMEM((SWA_GROUP * WINDOW, LANES), BF16),
                        pltpu.VMEM((SWA_GROUP * WINDOW, 2 * WINDOW), BF16)],
        compiler_params=_cparams("parallel", "parallel"),
        name="swa_attn",
    )(sinks, slopes, proj3, proj3, proj3, proj3, proj3)


DSA_CHUNK = 256
ALIBI_SPLIT = 3
POS_LO_BITS = 7


def _bf16_head(x):
    return lax.bitcast_convert_type(lax.bitcast_convert_type(x, I32) & jnp.int32(-65536), F32)


def _alibi_columns(slopes):
    parts, rest = [], slopes.astype(F32)
    for _ in range(ALIBI_SPLIT):
        piece = _bf16_head(rest)
        parts.append(piece)
        rest = rest - piece
    cols = [p * float(2 ** POS_LO_BITS) for p in parts] + parts
    pad = jnp.zeros((slopes.shape[0], LANES - 2 * ALIBI_SPLIT), F32)
    return jnp.concatenate([jnp.stack(cols, axis=1), pad], axis=1)


def _dsa_kernel(*refs, topk):
    n_pairs = DSA_HEADS // 2
    qb_refs, qi_refs = refs[:n_pairs], refs[n_pairs:2 * n_pairs]
    (kwi_blk_ref, kvb_ref, kwi_ref, acol_ref, o_ref,
     ki2, kx, vx, lhs_i, lhs_q, wb, key_s, keyt_s, s_s, thr_s, acc_s, m_s, e_s) = refs[2 * n_pairs:]
    i = pl.program_id(1)
    blk, ck = DSA_BLOCK, DSA_CHUNK
    nch = (i + 2) // 2
    seq = ki2.shape[0]
    max_ch = seq // ck
    half = lax.broadcasted_iota(I32, (blk, LANES), 1) < HEAD_DIM

    @pl.when(i == 0)
    def _():
        kvb = kvb_ref[0]
        kib = kwi_ref[0][:, KWI_K:KWI_K + IDX_DIM].astype(BF16)
        ki2[...] = jnp.concatenate([kib, kib], axis=1)
        kn = kvb[:, KVB_K:KVB_K + HEAD_DIM].astype(BF16)
        pos = lax.broadcasted_iota(I32, (seq, LANES), 0)
        lane = lax.broadcasted_iota(I32, (seq, LANES), 1)
        hi = lax.shift_right_logical(pos, POS_LO_BITS).astype(F32)
        lo = (pos & (2 ** POS_LO_BITS - 1)).astype(F32)
        posc = jnp.where(lane < ALIBI_SPLIT, hi, jnp.where(lane < 2 * ALIBI_SPLIT, lo, 0.0)).astype(BF16)
        kx[...] = jnp.concatenate([kn, kn, posc], axis=1)
        ones = jnp.where(lax.broadcasted_iota(I32, (seq, HEAD_DIM), 1) == 0, 1.0, 0.0).astype(BF16)
        vx[...] = jnp.concatenate([kvb[:, KVB_V:KVB_V + HEAD_DIM].astype(BF16), ones], axis=1)

    wi = kwi_blk_ref[0][:, KWI_W:KWI_W + IDX_HEADS]
    for p in range(n_pairs):
        qn = qb_refs[p][0]
        qip = qi_refs[p][0]
        for k in range(2):
            h = 2 * p + k
            keep = half if k == 0 else jnp.logical_not(half)
            rows = slice(h * blk, (h + 1) * blk)
            lhs_i[rows, :] = jnp.where(keep, qip, 0.0).astype(BF16)
            lhs_q[rows, :LANES] = jnp.where(keep, qn, 0.0).astype(BF16)
            lhs_q[rows, LANES:] = jnp.broadcast_to(acol_ref[h:h + 1, :], (blk, LANES)).astype(BF16)
            wb[h] = jnp.broadcast_to(wi[:, h:h + 1], (blk, LANES))

    row_t = i * blk + lax.broadcasted_iota(I32, (blk, ck), 0)
    col_k = lax.broadcasted_iota(I32, (blk, ck), 1)

    def score_chunk(c, carry):
        ks = pl.ds(pl.multiple_of(c * ck, ck), ck)
        logits = lax.dot_general(lhs_i[...], ki2[ks, :], NT_DIMS, preferred_element_type=F32)
        sc = jnp.zeros((blk, ck), F32)
        for h in range(IDX_HEADS):
            w = wb[h]
            lh = jnp.maximum(logits[h * blk:(h + 1) * blk, :], 0.0)
            sc = sc + lh * jnp.concatenate([w, w], axis=1)
        bits = lax.bitcast_convert_type(sc + 0.0, I32)
        key = jnp.where(bits < 0, bits ^ jnp.int32(0x7FFFFFFF), bits)
        key = jnp.where(c * ck + col_k <= row_t, key, jnp.int32(INT_MIN))
        key_s[c] = key
        keyt_s[c] = key.T
        return carry

    lax.fori_loop(0, nch, score_chunk, 0)

    thr_s[...] = jnp.full(thr_s.shape, INT_MIN, I32)
    for n in range(2, max_ch + 1):
        @pl.when(nch == n)
        def _(n=n):
            def step(it, thr):
                cand = thr + lax.shift_left(jnp.int32(1), 31 - it)
                cnt = jnp.zeros((1, blk), F32)
                for c in range(n):
                    cnt = cnt + jnp.sum(jnp.where(keyt_s[c] >= cand, 1.0, 0.0), axis=0, keepdims=True)
                return jnp.where(cnt >= float(topk), cand, thr)

            thr = lax.fori_loop(0, 32, step, jnp.full((1, blk), INT_MIN, I32))
            thr_s[...] = jnp.broadcast_to(thr, thr_s.shape).T

    thr = thr_s[...]
    thr2 = jnp.concatenate([thr, thr], axis=1)

    def count_gt(c, cnt):
        k = key_s[c]
        return cnt + jnp.where(k[:, :LANES] > thr, 1.0, 0.0) + jnp.where(k[:, LANES:] > thr, 1.0, 0.0)

    n_gt = jnp.sum(lax.fori_loop(0, nch, count_gt, jnp.zeros((blk, LANES), F32)), axis=-1, keepdims=True)
    need = float(topk) - n_gt
    upper = (lax.broadcasted_iota(I32, (ck, ck), 0) < lax.broadcasted_iota(I32, (ck, ck), 1)).astype(BF16)

    m_s[...] = jnp.full(m_s.shape, -jnp.inf, F32)

    def masked_scores(c, n_eq):
        ks = pl.ds(pl.multiple_of(c * ck, ck), ck)
        k = key_s[c]
        eq = (k == thr2) & (c * ck + col_k <= row_t)
        eqf = eq.astype(BF16)
        before = jnp.dot(eqf, upper, preferred_element_type=F32) + n_eq
        sel = (k > thr2) | (eq & (before < need))
        mb = jnp.where(sel, 0.0, NEG_INF)
        s_all = lax.dot_general(lhs_q[...], kx[ks, :], NT_DIMS, preferred_element_type=F32)
        for h in range(DSA_HEADS):
            rows = slice(h * blk, (h + 1) * blk)
            s = s_all[rows, :] + mb
            s_s[c, rows, :] = s
            m_s[rows, :] = jnp.maximum(m_s[rows, :], jnp.maximum(s[:, :LANES], s[:, LANES:]))
        return n_eq + jnp.sum(eqf.astype(F32), axis=-1, keepdims=True)

    lax.fori_loop(0, nch, masked_scores, jnp.zeros((blk, 1), F32))

    m_s[...] = jnp.broadcast_to(jnp.max(m_s[...], axis=-1, keepdims=True), m_s.shape)
    acc_s[...] = jnp.zeros_like(acc_s)

    def attend(c, carry):
        ks = pl.ds(pl.multiple_of(c * ck, ck), ck)
        for h in range(DSA_HEADS):
            rows = slice(h * blk, (h + 1) * blk)
            m = m_s[rows, :]
            e_s[rows, :] = jnp.exp(s_s[c, rows, :] - jnp.concatenate([m, m], axis=1)).astype(BF16)
        acc_s[...] += jnp.dot(e_s[...], vx[ks, :], preferred_element_type=F32)
        return carry

    lax.fori_loop(0, nch, attend, 0)

    for h in range(DSA_HEADS):
        a = acc_s[h * blk:(h + 1) * blk, :]
        o = a[:, :HEAD_DIM] / a[:, HEAD_DIM:HEAD_DIM + 1]
        o_ref[0, :, h * HEAD_DIM:(h + 1) * HEAD_DIM] = o.astype(o_ref.dtype)


def _dsa(proj3, slopes):
    b, s, _ = proj3.shape
    nb = s // DSA_BLOCK
    qw = DSA_HEADS * HEAD_DIM
    n_pairs = DSA_HEADS // 2
    rows = DSA_HEADS * DSA_BLOCK
    fixed = lambda i, n: (0, 0)

    def group(col):
        return pl.BlockSpec((1, DSA_BLOCK, LANES), lambda i, n: (i, n, col // LANES))

    def group_all(col):
        return pl.BlockSpec((1, s, LANES), lambda i, n: (i, 0, col // LANES))

    return pl.pallas_call(
        functools.partial(_dsa_kernel, topk=min(DSA_TOPK_MAX, s // 4)),
        grid=(b, nb),
        in_specs=([group(COL_QB + p * LANES) for p in range(n_pairs)]
                  + [group(COL_QI + p * LANES) for p in range(n_pairs)]
                  + [group(COL_KWI), group_all(COL_KVB), group_all(COL_KWI),
                     pl.BlockSpec((DSA_HEADS, LANES), fixed)]),
        out_specs=pl.BlockSpec((1, DSA_BLOCK, qw), lambda i, n: (i, n, 0)),
        out_shape=jax.ShapeDtypeStruct((b, s, qw), BF16),
        scratch_shapes=[pltpu.VMEM((s, LANES), BF16),
                        pltpu.VMEM((s, 2 * LANES), BF16),
                        pltpu.VMEM((s, LANES), BF16),
                        pltpu.VMEM((rows, LANES), BF16),
                        pltpu.VMEM((rows, 2 * LANES), BF16),
                        pltpu.VMEM((IDX_HEADS, DSA_BLOCK, LANES), F32),
                        pltpu.VMEM((s // DSA_CHUNK, DSA_BLOCK, DSA_CHUNK), I32),
                        pltpu.VMEM((s // DSA_CHUNK, DSA_CHUNK, DSA_BLOCK), I32),
                        pltpu.VMEM((s // DSA_CHUNK, rows, DSA_CHUNK), F32),
                        pltpu.VMEM((DSA_BLOCK, LANES), I32),
                        pltpu.VMEM((rows, LANES), F32),
                        pltpu.VMEM((rows, LANES), F32),
                        pltpu.VMEM((rows, DSA_CHUNK), BF16)],
        compiler_params=_cparams("arbitrary", "arbitrary"),
        name="dsa_attn",
    )(*([proj3] * (2 * n_pairs + 3)), _alibi_columns(slopes))


def _merge_kernel(h_ref, oa_ref, ob_ref, wt_hbm, wa_ref, wb_ref, o_ref, g32, gbf, sem):
    j, i = pl.program_id(0), pl.program_id(1)
    tn, d = o_ref.shape[1], h_ref.shape[1]

    def gate_copies(jj, slot):
        return [pltpu.make_async_copy(
            wt_hbm.at[pl.ds(pl.multiple_of(COL_GATE + k * d + jj * tn, 8), tn)], g32.at[slot, k], sem.at[slot, k])
            for k in range(2)]

    @pl.when((j == 0) & (i == 0))
    def _():
        for c in gate_copies(0, 0):
            c.start()

    @pl.when(i == 0)
    def _():
        slot = j % 2
        for c in gate_copies(j, slot):
            c.wait()
        gbf[...] = g32[slot].astype(BF16)

        @pl.when(j + 1 < pl.num_programs(0))
        def _():
            for c in gate_copies(j + 1, 1 - slot):
                c.start()

    h = h_ref[...]
    ga = lax.dot_general(h, gbf[0], NT_DIMS, preferred_element_type=F32)
    gb = lax.dot_general(h, gbf[1], NT_DIMS, preferred_element_type=F32)
    a = jnp.dot(oa_ref[...], wa_ref[...].astype(BF16), preferred_element_type=F32)
    b = jnp.dot(ob_ref[...], wb_ref[...].astype(BF16), preferred_element_type=F32)
    o_ref[...] = (_sigmoid(ga) * a + _sigmoid(gb) * b).astype(o_ref.dtype)


def _merge(h, oa, ob, w_in_t, wa, wb):
    m, d = h.shape
    tm, tn = 1024, 512
    return pl.pallas_call(
        _merge_kernel,
        grid=(d // tn, m // tm),
        in_specs=[pl.BlockSpec((tm, d), lambda j, i: (i, 0)),
                  pl.BlockSpec((tm, oa.shape[1]), lambda j, i: (i, 0)),
                  pl.BlockSpec((tm, ob.shape[1]), lambda j, i: (i, 0)),
                  pl.BlockSpec(memory_space=pl.ANY),
                  pl.BlockSpec((wa.shape[0], tn), lambda j, i: (0, j)),
                  pl.BlockSpec((wb.shape[0], tn), lambda j, i: (0, j))],
        out_specs=pl.BlockSpec((tm, tn), lambda j, i: (i, j)),
        out_shape=jax.ShapeDtypeStruct((m, d), BF16),
        scratch_shapes=[pltpu.VMEM((2, 2, tn, d), w_in_t.dtype),
                        pltpu.VMEM((2, tn, d), BF16),
                        pltpu.SemaphoreType.DMA((2, 2))],
        compiler_params=_cparams("arbitrary", "arbitrary"),
        name="merge_branches",
    )(h, oa, ob, w_in_t, wa, wb)


MOE_TILE = 256
MOE_SLOTS = 2
TOKEN_TILE = 512


def _moe_max_tiles(n_tokens):
    return -(-(n_tokens * MOE_SLOTS + N_EXPERTS * (MOE_TILE - 1)) // MOE_TILE)


def _pack_bf16_pair(x):
    k = x.shape[1] // 2
    bits = lax.bitcast_convert_type(x.astype(BF16).astype(F32), I32)
    return lax.shift_right_logical(bits[:, :k], 16) | (bits[:, k:] & jnp.int32(-65536))


def _unpack_bf16_pair(w):
    lo = lax.bitcast_convert_type(lax.shift_left(w, 16), F32).astype(BF16)
    hi = lax.bitcast_convert_type(w & jnp.int32(-65536), F32).astype(BF16)
    return lo, hi


def _outproj_kernel(mg_ref, wo_ref, x_ref, gt_ref, g2_ref, sc_ref, sh_ref, wr_ref, br_ref,
                    x1_ref, h2p_ref, ids_ref, cw_ref):
    y = jnp.dot(mg_ref[0], wo_ref[...].astype(BF16), preferred_element_type=F32)
    x1 = x_ref[0] + gt_ref[0] * y
    x1_ref[0] = x1
    h2 = _rms(x1, g2_ref[...]) * (1.0 + sc_ref[0]) + sh_ref[0]
    h2p_ref[0] = _pack_bf16_pair(h2)

    h_hi = h2.astype(BF16)
    h_lo = (h2 - h_hi.astype(F32)).astype(BF16)
    logits = (jnp.dot(h_hi, wr_ref[0], preferred_element_type=F32)
              + jnp.dot(h_lo, wr_ref[0], preferred_element_type=F32)
              + jnp.dot(h_hi, wr_ref[1], preferred_element_type=F32)) + br_ref[...]
    lane = lax.broadcasted_iota(I32, logits.shape, 1)
    is_g = (lane >= N_EXPERTS) & (lane < N_EXPERTS + N_GROUPS)
    neg = -jnp.inf
    gl = jnp.where(is_g, logits, neg)
    gmax = jnp.max(gl, axis=-1, keepdims=True)
    g_idx = jnp.min(jnp.where(gl == gmax, lane - N_EXPERTS, N_GROUPS), axis=-1, keepdims=True)
    p_g = 1.0 / jnp.sum(jnp.where(is_g, jnp.exp(gl - gmax), 0.0), axis=-1, keepdims=True)
    in_grp = (lane < N_EXPERTS) & ((lane // EXPERTS_PER_GROUP) == g_idx)
    el = jnp.where(in_grp, logits, neg)
    m1 = jnp.max(el, axis=-1, keepdims=True)
    i1 = jnp.min(jnp.where(el == m1, lane, LANES), axis=-1, keepdims=True)
    el2 = jnp.where(lane == i1, neg, el)
    m2 = jnp.max(el2, axis=-1, keepdims=True)
    i2 = jnp.min(jnp.where(el2 == m2, lane, LANES), axis=-1, keepdims=True)
    e2 = jnp.exp(m2 - m1)
    w1 = 1.0 / (1.0 + e2)
    w2 = e2 / (1.0 + e2)
    ids_ref[0] = jnp.where(lane == 0, i1, jnp.where(lane == 1, i2, 0))
    cw_ref[0] = p_g * jnp.where(lane == 0, w1, jnp.where(lane == 1, w2, 0.0))


def _outproj(merged3, w_out, x, gt1, g2, sc2, sh2, w_r, b_r):
    b, s, d = x.shape
    tm = 512
    row = lambda i, j: (i, j, 0)
    per_b = lambda i, j: (i, 0, 0)
    fixed = lambda i, j: (0, 0)
    return pl.pallas_call(
        _outproj_kernel,
        grid=(b, s // tm),
        in_specs=[pl.BlockSpec((1, tm, d), row),
                  pl.BlockSpec((d, d), fixed, pipeline_mode=pl.Buffered(1)),
                  pl.BlockSpec((1, tm, d), row),
                  pl.BlockSpec((1, 1, d), per_b),
                  pl.BlockSpec((1, d), fixed),
                  pl.BlockSpec((1, 1, d), per_b),
                  pl.BlockSpec((1, 1, d), per_b),
                  pl.BlockSpec((2, d, LANES), lambda i, j: (0, 0, 0)),
                  pl.BlockSpec((1, LANES), fixed)],
        out_specs=[pl.BlockSpec((1, tm, d), row),
                   pl.BlockSpec((1, tm, d // 2), row),
                   pl.BlockSpec((1, tm, LANES), row),
                   pl.BlockSpec((1, tm, LANES), row)],
        out_shape=[jax.ShapeDtypeStruct((b, s, d), F32),
                   jax.ShapeDtypeStruct((b, s, d // 2), I32),
                   jax.ShapeDtypeStruct((b, s, LANES), I32),
                   jax.ShapeDtypeStruct((b, s, LANES), F32)],
        compiler_params=_cparams("parallel", "parallel"),
        name="outproj_ln2_router",
    )(merged3, w_out, x, gt1, g2, sc2, sh2, w_r, b_r)


def _plan_kernel(ids_ref, dest_ref, te_ref, rank_ref):
    n = ids_ref.shape[0]
    cb = 256
    lane = lax.broadcasted_iota(I32, (cb, LANES), 1)
    lower = (lax.broadcasted_iota(I32, (cb, cb), 0) > lax.broadcasted_iota(I32, (cb, cb), 1)).astype(BF16)

    def count(bi, carry):
        rows = pl.ds(pl.multiple_of(bi * cb, cb), cb)
        ids = ids_ref[rows, :]
        m1 = lane == ids[:, 0:1]
        m2 = lane == ids[:, 1:2]
        onehot = m1 | m2
        before = jnp.dot(lower, onehot.astype(BF16), preferred_element_type=F32) + carry
        r1 = jnp.sum(jnp.where(m1, before, 0.0), axis=-1, keepdims=True)
        r2 = jnp.sum(jnp.where(m2, before, 0.0), axis=-1, keepdims=True)
        rank_ref[rows, :] = jnp.where(lane == 0, r1, jnp.where(lane == 1, r2, 0.0))
        return carry + jnp.sum(onehot.astype(F32), axis=0, keepdims=True)

    counts = lax.fori_loop(0, n // cb, count, jnp.zeros((1, LANES), F32))
    padded = jnp.floor((counts + (MOE_TILE - 1)) / MOE_TILE) * MOE_TILE
    upper = (lax.broadcasted_iota(I32, (LANES, LANES), 0)
             < lax.broadcasted_iota(I32, (LANES, LANES), 1)).astype(F32)
    offs = jnp.dot(jnp.broadcast_to(padded, (8, LANES)), upper, preferred_element_type=F32,
                   precision=lax.Precision.HIGHEST)[0:1]

    def place(bi, c):
        rows = pl.ds(pl.multiple_of(bi * cb, cb), cb)
        ids = ids_ref[rows, :]
        o1 = jnp.sum(jnp.where(lane == ids[:, 0:1], offs, 0.0), axis=-1, keepdims=True)
        o2 = jnp.sum(jnp.where(lane == ids[:, 1:2], offs, 0.0), axis=-1, keepdims=True)
        slot = rank_ref[rows, :] + jnp.where(lane == 0, o1, jnp.where(lane == 1, o2, 0.0))
        dest_ref[rows, :] = slot.astype(I32)
        return c

    lax.fori_loop(0, n // cb, place, 0)

    sq_row = lax.broadcasted_iota(I32, (LANES, LANES), 0)
    sq_lane = lax.broadcasted_iota(I32, (LANES, LANES), 1)
    tile_start = sq_row.astype(F32) * MOE_TILE
    finished = ((offs + padded) <= tile_start) & (sq_lane < N_EXPERTS)
    tile_expert = jnp.minimum(jnp.sum(finished.astype(F32), axis=-1, keepdims=True), N_EXPERTS - 1.0)
    n_tiles = jnp.sum(padded, axis=-1, keepdims=True) / MOE_TILE
    te_ref[...] = jnp.where(sq_lane == 1, n_tiles, tile_expert).astype(I32)


def _plan(ids):
    n = ids.shape[0]
    return pl.pallas_call(
        _plan_kernel,
        out_shape=[jax.ShapeDtypeStruct((n, LANES), I32),
                   jax.ShapeDtypeStruct((LANES, LANES), I32)],
        scratch_shapes=[pltpu.VMEM((n, LANES), F32)],
        compiler_params=pltpu.CompilerParams(vmem_limit_bytes=VMEM_LIMIT),
        name="moe_plan",
    )(ids)


def _dispatch_kernel(te_ref, nt_ref, dest_ref, h_ref, hs_ref, zeros, sem, zsem):
    tq = h_ref.shape[0]

    @pl.when(pl.program_id(0) == 0)
    def _():
        zeros[...] = jnp.zeros_like(zeros)
        n_tiles, nt = te_ref.shape[0], nt_ref[0]

        def has_padding(t):
            return (t >= nt - 1) | (te_ref[jnp.minimum(t + 1, n_tiles - 1)] != te_ref[t])

        def tile_copy(t):
            return pltpu.make_async_copy(zeros, hs_ref.at[pl.ds(pl.multiple_of(t * MOE_TILE, MOE_TILE), MOE_TILE)], zsem)

        def start(t, c):
            @pl.when(has_padding(t))
            def _():
                tile_copy(t).start()
            return c

        def wait(t, c):
            @pl.when(has_padding(t))
            def _():
                tile_copy(t).wait()
            return c

        lax.fori_loop(0, n_tiles, start, 0)
        lax.fori_loop(0, n_tiles, wait, 0)

    def issue(r, c):
        for k in range(MOE_SLOTS):
            slot = dest_ref[0, 0, MOE_SLOTS * r + k]
            pltpu.make_async_copy(h_ref.at[pl.ds(r, 1)], hs_ref.at[pl.ds(slot, 1)], sem).start(priority=k)
        return c

    lax.fori_loop(0, tq, issue, 0, unroll=8)
    done = hs_ref.at[pl.ds(0, MOE_SLOTS * tq)]
    pltpu.make_async_copy(done, done, sem).wait()


def _dispatch(tile_expert, n_tiles, dest3, h2p, n_rows):
    n, w = h2p.shape
    tq = TOKEN_TILE
    return pl.pallas_call(
        _dispatch_kernel,
        grid_spec=pltpu.PrefetchScalarGridSpec(
            num_scalar_prefetch=2,
            grid=(n // tq,),
            in_specs=[pl.BlockSpec((1, 1, MOE_SLOTS * tq), lambda i, te, nt: (i, 0, 0),
                                   memory_space=pltpu.SMEM),
                      pl.BlockSpec((tq, w), lambda i, te, nt: (i, 0))],
            out_specs=pl.BlockSpec(memory_space=pl.ANY),
            scratch_shapes=[pltpu.VMEM((MOE_TILE, w), h2p.dtype),
                            pltpu.SemaphoreType.DMA(()),
                            pltpu.SemaphoreType.DMA(())]),
        out_shape=jax.ShapeDtypeStruct((n_rows, w), h2p.dtype),
        compiler_params=_cparams("arbitrary"),
        name="moe_dispatch",
    )(tile_expert, n_tiles, dest3, h2p)


def _expert_kernel(te_ref, nt_ref, hs_ref, wg_hbm, wu_hbm, wd_hbm, ys_ref,
                   wg_buf, wu_buf, wd_buf, sem, slot_ref):
    t = pl.program_id(0)
    nt = nt_ref[0]
    used = t < nt
    last = te_ref.shape[0] - 1

    def weight_copies(e, slot):
        return (pltpu.make_async_copy(wg_hbm.at[e], wg_buf.at[slot], sem.at[0, slot]),
                pltpu.make_async_copy(wu_hbm.at[e], wu_buf.at[slot], sem.at[1, slot]),
                pltpu.make_async_copy(wd_hbm.at[e], wd_buf.at[slot], sem.at[2, slot]))

    @pl.when(t == 0)
    def _():
        slot_ref[0] = 0
        for c in weight_copies(te_ref[0], 0):
            c.start()

    @pl.when(jnp.logical_not(used))
    def _():
        ys_ref[...] = jnp.zeros_like(ys_ref)

    @pl.when(used)
    def _():
        e = te_ref[t]
        first = (t == 0) | (te_ref[jnp.maximum(t - 1, 0)] != e)

        @pl.when(first & (t > 0))
        def _():
            slot_ref[0] = 1 - slot_ref[0]

        slot = slot_ref[0]

        @pl.when(first)
        def _():
            for c in weight_copies(e, slot):
                c.wait()
            nxt = lax.while_loop(lambda s: (s < nt) & (te_ref[jnp.minimum(s, last)] == e),
                                 lambda s: s + 1, t + 1)

            @pl.when(nxt < nt)
            def _():
                for c in weight_copies(te_ref[jnp.minimum(nxt, last)], 1 - slot):
                    c.start()

        lo, hi = _unpack_bf16_pair(hs_ref[...])
        k = lo.shape[1]

        def up(w_buf):
            return (jnp.dot(lo, w_buf[slot, :k, :].astype(BF16), preferred_element_type=F32)
                    + jnp.dot(hi, w_buf[slot, k:, :].astype(BF16), preferred_element_type=F32))

        a = up(wg_buf)
        u = up(wu_buf)
        act = (a * _sigmoid(a) * u).astype(BF16)
        ys_ref[...] = jnp.dot(act, wd_buf[slot].astype(BF16), preferred_element_type=F32)


def _experts(tile_expert, n_tiles, hs, wg, wu, wd):
    n_rows, hw = hs.shape
    d, f = wg.shape[1], wg.shape[2]
    row = lambda t, te, nt: (jnp.minimum(t, nt[0] - 1), 0)
    hbm = pl.BlockSpec(memory_space=pl.ANY)
    return pl.pallas_call(
        _expert_kernel,
        grid_spec=pltpu.PrefetchScalarGridSpec(
            num_scalar_prefetch=2,
            grid=(n_rows // MOE_TILE,),
            in_specs=[pl.BlockSpec((MOE_TILE, hw), row), hbm, hbm, hbm],
            out_specs=pl.BlockSpec((MOE_TILE, d), lambda t, te, nt: (t, 0)),
            scratch_shapes=[pltpu.VMEM((2, d, f), wg.dtype),
                            pltpu.VMEM((2, d, f), wu.dtype),
                            pltpu.VMEM((2, f, d), wd.dtype),
                            pltpu.SemaphoreType.DMA((3, 2)),
                            pltpu.SMEM((1,), I32)]),
        out_shape=jax.ShapeDtypeStruct((n_rows, d), F32),
        compiler_params=_cparams("arbitrary"),
        name="moe_experts",
    )(tile_expert, n_tiles, hs, wg, wu, wd)


def _combine_kernel(dest_ref, dest_next_ref, ys_ref, x1_ref, cw_ref, gt_ref, o_ref, buf, sem):
    s, n_steps = pl.program_id(0), pl.num_programs(0)
    tq = x1_ref.shape[1]

    def request(d_ref, slot):
        def issue(r, c):
            for k in range(MOE_SLOTS):
                row = d_ref[0, 0, MOE_SLOTS * r + k]
                pltpu.make_async_copy(ys_ref.at[pl.ds(row, 1)], buf.at[slot, k, pl.ds(r, 1)],
                                      sem.at[slot]).start(priority=k)
            return c

        lax.fori_loop(0, tq, issue, 0, unroll=8)

    @pl.when(s == 0)
    def _():
        request(dest_ref, 0)

    @pl.when(s + 1 < n_steps)
    def _():
        request(dest_next_ref, 1 - s % 2)

    slot = s % 2
    pltpu.make_async_copy(buf.at[slot], buf.at[slot], sem.at[slot]).wait()
    cw = cw_ref[0]
    y = cw[:, 0:1] * buf[slot, 0] + cw[:, 1:2] * buf[slot, 1]
    o_ref[0] = x1_ref[0] + gt_ref[0] * y


def _combine(dest3, ys, x1, cw, gt2):
    b, s, d = x1.shape
    tq = TOKEN_TILE
    spb = s // tq
    n_steps = b * spb
    row = lambda t: (t // spb, t % spb, 0)
    idx = lambda t: (t, 0, 0)
    idx_next = lambda t: (jnp.minimum(t + 1, n_steps - 1), 0, 0)
    return pl.pallas_call(
        _combine_kernel,
        grid=(n_steps,),
        in_specs=[pl.BlockSpec((1, 1, MOE_SLOTS * tq), idx, memory_space=pltpu.SMEM),
                  pl.BlockSpec((1, 1, MOE_SLOTS * tq), idx_next, memory_space=pltpu.SMEM),
                  pl.BlockSpec(memory_space=pl.ANY),
                  pl.BlockSpec((1, tq, d), row),
                  pl.BlockSpec((1, tq, LANES), row),
                  pl.BlockSpec((1, 1, d), lambda t: (t // spb, 0, 0))],
        out_specs=pl.BlockSpec((1, tq, d), row),
        out_shape=jax.ShapeDtypeStruct((b, s, d), F32),
        scratch_shapes=[pltpu.VMEM((2, MOE_SLOTS, tq, d), F32), pltpu.SemaphoreType.DMA((2,))],
        compiler_params=_cparams("arbitrary"),
        name="moe_combine",
    )(dest3, dest3, ys, x1, cw, gt2)


def _moe(h2p, ids, cw, wg, wu, wd, x1, gt2):
    b, s, d = x1.shape
    n = b * s
    n_rows = _moe_max_tiles(n) * MOE_TILE
    dest, te = _plan(ids.reshape(n, LANES))
    dest3 = dest[:, :MOE_SLOTS].reshape(n // TOKEN_TILE, 1, MOE_SLOTS * TOKEN_TILE)
    tile_expert, n_tiles = te[:n_rows // MOE_TILE, 0], te[0, 1:2]
    hs = _dispatch(tile_expert, n_tiles, dest3, h2p.reshape(n, d // 2), n_rows)
    ys = _experts(tile_expert, n_tiles, hs, wg, wu, wd)
    return _combine(dest3, ys, x1, cw, gt2)


def _alibi_slopes():
    n = SWA_HEADS + DSA_HEADS
    i = jnp.arange(1, n + 1, dtype=F32)
    return jnp.exp2(-8.0 * i / n)


def _layer(x, mod, ln1_g, w_in, swa_q_norm, swa_k_norm, swa_sinks, dsa_q_norm, dsa_k_norm,
           w_branch_a, w_branch_b, w_out, ln2_g, w_group, b_group, w_expert, b_expert,
           w_gate_up, w_up, w_down, slopes):
    b, s, d = x.shape
    sh1, sc1, gt1, sh2, sc2, gt2 = [m[:, None, :] for m in jnp.split(mod, 6, axis=-1)]

    w_in_t = jnp.swapaxes(w_in, 0, 1)

    h1 = _ln_mod(x, ln1_g[None, :], sc1, sh1)
    h1f = h1.reshape(b * s, d)
    ones = lambda n: jnp.ones((n,), F32)
    mult = jnp.concatenate([
        jnp.tile(swa_q_norm, SWA_HEADS) * HEAD_DIM ** -0.5, jnp.tile(swa_k_norm, SWA_KV_HEADS),
        ones(SWA_KV_HEADS * HEAD_DIM),
        jnp.tile(dsa_q_norm, DSA_HEADS) * HEAD_DIM ** -0.5, dsa_k_norm, ones(HEAD_DIM),
        ones(IDX_HEADS * IDX_DIM) * IDX_DIM ** -0.5, ones(IDX_DIM), ones(IDX_HEADS) * IDX_HEADS ** -0.5,
        ones(ATT_WIDTH - COL_GATE)])[None, :]
    col = jnp.arange(ATT_WIDTH)
    normed = (((col >= COL_QA) & (col < COL_VA)) | ((col >= COL_QB) & (col < COL_KVB + KVB_V))
              ).astype(F32)[None, :]
    proj = _proj(h1f, w_in_t, mult, normed, 1024, 768).reshape(b, s, ATT_WIDTH)
    oa = _swa(proj, swa_sinks, slopes[:SWA_HEADS])
    ob = _dsa(proj, slopes[SWA_HEADS:])
    merged = _merge(h1f, oa.reshape(b * s, -1), ob.reshape(b * s, -1), w_in_t,
                    w_branch_a, w_branch_b)

    w_r = jnp.concatenate([w_expert, w_group,
                           jnp.zeros((d, LANES - N_EXPERTS - N_GROUPS), F32)], axis=1)
    b_r = jnp.concatenate([b_expert, b_group,
                           jnp.zeros((LANES - N_EXPERTS - N_GROUPS,), F32)])[None, :]
    w_r_hi = _bf16_head(w_r)
    w_r_split = jnp.stack([w_r_hi.astype(BF16), (w_r - w_r_hi).astype(BF16)])
    x1, h2p, ids, cw = _outproj(merged.reshape(b, s, d), w_out, x, gt1,
                                ln2_g[None, :], sc2, sh2, w_r_split, b_r)
    f = w_gate_up.shape[-1]
    return _moe(h2p, ids, cw, w_gate_up.reshape(N_EXPERTS, d, f), w_up.reshape(N_EXPERTS, d, f),
                w_down.reshape(N_EXPERTS, f, d), x1, gt2)


def kernel(x, c, w_ada, b_ada, ln1_g, w_in, swa_q_norm, swa_k_norm, swa_sinks, dsa_q_norm, dsa_k_norm,
           w_branch_a, w_branch_b, w_out, ln2_g, w_group, b_group, w_expert, b_expert,
           w_gate_up, w_up, w_down):
    slopes = _alibi_slopes()
    bsz = c.shape[0]
    c_pad = jnp.concatenate([c, jnp.zeros((ADA_ROWS - bsz, c.shape[1]), c.dtype)], axis=0)
    for l in range(w_ada.shape[0]):
        mod = _ada(c_pad, w_ada[l], b_ada[l][None, :])[:bsz]
        x = _layer(x, mod, ln1_g[l], w_in[l], swa_q_norm[l], swa_k_norm[l], swa_sinks[l],
                   dsa_q_norm[l], dsa_k_norm[l], w_branch_a[l], w_branch_b[l], w_out[l], ln2_g[l],
                   w_group[l], b_group[l], w_expert[l], b_expert[l],
                   w_gate_up[l], w_up[l], w_down[l], slopes)
    return x
```

```python
import functools

import jax
import jax.numpy as jnp
from jax import lax
from jax.experimental import pallas as pl
from jax.experimental.pallas import tpu as pltpu

F32 = jnp.float32
BF16 = jnp.bfloat16
I32 = jnp.int32

D_MODEL = 2048
HEAD_DIM = 64
SWA_HEADS = 16
SWA_KV_HEADS = 4
SWA_GROUP = SWA_HEADS // SWA_KV_HEADS
WINDOW = 128
DSA_HEADS = 16
IDX_HEADS = 16
IDX_DIM = 64
DSA_TOPK_MAX = 256
DSA_BLOCK = 128
N_GROUPS = 4
EXPERTS_PER_GROUP = 8
N_EXPERTS = N_GROUPS * EXPERTS_PER_GROUP
D_EXPERT = 512
RMS_EPS = 1e-6
NEG_INF = -1e30
INT_MIN = -(2 ** 31)

LANES = 128
VMEM_LIMIT = 56 * 1024 * 1024

COL_QA, COL_KA, COL_VA, COL_QB = 0, 1024, 1280, 1536
COL_KVB = 2560
COL_QI = 2688
COL_KWI = 3712
COL_GATE = 3792
ATT_WIDTH = 3840
KVB_K, KVB_V = 0, HEAD_DIM
KWI_K, KWI_W = 0, IDX_DIM

NT_DIMS = (((1,), (1,)), ((), ()))


def _cparams(*sem):
    return pltpu.CompilerParams(dimension_semantics=sem, vmem_limit_bytes=VMEM_LIMIT)


def _sigmoid(x):
    return 1.0 / (1.0 + jnp.exp(-x))


def _rms(x, g):
    return x * lax.rsqrt(jnp.mean(x * x, axis=-1, keepdims=True) + RMS_EPS) * g


def _ada_kernel(c_ref, w_ref, b_ref, o_ref):
    c = c_ref[...]
    cond = c * _sigmoid(c)
    c_hi = cond.astype(BF16)
    c_lo = (cond - c_hi.astype(F32)).astype(BF16)
    w = w_ref[...]
    w_hi = w.astype(BF16)
    w_lo = (w - w_hi.astype(F32)).astype(BF16)
    o_ref[...] = (jnp.dot(c_hi, w_hi, preferred_element_type=F32)
                  + jnp.dot(c_lo, w_hi, preferred_element_type=F32)
                  + jnp.dot(c_hi, w_lo, preferred_element_type=F32)) + b_ref[...]


ADA_ROWS = 16


def _ada(c_pad, w, b):
    n = w.shape[1]
    tn = 1024
    return pl.pallas_call(
        _ada_kernel,
        grid=(n // tn,),
        in_specs=[pl.BlockSpec((ADA_ROWS, D_MODEL), lambda j: (0, 0)),
                  pl.BlockSpec((D_MODEL, tn), lambda j: (0, j)),
                  pl.BlockSpec((1, tn), lambda j: (0, j))],
        out_specs=pl.BlockSpec((ADA_ROWS, tn), lambda j: (0, j)),
        out_shape=jax.ShapeDtypeStruct((ADA_ROWS, n), F32),
        compiler_params=_cparams("parallel"),
        name="ada_mod",
    )(c_pad, w, b)


def _ln_mod_kernel(x_ref, g_ref, sc_ref, sh_ref, o_ref):
    y = _rms(x_ref[0], g_ref[...])
    o_ref[0] = (y * (1.0 + sc_ref[0]) + sh_ref[0]).astype(o_ref.dtype)


def _ln_mod(x, g, sc, sh):
    b, s, d = x.shape
    ts = 512
    return pl.pallas_call(
        _ln_mod_kernel,
        grid=(b, s // ts),
        in_specs=[pl.BlockSpec((1, ts, d), lambda i, j: (i, j, 0)),
                  pl.BlockSpec((1, d), lambda i, j: (0, 0)),
                  pl.BlockSpec((1, 1, d), lambda i, j: (i, 0, 0)),
                  pl.BlockSpec((1, 1, d), lambda i, j: (i, 0, 0))],
        out_specs=pl.BlockSpec((1, ts, d), lambda i, j: (i, j, 0)),
        out_shape=jax.ShapeDtypeStruct((b, s, d), BF16),
        compiler_params=_cparams("parallel", "parallel"),
        name="ln_mod",
    )(x, g, sc, sh)


PROJ_ROW_PARTS = 4


def _proj_kernel(a_ref, wt_ref, mult_ref, normed_ref, o_ref):
    w = wt_ref[...].astype(BF16)
    tm, tn = o_ref.shape
    rows_per_part = tm // PROJ_ROW_PARTS
    half = lax.broadcasted_iota(I32, (rows_per_part, LANES), 1) < HEAD_DIM
    for part in range(PROJ_ROW_PARTS):
        rows = slice(part * rows_per_part, (part + 1) * rows_per_part)
        y = lax.dot_general(a_ref[rows, :], w, NT_DIMS, preferred_element_type=F32)
        for g in range(tn // LANES):
            cols = slice(g * LANES, (g + 1) * LANES)
            yg = y[:, cols]
            sq = yg * yg
            ms_lo = jnp.sum(jnp.where(half, sq, 0.0), axis=-1, keepdims=True) / HEAD_DIM
            ms_hi = jnp.sum(jnp.where(half, 0.0, sq), axis=-1, keepdims=True) / HEAD_DIM
            r = lax.rsqrt(jnp.where(half, ms_lo, ms_hi) + RMS_EPS)
            o_ref[rows, cols] = yg * jnp.where(normed_ref[:, cols] > 0.0, r, 1.0) * mult_ref[:, cols]


def _proj(a, wt, mult, normed, tm, tn):
    m, k = a.shape
    n = mult.shape[1]
    vec = pl.BlockSpec((1, tn), lambda i, j: (0, j))
    return pl.pallas_call(
        _proj_kernel,
        grid=(m // tm, n // tn),
        in_specs=[pl.BlockSpec((tm, k), lambda i, j: (i, 0)),
                  pl.BlockSpec((tn, k), lambda i, j: (j, 0)),
                  vec, vec],
        out_specs=pl.BlockSpec((tm, tn), lambda i, j: (i, j)),
        out_shape=jax.ShapeDtypeStruct((m, n), F32),
        compiler_params=_cparams("parallel", "parallel"),
        name="proj_att",
    )(a, wt, mult, normed)


def _swa_kernel(sink_ref, slope_ref, q_ref, kp_ref, kc_ref, vp_ref, vc_ref, o_ref, lhs, e_s):
    n = pl.program_id(1)
    w, w2 = WINDOW, 2 * WINDOW
    q = q_ref[0]
    kw = jnp.concatenate([kp_ref[0], kc_ref[0]], axis=0)
    vw = jnp.concatenate([vp_ref[0], vc_ref[0]], axis=0)
    half_q = lax.broadcasted_iota(I32, (w, LANES), 1) < HEAD_DIM
    half_k = lax.broadcasted_iota(I32, (w2, LANES), 1) < HEAD_DIM
    row = lax.broadcasted_iota(I32, (w, w2), 0)
    col = lax.broadcasted_iota(I32, (w, w2), 1)
    rel = row + w - col
    valid = (rel >= 0) & (rel < w) & ((col >= w) | (n > 0))
    mbias = jnp.where(valid, 0.0, NEG_INF)
    relf = rel.astype(F32)

    ones_col = jnp.ones((w2, LANES), BF16)
    for g in range(SWA_KV_HEADS):
        if g % 2 == 0:
            grp = slice((g // 2) * LANES, (g // 2 + 1) * LANES)
            kn = kw[:, grp]
            kn_sw = pltpu.roll(kn, HEAD_DIM, axis=1)
            vp = vw[:, grp]
            vp_sw = pltpu.roll(vp, HEAD_DIM, axis=1)
        kdup = jnp.where(half_k, kn, kn_sw) if g % 2 == 0 else jnp.where(half_k, kn_sw, kn)
        vdup = jnp.where(half_k, vp, vp_sw) if g % 2 == 0 else jnp.where(half_k, vp_sw, vp)
        vx = jnp.concatenate([vdup.astype(BF16), ones_col], axis=1)
        for r in range(SWA_GROUP):
            h = g * SWA_GROUP + r
            if r % 2 == 0:
                qn = q[:, (h // 2) * LANES:(h // 2 + 1) * LANES]
            keep = half_q if r % 2 == 0 else jnp.logical_not(half_q)
            lhs[r * w:(r + 1) * w, :] = jnp.where(keep, qn, 0.0).astype(BF16)
        s_all = lax.dot_general(lhs[...], kdup.astype(BF16), NT_DIMS, preferred_element_type=F32)
        sink_e = []
        for r in range(SWA_GROUP):
            h = g * SWA_GROUP + r
            rows = slice(r * w, (r + 1) * w)
            s = s_all[rows, :] + (mbias - slope_ref[h] * relf)
            sink = jnp.full((w, LANES), sink_ref[h], F32)
            m = jnp.maximum(jnp.max(s, axis=-1, keepdims=True), sink)
            e_s[rows, :] = jnp.exp(s - jnp.concatenate([m, m], axis=1)).astype(BF16)
            sink_e.append(jnp.exp(sink - m))
        pv = jnp.dot(e_s[...], vx, preferred_element_type=F32)
        for r in range(0, SWA_GROUP, 2):
            h = g * SWA_GROUP + r
            ev, od = pv[r * w:(r + 1) * w, :], pv[(r + 1) * w:(r + 2) * w, :]
            den = jnp.where(half_q, ev[:, LANES:] + sink_e[r], od[:, LANES:] + sink_e[r + 1])
            o = jnp.where(half_q, ev[:, :LANES], od[:, :LANES]) / den
            o_ref[0, :, (h // 2) * LANES:(h // 2 + 1) * LANES] = o.astype(o_ref.dtype)


def _swa(proj3, sinks, slopes):
    b, s, _ = proj3.shape
    nb = s // WINDOW
    kvw = SWA_KV_HEADS * HEAD_DIM
    qw = SWA_HEADS * HEAD_DIM
    smem = pl.BlockSpec(memory_space=pltpu.SMEM)
    prev =lambda i, n: (i, jnp.maximum(n - 1, 0), COL_KA // kvw)
    cur = lambda i, n: (i, n, COL_KA // kvw)
    prev_v = lambda i, n: (i, jnp.maximum(n - 1, 0), COL_VA // kvw)
    cur_v = lambda i, n: (i, n, COL_VA // kvw)
    return pl.pallas_call(
        _swa_kernel,
        grid=(b, nb),
        in_specs=[smem, smem,
                  pl.BlockSpec((1, WINDOW, qw), lambda i, n: (i, n, COL_QA // qw)),
                  pl.BlockSpec((1, WINDOW, kvw), prev),
                  pl.BlockSpec((1, WINDOW, kvw), cur),
                  pl.BlockSpec((1, WINDOW, kvw), prev_v),
                  pl.BlockSpec((1, WINDOW, kvw), cur_v)],
        out_specs=pl.BlockSpec((1, WINDOW, qw), lambda i, n: (i, n, 0)),
        out_shape=jax.ShapeDtypeStruct((b, s, qw), BF16),
        scratch_shapes=[pltpu.VMEM((SWA_GROUP * WINDOW, LANES), BF16),
                        pltpu.VMEM((SWA_GROUP * WINDOW, 2 * WINDOW), BF16)],
        compiler_params=_cparams("parallel", "parallel"),
        name="swa_attn",
    )(sinks, slopes, proj3, proj3, proj3, proj3, proj3)


DSA_CHUNK = 256
DSA_PART_HEADS = 4
ALIBI_SPLIT = 3
POS_LO_BITS = 7


def _bf16_head(x):
    return lax.bitcast_convert_type(lax.bitcast_convert_type(x, I32) & jnp.int32(-65536), F32)


def _alibi_columns(slopes):
    parts, rest = [], slopes.astype(F32)
    for _ in range(ALIBI_SPLIT):
        piece = _bf16_head(rest)
        parts.append(piece)
        rest = rest - piece
    cols = [p * float(2 ** POS_LO_BITS) for p in parts] + parts
    pad = jnp.zeros((slopes.shape[0], LANES - 2 * ALIBI_SPLIT), F32)
    return jnp.concatenate([jnp.stack(cols, axis=1), pad], axis=1)


def _dsa_kernel(*refs, topk):
    n_pairs = DSA_HEADS // 2
    qb_refs, qi_refs = refs[:n_pairs], refs[n_pairs:2 * n_pairs]
    (kwi_blk_ref, kvb_ref, kwi_ref, acol_ref, o_ref,
     ki2, kx, vx, lhs_i, lhs_q, wb, key_s, keyt_s, s_s, thr_s, acc_s, m_s) = refs[2 * n_pairs:]
    i = pl.program_id(1)
    blk, ck = DSA_BLOCK, DSA_CHUNK
    nch = (i + 2) // 2
    seq = ki2.shape[0]
    max_ch = seq // ck
    half = lax.broadcasted_iota(I32, (blk, LANES), 1) < HEAD_DIM

    @pl.when(i == 0)
    def _():
        kvb = kvb_ref[0]
        kib = kwi_ref[0][:, KWI_K:KWI_K + IDX_DIM].astype(BF16)
        ki2[...] = jnp.concatenate([kib, kib], axis=1)
        kn = kvb[:, KVB_K:KVB_K + HEAD_DIM].astype(BF16)
        pos = lax.broadcasted_iota(I32, (seq, LANES), 0)
        lane = lax.broadcasted_iota(I32, (seq, LANES), 1)
        hi = lax.shift_right_logical(pos, POS_LO_BITS).astype(F32)
        lo = (pos & (2 ** POS_LO_BITS - 1)).astype(F32)
        posc = jnp.where(lane < ALIBI_SPLIT, hi, jnp.where(lane < 2 * ALIBI_SPLIT, lo, 0.0)).astype(BF16)
        kx[...] = jnp.concatenate([kn, kn, posc], axis=1)
        ones = jnp.where(lax.broadcasted_iota(I32, (seq, HEAD_DIM), 1) == 0, 1.0, 0.0).astype(BF16)
        vx[...] = jnp.concatenate([kvb[:, KVB_V:KVB_V + HEAD_DIM].astype(BF16), ones], axis=1)

    wi = kwi_blk_ref[0][:, KWI_W:KWI_W + IDX_HEADS]
    for p in range(n_pairs):
        qn = qb_refs[p][0]
        qip = qi_refs[p][0]
        for k in range(2):
            h = 2 * p + k
            keep = half if k == 0 else jnp.logical_not(half)
            rows = slice(h * blk, (h + 1) * blk)
            lhs_i[rows, :] = jnp.where(keep, qip, 0.0).astype(BF16)
            lhs_q[rows, :LANES] = jnp.where(keep, qn, 0.0).astype(BF16)
            lhs_q[rows, LANES:] = jnp.broadcast_to(acol_ref[h:h + 1, :], (blk, LANES)).astype(BF16)
            wb[h] = jnp.broadcast_to(wi[:, h:h + 1], (blk, LANES))

    row_t = i * blk + lax.broadcasted_iota(I32, (blk, ck), 0)
    col_k = lax.broadcasted_iota(I32, (blk, ck), 1)

    def score_chunk(c, carry):
        ks = pl.ds(pl.multiple_of(c * ck, ck), ck)
        kic = ki2[ks, :]
        sc = jnp.zeros((blk, ck), F32)
        for part in range(0, IDX_HEADS, DSA_PART_HEADS):
            logits = lax.dot_general(lhs_i[part * blk:(part + DSA_PART_HEADS) * blk, :], kic, NT_DIMS,
                                     preferred_element_type=F32)
            for k in range(DSA_PART_HEADS):
                w = wb[part + k]
                lh = jnp.maximum(logits[k * blk:(k + 1) * blk, :], 0.0)
                sc = sc + lh * jnp.concatenate([w, w], axis=1)
        bits = lax.bitcast_convert_type(sc + 0.0, I32)
        key = jnp.where(bits < 0, bits ^ jnp.int32(0x7FFFFFFF), bits)
        key = jnp.where(c * ck + col_k <= row_t, key, jnp.int32(INT_MIN))
        key_s[c] = key
        keyt_s[c] = key.T
        return carry

    lax.fori_loop(0, nch, score_chunk, 0)

    thr_s[...] = jnp.full(thr_s.shape, INT_MIN, I32)
    for n in range(2, max_ch + 1):
        @pl.when(nch == n)
        def _(n=n):
            def step(it, thr):
                cand = thr + lax.shift_left(jnp.int32(1), 31 - it)
                cnt = jnp.zeros((1, blk), F32)
                for c in range(n):
                    cnt = cnt + jnp.sum(jnp.where(keyt_s[c] >= cand, 1.0, 0.0), axis=0, keepdims=True)
                return jnp.where(cnt >= float(topk), cand, thr)

            thr = lax.fori_loop(0, 32, step, jnp.full((1, blk), INT_MIN, I32))
            thr_s[...] = jnp.broadcast_to(thr, thr_s.shape).T

    thr = thr_s[...]
    thr2 = jnp.concatenate([thr, thr], axis=1)

    def count_gt(c, cnt):
        k = key_s[c]
        return cnt + jnp.where(k[:, :LANES] > thr, 1.0, 0.0) + jnp.where(k[:, LANES:] > thr, 1.0, 0.0)

    n_gt = jnp.sum(lax.fori_loop(0, nch, count_gt, jnp.zeros((blk, LANES), F32)), axis=-1, keepdims=True)
    need = float(topk) - n_gt
    upper = (lax.broadcasted_iota(I32, (ck, ck), 0) < lax.broadcasted_iota(I32, (ck, ck), 1)).astype(BF16)

    m_s[...] = jnp.full(m_s.shape, -jnp.inf, F32)

    def masked_scores(c, n_eq):
        ks = pl.ds(pl.multiple_of(c * ck, ck), ck)
        k = key_s[c]
        eq = (k == thr2) & (c * ck + col_k <= row_t)
        eqf = eq.astype(BF16)
        before = jnp.dot(eqf, upper, preferred_element_type=F32) + n_eq
        sel = (k > thr2) | (eq & (before < need))
        mb = jnp.where(sel, 0.0, NEG_INF)
        kxc = kx[ks, :]
        for part in range(0, DSA_HEADS, DSA_PART_HEADS):
            s_part = lax.dot_general(lhs_q[part * blk:(part + DSA_PART_HEADS) * blk, :], kxc, NT_DIMS,
                                     preferred_element_type=F32)
            for k in range(DSA_PART_HEADS):
                rows = slice((part + k) * blk, (part + k + 1) * blk)
                s = s_part[k * blk:(k + 1) * blk, :] + mb
                s_s[c, rows, :] = s
                m_s[rows, :] = jnp.maximum(m_s[rows, :], jnp.maximum(s[:, :LANES], s[:, LANES:]))
        return n_eq + jnp.sum(eqf.astype(F32), axis=-1, keepdims=True)

    lax.fori_loop(0, nch, masked_scores, jnp.zeros((blk, 1), F32))

    m_s[...] = jnp.broadcast_to(jnp.max(m_s[...], axis=-1, keepdims=True), m_s.shape)
    acc_s[...] = jnp.zeros_like(acc_s)

    def attend(c, carry):
        vxc = vx[pl.ds(pl.multiple_of(c * ck, ck), ck), :]
        for part in range(0, DSA_HEADS, DSA_PART_HEADS):
            e = []
            for k in range(DSA_PART_HEADS):
                rows = slice((part + k) * blk, (part + k + 1) * blk)
                m = m_s[rows, :]
                e.append(jnp.exp(s_s[c, rows, :] - jnp.concatenate([m, m], axis=1)).astype(BF16))
            rows = slice(part * blk, (part + DSA_PART_HEADS) * blk)
            acc_s[rows, :] += jnp.dot(jnp.concatenate(e, axis=0), vxc, preferred_element_type=F32)
        return carry

    lax.fori_loop(0, nch, attend, 0)

    for h in range(DSA_HEADS):
        a = acc_s[h * blk:(h + 1) * blk, :]
        o = a[:, :HEAD_DIM] / a[:, HEAD_DIM:HEAD_DIM + 1]
        o_ref[0, :, h * HEAD_DIM:(h + 1) * HEAD_DIM] = o.astype(o_ref.dtype)


def _dsa(proj3, slopes):
    b, s, _ = proj3.shape
    nb = s // DSA_BLOCK
    qw = DSA_HEADS * HEAD_DIM
    n_pairs = DSA_HEADS // 2
    rows = DSA_HEADS * DSA_BLOCK
    fixed = lambda i, n: (0, 0)

    def group(col):
        return pl.BlockSpec((1, DSA_BLOCK, LANES), lambda i, n: (i, n, col // LANES))

    def group_all(col):
        return pl.BlockSpec((1, s, LANES), lambda i, n: (i, 0, col // LANES))

    return pl.pallas_call(
        functools.partial(_dsa_kernel, topk=min(DSA_TOPK_MAX, s // 4)),
        grid=(b, nb),
        in_specs=([group(COL_QB + p * LANES) for p in range(n_pairs)]
                  + [group(COL_QI + p * LANES) for p in range(n_pairs)]
                  + [group(COL_KWI), group_all(COL_KVB), group_all(COL_KWI),
                     pl.BlockSpec((DSA_HEADS, LANES), fixed)]),
        out_specs=pl.BlockSpec((1, DSA_BLOCK, qw), lambda i, n: (i, n, 0)),
        out_shape=jax.ShapeDtypeStruct((b, s, qw), BF16),
        scratch_shapes=[pltpu.VMEM((s, LANES), BF16),
                        pltpu.VMEM((s, 2 * LANES), BF16),
                        pltpu.VMEM((s, LANES), BF16),
                        pltpu.VMEM((rows, LANES), BF16),
                        pltpu.VMEM((rows, 2 * LANES), BF16),
                        pltpu.VMEM((IDX_HEADS, DSA_BLOCK, LANES), F32),
                        pltpu.VMEM((s // DSA_CHUNK, DSA_BLOCK, DSA_CHUNK), I32),
                        pltpu.VMEM((s // DSA_CHUNK, DSA_CHUNK, DSA_BLOCK), I32),
                        pltpu.VMEM((s // DSA_CHUNK, rows, DSA_CHUNK), F32),
                        pltpu.VMEM((DSA_BLOCK, LANES), I32),
                        pltpu.VMEM((rows, LANES), F32),
                        pltpu.VMEM((rows, LANES), F32)],
        compiler_params=_cparams("arbitrary", "arbitrary"),
        name="dsa_attn",
    )(*([proj3] * (2 * n_pairs + 3)), _alibi_columns(slopes))


def _merge_kernel(h_ref, oa_ref, ob_ref, wt_hbm, wa_ref, wb_ref, o_ref, g32, gbf, sem):
    j, i = pl.program_id(0), pl.program_id(1)
    tn, d = o_ref.shape[1], h_ref.shape[1]

    def gate_copies(jj, slot):
        return [pltpu.make_async_copy(
            wt_hbm.at[pl.ds(pl.multiple_of(COL_GATE + k * d + jj * tn, 8), tn)], g32.at[slot, k], sem.at[slot, k])
            for k in range(2)]

    @pl.when((j == 0) & (i == 0))
    def _():
        for c in gate_copies(0, 0):
            c.start()

    @pl.when(i == 0)
    def _():
        slot = j % 2
        for c in gate_copies(j, slot):
            c.wait()
        gbf[...] = g32[slot].astype(BF16)

        @pl.when(j + 1 < pl.num_programs(0))
        def _():
            for c in gate_copies(j + 1, 1 - slot):
                c.start()

    h = h_ref[...]
    ga = lax.dot_general(h, gbf[0], NT_DIMS, preferred_element_type=F32)
    gb = lax.dot_general(h, gbf[1], NT_DIMS, preferred_element_type=F32)
    a = jnp.dot(oa_ref[...], wa_ref[...].astype(BF16), preferred_element_type=F32)
    b = jnp.dot(ob_ref[...], wb_ref[...].astype(BF16), preferred_element_type=F32)
    o_ref[...] = (_sigmoid(ga) * a + _sigmoid(gb) * b).astype(o_ref.dtype)


def _merge(h, oa, ob, w_in_t, wa, wb):
    m, d = h.shape
    tm, tn = 1024, 512
    return pl.pallas_call(
        _merge_kernel,
        grid=(d // tn, m // tm),
        in_specs=[pl.BlockSpec((tm, d), lambda j, i: (i, 0)),
                  pl.BlockSpec((tm, oa.shape[1]), lambda j, i: (i, 0)),
                  pl.BlockSpec((tm, ob.shape[1]), lambda j, i: (i, 0)),
                  pl.BlockSpec(memory_space=pl.ANY),
                  pl.BlockSpec((wa.shape[0], tn), lambda j, i: (0, j)),
                  pl.BlockSpec((wb.shape[0], tn), lambda j, i: (0, j))],
        out_specs=pl.BlockSpec((tm, tn), lambda j, i: (i, j)),
        out_shape=jax.ShapeDtypeStruct((m, d), BF16),
        scratch_shapes=[pltpu.VMEM((2, 2, tn, d), w_in_t.dtype),
                        pltpu.VMEM((2, tn, d), BF16),
                        pltpu.SemaphoreType.DMA((2, 2))],
        compiler_params=_cparams("arbitrary", "arbitrary"),
        name="merge_branches",
    )(h, oa, ob, w_in_t, wa, wb)


MOE_TILE = 256
MOE_SLOTS = 2
TOKEN_TILE = 512


def _moe_max_tiles(n_tokens):
    return -(-(n_tokens * MOE_SLOTS + N_EXPERTS * (MOE_TILE - 1)) // MOE_TILE)


def _pack_bf16_pair(x):
    k = x.shape[1] // 2
    bits = lax.bitcast_convert_type(x.astype(BF16).astype(F32), I32)
    return lax.shift_right_logical(bits[:, :k], 16) | (bits[:, k:] & jnp.int32(-65536))


def _unpack_bf16_pair(w):
    lo = lax.bitcast_convert_type(lax.shift_left(w, 16), F32).astype(BF16)
    hi = lax.bitcast_convert_type(w & jnp.int32(-65536), F32).astype(BF16)
    return lo, hi


def _route(logits):
    lane = lax.broadcasted_iota(I32, logits.shape, 1)
    is_g = (lane >= N_EXPERTS) & (lane < N_EXPERTS + N_GROUPS)
    neg = -jnp.inf
    gl = jnp.where(is_g, logits, neg)
    gmax = jnp.max(gl, axis=-1, keepdims=True)
    g_idx = jnp.min(jnp.where(gl == gmax, lane - N_EXPERTS, N_GROUPS), axis=-1, keepdims=True)
    p_g = 1.0 / jnp.sum(jnp.where(is_g, jnp.exp(gl - gmax), 0.0), axis=-1, keepdims=True)
    in_grp = (lane < N_EXPERTS) & ((lane // EXPERTS_PER_GROUP) == g_idx)
    el = jnp.where(in_grp, logits, neg)
    m1 = jnp.max(el, axis=-1, keepdims=True)
    i1 = jnp.min(jnp.where(el == m1, lane, LANES), axis=-1, keepdims=True)
    el2 = jnp.where(lane == i1, neg, el)
    m2 = jnp.max(el2, axis=-1, keepdims=True)
    i2 = jnp.min(jnp.where(el2 == m2, lane, LANES), axis=-1, keepdims=True)
    e2 = jnp.exp(m2 - m1)
    w1 = 1.0 / (1.0 + e2)
    w2 = e2 / (1.0 + e2)
    ids = jnp.where(lane == 0, i1, jnp.where(lane == 1, i2, 0))
    return ids, p_g * jnp.where(lane == 0, w1, jnp.where(lane == 1, w2, 0.0))


def _outproj_kernel(mg_ref, wo_ref, x_ref, gt_ref, g2_ref, sc_ref, sh_ref, wr_ref, br_ref,
                    x1_ref, h2p_ref, ids_ref, cw_ref):
    y = jnp.dot(mg_ref[0], wo_ref[...].astype(BF16), preferred_element_type=F32)
    x1 = x_ref[0] + gt_ref[0] * y
    x1_ref[0] = x1
    h2 = _rms(x1, g2_ref[...]) * (1.0 + sc_ref[0]) + sh_ref[0]
    h2p_ref[0] = _pack_bf16_pair(h2)
    h_hi = h2.astype(BF16)
    h_lo = (h2 - h_hi.astype(F32)).astype(BF16)
    logits = (jnp.dot(h_hi, wr_ref[0], preferred_element_type=F32)
              + jnp.dot(h_lo, wr_ref[0], preferred_element_type=F32)
              + jnp.dot(h_hi, wr_ref[1], preferred_element_type=F32)) + br_ref[...]
    ids_ref[0], cw_ref[0] = _route(logits)


def _outproj(merged3, w_out, x, gt1, g2, sc2, sh2, w_r, b_r):
    b, s, d = x.shape
    tm = 512
    row = lambda i, j: (i, j, 0)
    per_b = lambda i, j: (i, 0, 0)
    fixed = lambda i, j: (0, 0)
    return pl.pallas_call(
        _outproj_kernel,
        grid=(b, s // tm),
        in_specs=[pl.BlockSpec((1, tm, d), row),
                  pl.BlockSpec((d, d), fixed, pipeline_mode=pl.Buffered(1)),
                  pl.BlockSpec((1, tm, d), row),
                  pl.BlockSpec((1, 1, d), per_b),
                  pl.BlockSpec((1, d), fixed),
                  pl.BlockSpec((1, 1, d), per_b),
                  pl.BlockSpec((1, 1, d), per_b),
                  pl.BlockSpec((2, d, LANES), lambda i, j: (0, 0, 0)),
                  pl.BlockSpec((1, LANES), fixed)],
        out_specs=[pl.BlockSpec((1, tm, d), row),
                   pl.BlockSpec((1, tm, d // 2), row),
                   pl.BlockSpec((1, tm, LANES), row),
                   pl.BlockSpec((1, tm, LANES), row)],
        out_shape=[jax.ShapeDtypeStruct((b, s, d), F32),
                   jax.ShapeDtypeStruct((b, s, d // 2), I32),
                   jax.ShapeDtypeStruct((b, s, LANES), I32),
                   jax.ShapeDtypeStruct((b, s, LANES), F32)],
        compiler_params=_cparams("parallel", "parallel"),
        name="outproj_ln2_router",
    )(merged3, w_out, x, gt1, g2, sc2, sh2, w_r, b_r)


def _plan_kernel(ids_ref, dest_ref, te_ref, rank_ref):
    n = ids_ref.shape[0]
    cb = 256
    lane = lax.broadcasted_iota(I32, (cb, LANES), 1)
    lower = (lax.broadcasted_iota(I32, (cb, cb), 0) > lax.broadcasted_iota(I32, (cb, cb), 1)).astype(BF16)

    def count(bi, carry):
        rows = pl.ds(pl.multiple_of(bi * cb, cb), cb)
        ids = ids_ref[rows, :]
        m1 = lane == ids[:, 0:1]
        m2 = lane == ids[:, 1:2]
        onehot = m1 | m2
        before = jnp.dot(lower, onehot.astype(BF16), preferred_element_type=F32) + carry
        r1 = jnp.sum(jnp.where(m1, before, 0.0), axis=-1, keepdims=True)
        r2 = jnp.sum(jnp.where(m2, before, 0.0), axis=-1, keepdims=True)
        rank_ref[rows, :] = jnp.where(lane == 0, r1, jnp.where(lane == 1, r2, 0.0))
        return carry + jnp.sum(onehot.astype(F32), axis=0, keepdims=True)

    counts = lax.fori_loop(0, n // cb, count, jnp.zeros((1, LANES), F32))
    padded = jnp.floor((counts + (MOE_TILE - 1)) / MOE_TILE) * MOE_TILE
    upper = (lax.broadcasted_iota(I32, (LANES, LANES), 0)
             < lax.broadcasted_iota(I32, (LANES, LANES), 1)).astype(F32)
    offs = jnp.dot(jnp.broadcast_to(padded, (8, LANES)), upper, preferred_element_type=F32,
                   precision=lax.Precision.HIGHEST)[0:1]

    def place(bi, c):
        rows = pl.ds(pl.multiple_of(bi * cb, cb), cb)
        ids = ids_ref[rows, :]
        o1 = jnp.sum(jnp.where(lane == ids[:, 0:1], offs, 0.0), axis=-1, keepdims=True)
        o2 = jnp.sum(jnp.where(lane == ids[:, 1:2], offs, 0.0), axis=-1, keepdims=True)
        slot = rank_ref[rows, :] + jnp.where(lane == 0, o1, jnp.where(lane == 1, o2, 0.0))
        dest_ref[rows, :] = slot.astype(I32)
        return c

    lax.fori_loop(0, n // cb, place, 0)

    sq_row = lax.broadcasted_iota(I32, (LANES, LANES), 0)
    sq_lane = lax.broadcasted_iota(I32, (LANES, LANES), 1)
    tile_start = sq_row.astype(F32) * MOE_TILE
    finished = ((offs + padded) <= tile_start) & (sq_lane < N_EXPERTS)
    tile_expert = jnp.minimum(jnp.sum(finished.astype(F32), axis=-1, keepdims=True), N_EXPERTS - 1.0)
    n_tiles = jnp.sum(padded, axis=-1, keepdims=True) / MOE_TILE
    te_ref[...] = jnp.where(sq_lane == 1, n_tiles, tile_expert).astype(I32)


def _plan(ids):
    n = ids.shape[0]
    return pl.pallas_call(
        _plan_kernel,
        out_shape=[jax.ShapeDtypeStruct((n, LANES), I32),
                   jax.ShapeDtypeStruct((LANES, LANES), I32)],
        scratch_shapes=[pltpu.VMEM((n, LANES), F32)],
        compiler_params=pltpu.CompilerParams(vmem_limit_bytes=VMEM_LIMIT),
        name="moe_plan",
    )(ids)


def _dispatch_kernel(te_ref, nt_ref, dest_ref, h_ref, hs_ref, zeros, sem, zsem):
    tq = h_ref.shape[0]

    @pl.when(pl.program_id(0) == 0)
    def _():
        zeros[...] = jnp.zeros_like(zeros)
        n_tiles, nt = te_ref.shape[0], nt_ref[0]

        def has_padding(t):
            return (t >= nt - 1) | (te_ref[jnp.minimum(t + 1, n_tiles - 1)] != te_ref[t])

        def tile_copy(t):
            return pltpu.make_async_copy(zeros, hs_ref.at[pl.ds(pl.multiple_of(t * MOE_TILE, MOE_TILE), MOE_TILE)], zsem)

        def start(t, c):
            @pl.when(has_padding(t))
            def _():
                tile_copy(t).start()
            return c

        def wait(t, c):
            @pl.when(has_padding(t))
            def _():
                tile_copy(t).wait()
            return c

        lax.fori_loop(0, n_tiles, start, 0)
        lax.fori_loop(0, n_tiles, wait, 0)

    def issue(r, c):
        for k in range(MOE_SLOTS):
            slot = dest_ref[0, 0, MOE_SLOTS * r + k]
            pltpu.make_async_copy(h_ref.at[pl.ds(r, 1)], hs_ref.at[pl.ds(slot, 1)], sem).start(priority=k)
        return c

    lax.fori_loop(0, tq, issue, 0, unroll=8)
    done = hs_ref.at[pl.ds(0, MOE_SLOTS * tq)]
    pltpu.make_async_copy(done, done, sem).wait()


def _dispatch(tile_expert, n_tiles, dest3, h2p, n_rows):
    n, w = h2p.shape
    tq = TOKEN_TILE
    return pl.pallas_call(
        _dispatch_kernel,
        grid_spec=pltpu.PrefetchScalarGridSpec(
            num_scalar_prefetch=2,
            grid=(n // tq,),
            in_specs=[pl.BlockSpec((1, 1, MOE_SLOTS * tq), lambda i, te, nt: (i, 0, 0),
                                   memory_space=pltpu.SMEM),
                      pl.BlockSpec((tq, w), lambda i, te, nt: (i, 0))],
            out_specs=pl.BlockSpec(memory_space=pl.ANY),
            scratch_shapes=[pltpu.VMEM((MOE_TILE, w), h2p.dtype),
                            pltpu.SemaphoreType.DMA(()),
                            pltpu.SemaphoreType.DMA(())]),
        out_shape=jax.ShapeDtypeStruct((n_rows, w), h2p.dtype),
        compiler_params=_cparams("arbitrary"),
        name="moe_dispatch",
    )(tile_expert, n_tiles, dest3, h2p)


def _expert_kernel(te_ref, nt_ref, hs_ref, wg_hbm, wu_hbm, wd_hbm, ys_ref,
                   wg_buf, wu_buf, wd_buf, sem, slot_ref):
    t = pl.program_id(0)
    nt = nt_ref[0]
    used = t < nt
    last = te_ref.shape[0] - 1

    def weight_copies(e, slot):
        return (pltpu.make_async_copy(wg_hbm.at[e], wg_buf.at[slot], sem.at[0, slot]),
                pltpu.make_async_copy(wu_hbm.at[e], wu_buf.at[slot], sem.at[1, slot]),
                pltpu.make_async_copy(wd_hbm.at[e], wd_buf.at[slot], sem.at[2, slot]))

    @pl.when(t == 0)
    def _():
        slot_ref[0] = 0
        for c in weight_copies(te_ref[0], 0):
            c.start()

    @pl.when(jnp.logical_not(used))
    def _():
        ys_ref[...] = jnp.zeros_like(ys_ref)

    @pl.when(used)
    def _():
        e = te_ref[t]
        first = (t == 0) | (te_ref[jnp.maximum(t - 1, 0)] != e)

        @pl.when(first & (t > 0))
        def _():
            slot_ref[0] = 1 - slot_ref[0]

        slot = slot_ref[0]

        @pl.when(first)
        def _():
            for c in weight_copies(e, slot):
                c.wait()
            nxt = lax.while_loop(lambda s: (s < nt) & (te_ref[jnp.minimum(s, last)] == e),
                                 lambda s: s + 1, t + 1)

            @pl.when(nxt < nt)
            def _():
                for c in weight_copies(te_ref[jnp.minimum(nxt, last)], 1 - slot):
                    c.start()

        lo, hi = _unpack_bf16_pair(hs_ref[...])
        k = lo.shape[1]

        def up(w_buf):
            return (jnp.dot(lo, w_buf[slot, :k, :].astype(BF16), preferred_element_type=F32)
                    + jnp.dot(hi, w_buf[slot, k:, :].astype(BF16), preferred_element_type=F32))

        a = up(wg_buf)
        u = up(wu_buf)
        act = (a * _sigmoid(a) * u).astype(BF16)
        ys_ref[...] = jnp.dot(act, wd_buf[slot].astype(BF16), preferred_element_type=F32)


def _experts(tile_expert, n_tiles, hs, wg, wu, wd):
    n_rows, hw = hs.shape
    d, f = wg.shape[1], wg.shape[2]
    row = lambda t, te, nt: (jnp.minimum(t, nt[0] - 1), 0)
    hbm = pl.BlockSpec(memory_space=pl.ANY)
    return pl.pallas_call(
        _expert_kernel,
        grid_spec=pltpu.PrefetchScalarGridSpec(
            num_scalar_prefetch=2,
            grid=(n_rows // MOE_TILE,),
            in_specs=[pl.BlockSpec((MOE_TILE, hw), row), hbm, hbm, hbm],
            out_specs=pl.BlockSpec((MOE_TILE, d), lambda t, te, nt: (t, 0)),
            scratch_shapes=[pltpu.VMEM((2, d, f), wg.dtype),
                            pltpu.VMEM((2, d, f), wu.dtype),
                            pltpu.VMEM((2, f, d), wd.dtype),
                            pltpu.SemaphoreType.DMA((3, 2)),
                            pltpu.SMEM((1,), I32)]),
        out_shape=jax.ShapeDtypeStruct((n_rows, d), F32),
        compiler_params=_cparams("arbitrary"),
        name="moe_experts",
    )(tile_expert, n_tiles, hs, wg, wu, wd)


def _combine_kernel(dest_ref, dest_next_ref, ys_ref, x1_ref, cw_ref, gt_ref, o_ref, buf, sem):
    s, n_steps = pl.program_id(0), pl.num_programs(0)
    tq = x1_ref.shape[1]

    def request(d_ref, slot):
        def issue(r, c):
            for k in range(MOE_SLOTS):
                row = d_ref[0, 0, MOE_SLOTS * r + k]
                pltpu.make_async_copy(ys_ref.at[pl.ds(row, 1)], buf.at[slot, k, pl.ds(r, 1)],
                                      sem.at[slot]).start(priority=k)
            return c

        lax.fori_loop(0, tq, issue, 0, unroll=8)

    @pl.when(s == 0)
    def _():
        request(dest_ref, 0)

    @pl.when(s + 1 < n_steps)
    def _():
        request(dest_next_ref, 1 - s % 2)

    slot = s % 2
    pltpu.make_async_copy(buf.at[slot], buf.at[slot], sem.at[slot]).wait()
    cw = cw_ref[0]
    y = cw[:, 0:1] * buf[slot, 0] + cw[:, 1:2] * buf[slot, 1]
    o_ref[0] = x1_ref[0] + gt_ref[0] * y


def _combine(dest3, ys, x1, cw, gt2):
    b, s, d = x1.shape
    tq = TOKEN_TILE
    spb = s // tq
    n_steps = b * spb
    row = lambda t: (t // spb, t % spb, 0)
    idx = lambda t: (t, 0, 0)
    idx_next = lambda t: (jnp.minimum(t + 1, n_steps - 1), 0, 0)
    return pl.pallas_call(
        _combine_kernel,
        grid=(n_steps,),
        in_specs=[pl.BlockSpec((1, 1, MOE_SLOTS * tq), idx, memory_space=pltpu.SMEM),
                  pl.BlockSpec((1, 1, MOE_SLOTS * tq), idx_next, memory_space=pltpu.SMEM),
                  pl.BlockSpec(memory_space=pl.ANY),
                  pl.BlockSpec((1, tq, d), row),
                  pl.BlockSpec((1, tq, LANES), row),
                  pl.BlockSpec((1, 1, d), lambda t: (t // spb, 0, 0))],
        out_specs=pl.BlockSpec((1, tq, d), row),
        out_shape=jax.ShapeDtypeStruct((b, s, d), F32),
        scratch_shapes=[pltpu.VMEM((2, MOE_SLOTS, tq, d), F32), pltpu.SemaphoreType.DMA((2,))],
        compiler_params=_cparams("arbitrary"),
        name="moe_combine",
    )(dest3, dest3, ys, x1, cw, gt2)


def _moe(h2p, ids, cw, wg, wu, wd, x1, gt2):
    b, s, d = x1.shape
    n = b * s
    n_rows = _moe_max_tiles(n) * MOE_TILE
    dest, te = _plan(ids.reshape(n, LANES))
    dest3 = dest[:, :MOE_SLOTS].reshape(n // TOKEN_TILE, 1, MOE_SLOTS * TOKEN_TILE)
    tile_expert, n_tiles = te[:n_rows // MOE_TILE, 0], te[0, 1:2]
    hs = _dispatch(tile_expert, n_tiles, dest3, h2p.reshape(n, d // 2), n_rows)
    ys = _experts(tile_expert, n_tiles, hs, wg, wu, wd)
    return _combine(dest3, ys, x1, cw, gt2)


def _alibi_slopes():
    n = SWA_HEADS + DSA_HEADS
    i = jnp.arange(1, n + 1, dtype=F32)
    return jnp.exp2(-8.0 * i / n)


def _layer(x, mod, ln1_g, w_in, swa_q_norm, swa_k_norm, swa_sinks, dsa_q_norm, dsa_k_norm,
           w_branch_a, w_branch_b, w_out, ln2_g, w_group, b_group, w_expert, b_expert,
           w_gate_up, w_up, w_down, slopes):
    b, s, d = x.shape
    sh1, sc1, gt1, sh2, sc2, gt2 = [m[:, None, :] for m in jnp.split(mod, 6, axis=-1)]

    w_in_t = jnp.swapaxes(w_in, 0, 1)

    h1 = _ln_mod(x, ln1_g[None, :], sc1, sh1)
    h1f = h1.reshape(b * s, d)
    ones = lambda n: jnp.ones((n,), F32)
    mult = jnp.concatenate([
        jnp.tile(swa_q_norm, SWA_HEADS) * HEAD_DIM ** -0.5, jnp.tile(swa_k_norm, SWA_KV_HEADS),
        ones(SWA_KV_HEADS * HEAD_DIM),
        jnp.tile(dsa_q_norm, DSA_HEADS) * HEAD_DIM ** -0.5, dsa_k_norm, ones(HEAD_DIM),
        ones(IDX_HEADS * IDX_DIM) * IDX_DIM ** -0.5, ones(IDX_DIM), ones(IDX_HEADS) * IDX_HEADS ** -0.5,
        ones(ATT_WIDTH - COL_GATE)])[None, :]
    col = jnp.arange(ATT_WIDTH)
    normed = (((col >= COL_QA) & (col < COL_VA)) | ((col >= COL_QB) & (col < COL_KVB + KVB_V))
              ).astype(F32)[None, :]
    proj = _proj(h1f, w_in_t, mult, normed, 1024, 768).reshape(b, s, ATT_WIDTH)
    oa = _swa(proj, swa_sinks, slopes[:SWA_HEADS])
    ob = _dsa(proj, slopes[SWA_HEADS:])
    merged = _merge(h1f, oa.reshape(b * s, -1), ob.reshape(b * s, -1), w_in_t,
                    w_branch_a, w_branch_b)

    w_r = jnp.concatenate([w_expert, w_group,
                           jnp.zeros((d, LANES - N_EXPERTS - N_GROUPS), F32)], axis=1)
    b_r = jnp.concatenate([b_expert, b_group,
                           jnp.zeros((LANES - N_EXPERTS - N_GROUPS,), F32)])[None, :]
    w_r_hi = _bf16_head(w_r)
    w_r_split = jnp.stack([w_r_hi.astype(BF16), (w_r - w_r_hi).astype(BF16)])
    x1, h2p, ids, cw = _outproj(merged.reshape(b, s, d), w_out, x, gt1,
                                ln2_g[None, :], sc2, sh2, w_r_split, b_r)
    f = w_gate_up.shape[-1]
    return _moe(h2p, ids, cw, w_gate_up.reshape(N_EXPERTS, d, f), w_up.reshape(N_EXPERTS, d, f),
                w_down.reshape(N_EXPERTS, f, d), x1, gt2)


def kernel(x, c, w_ada, b_ada, ln1_g, w_in, swa_q_norm, swa_k_norm, swa_sinks, dsa_q_norm, dsa_k_norm,
           w_branch_a, w_branch_b, w_out, ln2_g, w_group, b_group, w_expert, b_expert,
           w_gate_up, w_up, w_down):
    slopes = _alibi_slopes()
    bsz = c.shape[0]
    c_pad = jnp.concatenate([c, jnp.zeros((ADA_ROWS - bsz, c.shape[1]), c.dtype)], axis=0)
    for l in range(w_ada.shape[0]):
        mod = _ada(c_pad, w_ada[l], b_ada[l][None, :])[:bsz]
        x = _layer(x, mod, ln1_g[l], w_in[l], swa_q_norm[l], swa_k_norm[l], swa_sinks[l],
                   dsa_q_norm[l], dsa_k_norm[l], w_branch_a[l], w_branch_b[l], w_out[l], ln2_g[l],
                   w_group[l], b_group[l], w_expert[l], b_expert[l],
                   w_gate_up[l], w_up[l], w_down[l], slopes)
    return x
```

```python
import functools

import jax
import jax.numpy as jnp
from jax import lax
from jax.experimental import pallas as pl
from jax.experimental.pallas import tpu as pltpu

F32 = jnp.float32
BF16 = jnp.bfloat16
I32 = jnp.int32

D_MODEL = 2048
HEAD_DIM = 64
SWA_HEADS = 16
SWA_KV_HEADS = 4
SWA_GROUP = SWA_HEADS // SWA_KV_HEADS
WINDOW = 128
DSA_HEADS = 16
IDX_HEADS = 16
IDX_DIM = 64
DSA_TOPK_MAX = 256
DSA_BLOCK = 128
N_GROUPS = 4
EXPERTS_PER_GROUP = 8
N_EXPERTS = N_GROUPS * EXPERTS_PER_GROUP
D_EXPERT = 512
RMS_EPS = 1e-6
NEG_INF = -1e30
INT_MIN = -(2 ** 31)

LANES = 128
VMEM_LIMIT = 56 * 1024 * 1024

COL_QA, COL_KA, COL_VA, COL_QB = 0, 1024, 1280, 1536
COL_KVB = 2560
COL_QI = 2688
COL_KWI = 3712
COL_GATE = 3792
ATT_WIDTH = 3840
KVB_K, KVB_V = 0, HEAD_DIM
KWI_K, KWI_W = 0, IDX_DIM

NT_DIMS = (((1,), (1,)), ((), ()))


def _cparams(*sem):
    return pltpu.CompilerParams(dimension_semantics=sem, vmem_limit_bytes=VMEM_LIMIT)


def _sigmoid(x):
    return 1.0 / (1.0 + jnp.exp(-x))


def _rms(x, g):
    return x * lax.rsqrt(jnp.mean(x * x, axis=-1, keepdims=True) + RMS_EPS) * g


def _ada_kernel(c_ref, w_ref, b_ref, o_ref):
    c = c_ref[...]
    cond = c * _sigmoid(c)
    c_hi = cond.astype(BF16)
    c_lo = (cond - c_hi.astype(F32)).astype(BF16)
    w = w_ref[...]
    w_hi = w.astype(BF16)
    w_lo = (w - w_hi.astype(F32)).astype(BF16)
    o_ref[...] = (jnp.dot(c_hi, w_hi, preferred_element_type=F32)
                  + jnp.dot(c_lo, w_hi, preferred_element_type=F32)
                  + jnp.dot(c_hi, w_lo, preferred_element_type=F32)) + b_ref[...]


ADA_ROWS = 16


def _ada(c_pad, w, b):
    n = w.shape[1]
    tn = 1024
    return pl.pallas_call(
        _ada_kernel,
        grid=(n // tn,),
        in_specs=[pl.BlockSpec((ADA_ROWS, D_MODEL), lambda j: (0, 0)),
                  pl.BlockSpec((D_MODEL, tn), lambda j: (0, j)),
                  pl.BlockSpec((1, tn), lambda j: (0, j))],
        out_specs=pl.BlockSpec((ADA_ROWS, tn), lambda j: (0, j)),
        out_shape=jax.ShapeDtypeStruct((ADA_ROWS, n), F32),
        compiler_params=_cparams("parallel"),
        name="ada_mod",
    )(c_pad, w, b)


def _ln_mod_kernel(x_ref, g_ref, sc_ref, sh_ref, o_ref):
    y = _rms(x_ref[0], g_ref[...])
    o_ref[0] = (y * (1.0 + sc_ref[0]) + sh_ref[0]).astype(o_ref.dtype)


def _ln_mod(x, g, sc, sh):
    b, s, d = x.shape
    ts = 512
    return pl.pallas_call(
        _ln_mod_kernel,
        grid=(b, s // ts),
        in_specs=[pl.BlockSpec((1, ts, d), lambda i, j: (i, j, 0)),
                  pl.BlockSpec((1, d), lambda i, j: (0, 0)),
                  pl.BlockSpec((1, 1, d), lambda i, j: (i, 0, 0)),
                  pl.BlockSpec((1, 1, d), lambda i, j: (i, 0, 0))],
        out_specs=pl.BlockSpec((1, ts, d), lambda i, j: (i, j, 0)),
        out_shape=jax.ShapeDtypeStruct((b, s, d), BF16),
        compiler_params=_cparams("parallel", "parallel"),
        name="ln_mod",
    )(x, g, sc, sh)


PROJ_ROW_PARTS = 4


def _proj_kernel(a_ref, wt_ref, mult_ref, normed_ref, o_ref):
    w = wt_ref[...].astype(BF16)
    tm, tn = o_ref.shape
    rows_per_part = tm // PROJ_ROW_PARTS
    half = lax.broadcasted_iota(I32, (rows_per_part, LANES), 1) < HEAD_DIM
    for part in range(PROJ_ROW_PARTS):
        rows = slice(part * rows_per_part, (part + 1) * rows_per_part)
        y = lax.dot_general(a_ref[rows, :], w, NT_DIMS, preferred_element_type=F32)
        for g in range(tn // LANES):
            cols = slice(g * LANES, (g + 1) * LANES)
            yg = y[:, cols]
            sq = yg * yg
            ms_lo = jnp.sum(jnp.where(half, sq, 0.0), axis=-1, keepdims=True) / HEAD_DIM
            ms_hi = jnp.sum(jnp.where(half, 0.0, sq), axis=-1, keepdims=True) / HEAD_DIM
            r = lax.rsqrt(jnp.where(half, ms_lo, ms_hi) + RMS_EPS)
            o_ref[rows, cols] = yg * jnp.where(normed_ref[:, cols] > 0.0, r, 1.0) * mult_ref[:, cols]


def _proj(a, wt, mult, normed, tm, tn):
    m, k = a.shape
    n = mult.shape[1]
    vec = pl.BlockSpec((1, tn), lambda i, j: (0, j))
    return pl.pallas_call(
        _proj_kernel,
        grid=(m // tm, n // tn),
        in_specs=[pl.BlockSpec((tm, k), lambda i, j: (i, 0)),
                  pl.BlockSpec((tn, k), lambda i, j: (j, 0)),
                  vec, vec],
        out_specs=pl.BlockSpec((tm, tn), lambda i, j: (i, j)),
        out_shape=jax.ShapeDtypeStruct((m, n), F32),
        compiler_params=_cparams("parallel", "parallel"),
        name="proj_att",
    )(a, wt, mult, normed)


def _swa_kernel(sink_ref, slope_ref, q_ref, kp_ref, kc_ref, vp_ref, vc_ref, o_ref, lhs, e_s):
    n = pl.program_id(1)
    w, w2 = WINDOW, 2 * WINDOW
    q = q_ref[0]
    kw = jnp.concatenate([kp_ref[0], kc_ref[0]], axis=0)
    vw = jnp.concatenate([vp_ref[0], vc_ref[0]], axis=0)
    half_q = lax.broadcasted_iota(I32, (w, LANES), 1) < HEAD_DIM
    half_k = lax.broadcasted_iota(I32, (w2, LANES), 1) < HEAD_DIM
    row = lax.broadcasted_iota(I32, (w, w2), 0)
    col = lax.broadcasted_iota(I32, (w, w2), 1)
    rel = row + w - col
    valid = (rel >= 0) & (rel < w) & ((col >= w) | (n > 0))
    mbias = jnp.where(valid, 0.0, NEG_INF)
    relf = rel.astype(F32)

    ones_col = jnp.ones((w2, LANES), BF16)
    for g in range(SWA_KV_HEADS):
        if g % 2 == 0:
            grp = slice((g // 2) * LANES, (g // 2 + 1) * LANES)
            kn = kw[:, grp]
            kn_sw = pltpu.roll(kn, HEAD_DIM, axis=1)
            vp = vw[:, grp]
            vp_sw = pltpu.roll(vp, HEAD_DIM, axis=1)
        kdup = jnp.where(half_k, kn, kn_sw) if g % 2 == 0 else jnp.where(half_k, kn_sw, kn)
        vdup = jnp.where(half_k, vp, vp_sw) if g % 2 == 0 else jnp.where(half_k, vp_sw, vp)
        vx = jnp.concatenate([vdup.astype(BF16), ones_col], axis=1)
        for r in range(SWA_GROUP):
            h = g * SWA_GROUP + r
            if r % 2 == 0:
                qn = q[:, (h // 2) * LANES:(h // 2 + 1) * LANES]
            keep = half_q if r % 2 == 0 else jnp.logical_not(half_q)
            lhs[r * w:(r + 1) * w, :] = jnp.where(keep, qn, 0.0).astype(BF16)
        s_all = lax.dot_general(lhs[...], kdup.astype(BF16), NT_DIMS, preferred_element_type=F32)
        sink_e = []
        for r in range(SWA_GROUP):
            h = g * SWA_GROUP + r
            rows = slice(r * w, (r + 1) * w)
            s = s_all[rows, :] + (mbias - slope_ref[h] * relf)
            sink = jnp.full((w, LANES), sink_ref[h], F32)
            m = jnp.maximum(jnp.max(s, axis=-1, keepdims=True), sink)
            e_s[rows, :] = jnp.exp(s - jnp.concatenate([m, m], axis=1)).astype(BF16)
            sink_e.append(jnp.exp(sink - m))
        pv = jnp.dot(e_s[...], vx, preferred_element_type=F32)
        for r in range(0, SWA_GROUP, 2):
            h = g * SWA_GROUP + r
            ev, od = pv[r * w:(r + 1) * w, :], pv[(r + 1) * w:(r + 2) * w, :]
            den = jnp.where(half_q, ev[:, LANES:] + sink_e[r], od[:, LANES:] + sink_e[r + 1])
            o = jnp.where(half_q, ev[:, :LANES], od[:, :LANES]) / den
            o_ref[0, :, (h // 2) * LANES:(h // 2 + 1) * LANES] = o.astype(o_ref.dtype)


def _swa(proj3, sinks, slopes):
    b, s, _ = proj3.shape
    nb = s // WINDOW
    kvw = SWA_KV_HEADS * HEAD_DIM
    qw = SWA_HEADS * HEAD_DIM
    smem = pl.BlockSpec(memory_space=pltpu.SMEM)
    prev =lambda i, n: (i, jnp.maximum(n - 1, 0), COL_KA // kvw)
    cur = lambda i, n: (i, n, COL_KA // kvw)
    prev_v = lambda i, n: (i, jnp.maximum(n - 1, 0), COL_VA // kvw)
    cur_v = lambda i, n: (i, n, COL_VA // kvw)
    return pl.pallas_call(
        _swa_kernel,
        grid=(b, nb),
        in_specs=[smem, smem,
                  pl.BlockSpec((1, WINDOW, qw), lambda i, n: (i, n, COL_QA // qw)),
                  pl.BlockSpec((1, WINDOW, kvw), prev),
                  pl.BlockSpec((1, WINDOW, kvw), cur),
                  pl.BlockSpec((1, WINDOW, kvw), prev_v),
                  pl.BlockSpec((1, WINDOW, kvw), cur_v)],
        out_specs=pl.BlockSpec((1, WINDOW, qw), lambda i, n: (i, n, 0)),
        out_shape=jax.ShapeDtypeStruct((b, s, qw), BF16),
        scratch_shapes=[pltpu.VMEM((SWA_GROUP * WINDOW, LANES), BF16),
                        pltpu.VMEM((SWA_GROUP * WINDOW, 2 * WINDOW), BF16)],
        compiler_params=_cparams("parallel", "parallel"),
        name="swa_attn",
    )(sinks, slopes, proj3, proj3, proj3, proj3, proj3)


DSA_CHUNK = 256
DSA_PART_HEADS = 4
ALIBI_SPLIT = 3
POS_LO_BITS = 7


def _bf16_head(x):
    return lax.bitcast_convert_type(lax.bitcast_convert_type(x, I32) & jnp.int32(-65536), F32)


def _alibi_columns(slopes):
    parts, rest = [], slopes.astype(F32)
    for _ in range(ALIBI_SPLIT):
        piece = _bf16_head(rest)
        parts.append(piece)
        rest = rest - piece
    cols = [p * float(2 ** POS_LO_BITS) for p in parts] + parts
    pad = jnp.zeros((slopes.shape[0], LANES - 2 * ALIBI_SPLIT), F32)
    return jnp.concatenate([jnp.stack(cols, axis=1), pad], axis=1)


def _dsa_kernel(*refs, topk):
    n_pairs = DSA_HEADS // 2
    qb_refs, qi_refs = refs[:n_pairs], refs[n_pairs:2 * n_pairs]
    (kwi_blk_ref, kvb_ref, kwi_ref, acol_ref, o_ref,
     ki2, kx, vx, lhs_i, lhs_q, wb, key_s, keyt_s, s_s, thr_s, acc_s, m_s) = refs[2 * n_pairs:]
    i = pl.program_id(1)
    blk, ck = DSA_BLOCK, DSA_CHUNK
    nch = (i + 2) // 2
    seq = ki2.shape[0]
    max_ch = seq // ck
    half = lax.broadcasted_iota(I32, (blk, LANES), 1) < HEAD_DIM

    @pl.when(i == 0)
    def _():
        kvb = kvb_ref[0]
        kib = kwi_ref[0][:, KWI_K:KWI_K + IDX_DIM].astype(BF16)
        ki2[...] = jnp.concatenate([kib, kib], axis=1)
        kn = kvb[:, KVB_K:KVB_K + HEAD_DIM].astype(BF16)
        pos = lax.broadcasted_iota(I32, (seq, LANES), 0)
        lane = lax.broadcasted_iota(I32, (seq, LANES), 1)
        hi = lax.shift_right_logical(pos, POS_LO_BITS).astype(F32)
        lo = (pos & (2 ** POS_LO_BITS - 1)).astype(F32)
        posc = jnp.where(lane < ALIBI_SPLIT, hi, jnp.where(lane < 2 * ALIBI_SPLIT, lo, 0.0)).astype(BF16)
        kx[...] = jnp.concatenate([kn, kn, posc], axis=1)
        ones = jnp.where(lax.broadcasted_iota(I32, (seq, HEAD_DIM), 1) == 0, 1.0, 0.0).astype(BF16)
        vx[...] = jnp.concatenate([kvb[:, KVB_V:KVB_V + HEAD_DIM].astype(BF16), ones], axis=1)

    wi = kwi_blk_ref[0][:, KWI_W:KWI_W + IDX_HEADS]
    for p in range(n_pairs):
        qn = qb_refs[p][0]
        qip = qi_refs[p][0]
        for k in range(2):
            h = 2 * p + k
            keep = half if k == 0 else jnp.logical_not(half)
            rows = slice(h * blk, (h + 1) * blk)
            lhs_i[rows, :] = jnp.where(keep, qip, 0.0).astype(BF16)
            lhs_q[rows, :LANES] = jnp.where(keep, qn, 0.0).astype(BF16)
            lhs_q[rows, LANES:] = jnp.broadcast_to(acol_ref[h:h + 1, :], (blk, LANES)).astype(BF16)
            wb[h] = jnp.broadcast_to(wi[:, h:h + 1], (blk, LANES))

    row_t = i * blk + lax.broadcasted_iota(I32, (blk, ck), 0)
    col_k = lax.broadcasted_iota(I32, (blk, ck), 1)

    def score_chunk(c, carry):
        ks = pl.ds(pl.multiple_of(c * ck, ck), ck)
        kic = ki2[ks, :]
        sc = jnp.zeros((blk, ck), F32)
        for part in range(0, IDX_HEADS, DSA_PART_HEADS):
            logits = lax.dot_general(lhs_i[part * blk:(part + DSA_PART_HEADS) * blk, :], kic, NT_DIMS,
                                     preferred_element_type=F32)
            for k in range(DSA_PART_HEADS):
                w = wb[part + k]
                lh = jnp.maximum(logits[k * blk:(k + 1) * blk, :], 0.0)
                sc = sc + lh * jnp.concatenate([w, w], axis=1)
        bits = lax.bitcast_convert_type(sc + 0.0, I32)
        key = jnp.where(bits < 0, bits ^ jnp.int32(0x7FFFFFFF), bits)
        key = jnp.where(c * ck + col_k <= row_t, key, jnp.int32(INT_MIN))
        key_s[c] = key
        keyt_s[c] = key.T
        return carry

    lax.fori_loop(0, nch, score_chunk, 0)

    thr_s[...] = jnp.full(thr_s.shape, INT_MIN, I32)
    for n in range(2, max_ch + 1):
        @pl.when(nch == n)
        def _(n=n):
            def step(it, thr):
                cand = thr + lax.shift_left(jnp.int32(1), 31 - it)
                cnt8 = jnp.zeros((8, blk), F32)
                for c in range(n):
                    hit = jnp.where(keyt_s[c] >= cand, 1.0, 0.0)
                    cnt8 = cnt8 + jnp.sum(hit.reshape(ck // 8, 8, blk), axis=0)
                cnt = jnp.sum(cnt8, axis=0, keepdims=True)
                return jnp.where(cnt >= float(topk), cand, thr)

            thr = lax.fori_loop(0, 32, step, jnp.full((1, blk), INT_MIN, I32))
            thr_s[...] = jnp.broadcast_to(thr, thr_s.shape).T

    thr = thr_s[...]
    thr2 = jnp.concatenate([thr, thr], axis=1)

    def count_gt(c, cnt):
        k = key_s[c]
        return cnt + jnp.where(k[:, :LANES] > thr, 1.0, 0.0) + jnp.where(k[:, LANES:] > thr, 1.0, 0.0)

    n_gt = jnp.sum(lax.fori_loop(0, nch, count_gt, jnp.zeros((blk, LANES), F32)), axis=-1, keepdims=True)
    need = float(topk) - n_gt
    upper = (lax.broadcasted_iota(I32, (ck, ck), 0) < lax.broadcasted_iota(I32, (ck, ck), 1)).astype(BF16)

    m_s[...] = jnp.full(m_s.shape, -jnp.inf, F32)

    def masked_scores(c, n_eq):
        ks = pl.ds(pl.multiple_of(c * ck, ck), ck)
        k = key_s[c]
        eq = (k == thr2) & (c * ck + col_k <= row_t)
        eqf = eq.astype(BF16)
        before = jnp.dot(eqf, upper, preferred_element_type=F32) + n_eq
        sel = (k > thr2) | (eq & (before < need))
        mb = jnp.where(sel, 0.0, NEG_INF)
        kxc = kx[ks, :]
        for part in range(0, DSA_HEADS, DSA_PART_HEADS):
            s_part = lax.dot_general(lhs_q[part * blk:(part + DSA_PART_HEADS) * blk, :], kxc, NT_DIMS,
                                     preferred_element_type=F32)
            for k in range(DSA_PART_HEADS):
                rows = slice((part + k) * blk, (part + k + 1) * blk)
                s = s_part[k * blk:(k + 1) * blk, :] + mb
                s_s[c, rows, :] = s
                m_s[rows, :] = jnp.maximum(m_s[rows, :], jnp.maximum(s[:, :LANES], s[:, LANES:]))
        return n_eq + jnp.sum(eqf.astype(F32), axis=-1, keepdims=True)

    lax.fori_loop(0, nch, masked_scores, jnp.zeros((blk, 1), F32))

    m_s[...] = jnp.broadcast_to(jnp.max(m_s[...], axis=-1, keepdims=True), m_s.shape)
    acc_s[...] = jnp.zeros_like(acc_s)

    def attend(c, carry):
        vxc = vx[pl.ds(pl.multiple_of(c * ck, ck), ck), :]
        for part in range(0, DSA_HEADS, DSA_PART_HEADS):
            e = []
            for k in range(DSA_PART_HEADS):
                rows = slice((part + k) * blk, (part + k + 1) * blk)
                m = m_s[rows, :]
                e.append(jnp.exp(s_s[c, rows, :] - jnp.concatenate([m, m], axis=1)).astype(BF16))
            rows = slice(part * blk, (part + DSA_PART_HEADS) * blk)
            acc_s[rows, :] += jnp.dot(jnp.concatenate(e, axis=0), vxc, preferred_element_type=F32)
        return carry

    lax.fori_loop(0, nch, attend, 0)

    for h in range(DSA_HEADS):
        a = acc_s[h * blk:(h + 1) * blk, :]
        o = a[:, :HEAD_DIM] / a[:, HEAD_DIM:HEAD_DIM + 1]
        o_ref[0, :, h * HEAD_DIM:(h + 1) * HEAD_DIM] = o.astype(o_ref.dtype)


def _dsa(proj3, slopes):
    b, s, _ = proj3.shape
    nb = s // DSA_BLOCK
    qw = DSA_HEADS * HEAD_DIM
    n_pairs = DSA_HEADS // 2
    rows = DSA_HEADS * DSA_BLOCK
    fixed = lambda i, n: (0, 0)

    def group(col):
        return pl.BlockSpec((1, DSA_BLOCK, LANES), lambda i, n: (i, n, col // LANES))

    def group_all(col):
        return pl.BlockSpec((1, s, LANES), lambda i, n: (i, 0, col // LANES))

    return pl.pallas_call(
        functools.partial(_dsa_kernel, topk=min(DSA_TOPK_MAX, s // 4)),
        grid=(b, nb),
        in_specs=([group(COL_QB + p * LANES) for p in range(n_pairs)]
                  + [group(COL_QI + p * LANES) for p in range(n_pairs)]
                  + [group(COL_KWI), group_all(COL_KVB), group_all(COL_KWI),
                     pl.BlockSpec((DSA_HEADS, LANES), fixed)]),
        out_specs=pl.BlockSpec((1, DSA_BLOCK, qw), lambda i, n: (i, n, 0)),
        out_shape=jax.ShapeDtypeStruct((b, s, qw), BF16),
        scratch_shapes=[pltpu.VMEM((s, LANES), BF16),
                        pltpu.VMEM((s, 2 * LANES), BF16),
                        pltpu.VMEM((s, LANES), BF16),
                        pltpu.VMEM((rows, LANES), BF16),
                        pltpu.VMEM((rows, 2 * LANES), BF16),
                        pltpu.VMEM((IDX_HEADS, DSA_BLOCK, LANES), F32),
                        pltpu.VMEM((s // DSA_CHUNK, DSA_BLOCK, DSA_CHUNK), I32),
                        pltpu.VMEM((s // DSA_CHUNK, DSA_CHUNK, DSA_BLOCK), I32),
                        pltpu.VMEM((s // DSA_CHUNK, rows, DSA_CHUNK), F32),
                        pltpu.VMEM((DSA_BLOCK, LANES), I32),
                        pltpu.VMEM((rows, LANES), F32),
                        pltpu.VMEM((rows, LANES), F32)],
        compiler_params=_cparams("arbitrary", "arbitrary"),
        name="dsa_attn",
    )(*([proj3] * (2 * n_pairs + 3)), _alibi_columns(slopes))


def _merge_kernel(h_ref, oa_ref, ob_ref, wt_hbm, wa_ref, wb_ref, o_ref, g32, gbf, sem):
    j, i = pl.program_id(0), pl.program_id(1)
    tn, d = o_ref.shape[1], h_ref.shape[1]

    def gate_copies(jj, slot):
        return [pltpu.make_async_copy(
            wt_hbm.at[pl.ds(pl.multiple_of(COL_GATE + k * d + jj * tn, 8), tn)], g32.at[slot, k], sem.at[slot, k])
            for k in range(2)]

    @pl.when((j == 0) & (i == 0))
    def _():
        for c in gate_copies(0, 0):
            c.start()

    @pl.when(i == 0)
    def _():
        slot = j % 2
        for c in gate_copies(j, slot):
            c.wait()
        gbf[...] = g32[slot].astype(BF16)

        @pl.when(j + 1 < pl.num_programs(0))
        def _():
            for c in gate_copies(j + 1, 1 - slot):
                c.start()

    h = h_ref[...]
    ga = lax.dot_general(h, gbf[0], NT_DIMS, preferred_element_type=F32)
    gb = lax.dot_general(h, gbf[1], NT_DIMS, preferred_element_type=F32)
    a = jnp.dot(oa_ref[...], wa_ref[...].astype(BF16), preferred_element_type=F32)
    b = jnp.dot(ob_ref[...], wb_ref[...].astype(BF16), preferred_element_type=F32)
    o_ref[...] = (_sigmoid(ga) * a + _sigmoid(gb) * b).astype(o_ref.dtype)


def _merge(h, oa, ob, w_in_t, wa, wb):
    m, d = h.shape
    tm, tn = 1024, 512
    return pl.pallas_call(
        _merge_kernel,
        grid=(d // tn, m // tm),
        in_specs=[pl.BlockSpec((tm, d), lambda j, i: (i, 0)),
                  pl.BlockSpec((tm, oa.shape[1]), lambda j, i: (i, 0)),
                  pl.BlockSpec((tm, ob.shape[1]), lambda j, i: (i, 0)),
                  pl.BlockSpec(memory_space=pl.ANY),
                  pl.BlockSpec((wa.shape[0], tn), lambda j, i: (0, j)),
                  pl.BlockSpec((wb.shape[0], tn), lambda j, i: (0, j))],
        out_specs=pl.BlockSpec((tm, tn), lambda j, i: (i, j)),
        out_shape=jax.ShapeDtypeStruct((m, d), BF16),
        scratch_shapes=[pltpu.VMEM((2, 2, tn, d), w_in_t.dtype),
                        pltpu.VMEM((2, tn, d), BF16),
                        pltpu.SemaphoreType.DMA((2, 2))],
        compiler_params=_cparams("arbitrary", "arbitrary"),
        name="merge_branches",
    )(h, oa, ob, w_in_t, wa, wb)


MOE_TILE = 256
MOE_SLOTS = 2
TOKEN_TILE = 512


def _moe_max_tiles(n_tokens):
    return -(-(n_tokens * MOE_SLOTS + N_EXPERTS * (MOE_TILE - 1)) // MOE_TILE)


def _pack_bf16_pair(x):
    k = x.shape[1] // 2
    bits = lax.bitcast_convert_type(x.astype(BF16).astype(F32), I32)
    return lax.shift_right_logical(bits[:, :k], 16) | (bits[:, k:] & jnp.int32(-65536))


def _unpack_bf16_pair(w):
    lo = lax.bitcast_convert_type(lax.shift_left(w, 16), F32).astype(BF16)
    hi = lax.bitcast_convert_type(w & jnp.int32(-65536), F32).astype(BF16)
    return lo, hi


def _route(logits):
    lane = lax.broadcasted_iota(I32, logits.shape, 1)
    is_g = (lane >= N_EXPERTS) & (lane < N_EXPERTS + N_GROUPS)
    neg = -jnp.inf
    gl = jnp.where(is_g, logits, neg)
    gmax = jnp.max(gl, axis=-1, keepdims=True)
    g_idx = jnp.min(jnp.where(gl == gmax, lane - N_EXPERTS, N_GROUPS), axis=-1, keepdims=True)
    p_g = 1.0 / jnp.sum(jnp.where(is_g, jnp.exp(gl - gmax), 0.0), axis=-1, keepdims=True)
    in_grp = (lane < N_EXPERTS) & ((lane // EXPERTS_PER_GROUP) == g_idx)
    el = jnp.where(in_grp, logits, neg)
    m1 = jnp.max(el, axis=-1, keepdims=True)
    i1 = jnp.min(jnp.where(el == m1, lane, LANES), axis=-1, keepdims=True)
    el2 = jnp.where(lane == i1, neg, el)
    m2 = jnp.max(el2, axis=-1, keepdims=True)
    i2 = jnp.min(jnp.where(el2 == m2, lane, LANES), axis=-1, keepdims=True)
    e2 = jnp.exp(m2 - m1)
    w1 = 1.0 / (1.0 + e2)
    w2 = e2 / (1.0 + e2)
    ids = jnp.where(lane == 0, i1, jnp.where(lane == 1, i2, 0))
    return ids, p_g * jnp.where(lane == 0, w1, jnp.where(lane == 1, w2, 0.0))


def _outproj_kernel(mg_ref, wo_ref, x_ref, gt_ref, g2_ref, sc_ref, sh_ref, wr_ref, br_ref,
                    x1_ref, h2p_ref, ids_ref, cw_ref):
    y = jnp.dot(mg_ref[0], wo_ref[...].astype(BF16), preferred_element_type=F32)
    x1 = x_ref[0] + gt_ref[0] * y
    x1_ref[0] = x1
    h2 = _rms(x1, g2_ref[...]) * (1.0 + sc_ref[0]) + sh_ref[0]
    h2p_ref[0] = _pack_bf16_pair(h2)
    h_hi = h2.astype(BF16)
    h_lo = (h2 - h_hi.astype(F32)).astype(BF16)
    both = jnp.dot(h_hi, wr_ref[...], preferred_element_type=F32)
    logits = (both[:, :LANES] + jnp.dot(h_lo, wr_ref[:, :LANES], preferred_element_type=F32)
              + both[:, LANES:]) + br_ref[...]
    ids_ref[0], cw_ref[0] = _route(logits)


def _outproj(merged3, w_out, x, gt1, g2, sc2, sh2, w_r, b_r):
    b, s, d = x.shape
    tm = 512
    row = lambda i, j: (i, j, 0)
    per_b = lambda i, j: (i, 0, 0)
    fixed = lambda i, j: (0, 0)
    return pl.pallas_call(
        _outproj_kernel,
        grid=(b, s // tm),
        in_specs=[pl.BlockSpec((1, tm, d), row),
                  pl.BlockSpec((d, d), fixed, pipeline_mode=pl.Buffered(1)),
                  pl.BlockSpec((1, tm, d), row),
                  pl.BlockSpec((1, 1, d), per_b),
                  pl.BlockSpec((1, d), fixed),
                  pl.BlockSpec((1, 1, d), per_b),
                  pl.BlockSpec((1, 1, d), per_b),
                  pl.BlockSpec((d, 2 * LANES), fixed),
                  pl.BlockSpec((1, LANES), fixed)],
        out_specs=[pl.BlockSpec((1, tm, d), row),
                   pl.BlockSpec((1, tm, d // 2), row),
                   pl.BlockSpec((1, tm, LANES), row),
                   pl.BlockSpec((1, tm, LANES), row)],
        out_shape=[jax.ShapeDtypeStruct((b, s, d), F32),
                   jax.ShapeDtypeStruct((b, s, d // 2), I32),
                   jax.ShapeDtypeStruct((b, s, LANES), I32),
                   jax.ShapeDtypeStruct((b, s, LANES), F32)],
        compiler_params=_cparams("parallel", "parallel"),
        name="outproj_ln2_router",
    )(merged3, w_out, x, gt1, g2, sc2, sh2, w_r, b_r)


def _plan_kernel(ids_ref, dest_ref, te_ref, rank_ref):
    n = ids_ref.shape[0]
    cb = 256
    lane = lax.broadcasted_iota(I32, (cb, LANES), 1)
    lower = (lax.broadcasted_iota(I32, (cb, cb), 0) > lax.broadcasted_iota(I32, (cb, cb), 1)).astype(BF16)

    def count(bi, carry):
        rows = pl.ds(pl.multiple_of(bi * cb, cb), cb)
        ids = ids_ref[rows, :]
        m1 = lane == ids[:, 0:1]
        m2 = lane == ids[:, 1:2]
        onehot = m1 | m2
        before = jnp.dot(lower, onehot.astype(BF16), preferred_element_type=F32) + carry
        r1 = jnp.sum(jnp.where(m1, before, 0.0), axis=-1, keepdims=True)
        r2 = jnp.sum(jnp.where(m2, before, 0.0), axis=-1, keepdims=True)
        rank_ref[rows, :] = jnp.where(lane == 0, r1, jnp.where(lane == 1, r2, 0.0))
        return carry + jnp.sum(onehot.astype(F32), axis=0, keepdims=True)

    counts = lax.fori_loop(0, n // cb, count, jnp.zeros((1, LANES), F32))
    padded = jnp.floor((counts + (MOE_TILE - 1)) / MOE_TILE) * MOE_TILE
    upper = (lax.broadcasted_iota(I32, (LANES, LANES), 0)
             < lax.broadcasted_iota(I32, (LANES, LANES), 1)).astype(F32)
    offs = jnp.dot(jnp.broadcast_to(padded, (8, LANES)), upper, preferred_element_type=F32,
                   precision=lax.Precision.HIGHEST)[0:1]

    def place(bi, c):
        rows = pl.ds(pl.multiple_of(bi * cb, cb), cb)
        ids = ids_ref[rows, :]
        o1 = jnp.sum(jnp.where(lane == ids[:, 0:1], offs, 0.0), axis=-1, keepdims=True)
        o2 = jnp.sum(jnp.where(lane == ids[:, 1:2], offs, 0.0), axis=-1, keepdims=True)
        slot = rank_ref[rows, :] + jnp.where(lane == 0, o1, jnp.where(lane == 1, o2, 0.0))
        dest_ref[rows, :] = slot.astype(I32)
        return c

    lax.fori_loop(0, n // cb, place, 0)

    sq_row = lax.broadcasted_iota(I32, (LANES, LANES), 0)
    sq_lane = lax.broadcasted_iota(I32, (LANES, LANES), 1)
    tile_start = sq_row.astype(F32) * MOE_TILE
    finished = ((offs + padded) <= tile_start) & (sq_lane < N_EXPERTS)
    tile_expert = jnp.minimum(jnp.sum(finished.astype(F32), axis=-1, keepdims=True), N_EXPERTS - 1.0)
    n_tiles = jnp.sum(padded, axis=-1, keepdims=True) / MOE_TILE
    te_ref[...] = jnp.where(sq_lane == 1, n_tiles, tile_expert).astype(I32)


def _plan(ids):
    n = ids.shape[0]
    return pl.pallas_call(
        _plan_kernel,
        out_shape=[jax.ShapeDtypeStruct((n, LANES), I32),
                   jax.ShapeDtypeStruct((LANES, LANES), I32)],
        scratch_shapes=[pltpu.VMEM((n, LANES), F32)],
        compiler_params=pltpu.CompilerParams(vmem_limit_bytes=VMEM_LIMIT),
        name="moe_plan",
    )(ids)


def _dispatch_kernel(te_ref, nt_ref, dest_ref, h_ref, hs_ref, zeros, sem, zsem):
    tq = h_ref.shape[0]

    @pl.when(pl.program_id(0) == 0)
    def _():
        zeros[...] = jnp.zeros_like(zeros)
        n_tiles, nt = te_ref.shape[0], nt_ref[0]

        def has_padding(t):
            return (t >= nt - 1) | (te_ref[jnp.minimum(t + 1, n_tiles - 1)] != te_ref[t])

        def tile_copy(t):
            return pltpu.make_async_copy(zeros, hs_ref.at[pl.ds(pl.multiple_of(t * MOE_TILE, MOE_TILE), MOE_TILE)], zsem)

        def start(t, c):
            @pl.when(has_padding(t))
            def _():
                tile_copy(t).start()
            return c

        def wait(t, c):
            @pl.when(has_padding(t))
            def _():
                tile_copy(t).wait()
            return c

        lax.fori_loop(0, n_tiles, start, 0)
        lax.fori_loop(0, n_tiles, wait, 0)

    def issue(r, c):
        for k in range(MOE_SLOTS):
            slot = dest_ref[0, 0, MOE_SLOTS * r + k]
            pltpu.make_async_copy(h_ref.at[pl.ds(r, 1)], hs_ref.at[pl.ds(slot, 1)], sem).start(priority=k)
        return c

    lax.fori_loop(0, tq, issue, 0, unroll=8)
    done = hs_ref.at[pl.ds(0, MOE_SLOTS * tq)]
    pltpu.make_async_copy(done, done, sem).wait()


def _dispatch(tile_expert, n_tiles, dest3, h2p, n_rows):
    n, w = h2p.shape
    tq = TOKEN_TILE
    return pl.pallas_call(
        _dispatch_kernel,
        grid_spec=pltpu.PrefetchScalarGridSpec(
            num_scalar_prefetch=2,
            grid=(n // tq,),
            in_specs=[pl.BlockSpec((1, 1, MOE_SLOTS * tq), lambda i, te, nt: (i, 0, 0),
                                   memory_space=pltpu.SMEM),
                      pl.BlockSpec((tq, w), lambda i, te, nt: (i, 0))],
            out_specs=pl.BlockSpec(memory_space=pl.ANY),
            scratch_shapes=[pltpu.VMEM((MOE_TILE, w), h2p.dtype),
                            pltpu.SemaphoreType.DMA(()),
                            pltpu.SemaphoreType.DMA(())]),
        out_shape=jax.ShapeDtypeStruct((n_rows, w), h2p.dtype),
        compiler_params=_cparams("arbitrary"),
        name="moe_dispatch",
    )(tile_expert, n_tiles, dest3, h2p)


def _expert_kernel(te_ref, nt_ref, hs_ref, wg_hbm, wu_hbm, wd_hbm, ys_ref,
                   wg_buf, wu_buf, wd_buf, sem, slot_ref):
    t = pl.program_id(0)
    nt = nt_ref[0]
    used = t < nt
    last = te_ref.shape[0] - 1

    def weight_copies(e, slot):
        return (pltpu.make_async_copy(wg_hbm.at[e], wg_buf.at[slot], sem.at[0, slot]),
                pltpu.make_async_copy(wu_hbm.at[e], wu_buf.at[slot], sem.at[1, slot]),
                pltpu.make_async_copy(wd_hbm.at[e], wd_buf.at[slot], sem.at[2, slot]))

    @pl.when(t == 0)
    def _():
        slot_ref[0] = 0
        for c in weight_copies(te_ref[0], 0):
            c.start()

    @pl.when(jnp.logical_not(used))
    def _():
        ys_ref[...] = jnp.zeros_like(ys_ref)

    @pl.when(used)
    def _():
        e = te_ref[t]
        first = (t == 0) | (te_ref[jnp.maximum(t - 1, 0)] != e)

        @pl.when(first & (t > 0))
        def _():
            slot_ref[0] = 1 - slot_ref[0]

        slot = slot_ref[0]

        @pl.when(first)
        def _():
            for c in weight_copies(e, slot):
                c.wait()
            nxt = lax.while_loop(lambda s: (s < nt) & (te_ref[jnp.minimum(s, last)] == e),
                                 lambda s: s + 1, t + 1)

            @pl.when(nxt < nt)
            def _():
                for c in weight_copies(te_ref[jnp.minimum(nxt, last)], 1 - slot):
                    c.start()

        lo, hi = _unpack_bf16_pair(hs_ref[...])
        k = lo.shape[1]

        def up(w_buf):
            return (jnp.dot(lo, w_buf[slot, :k, :].astype(BF16), preferred_element_type=F32)
                    + jnp.dot(hi, w_buf[slot, k:, :].astype(BF16), preferred_element_type=F32))

        a = up(wg_buf)
        u = up(wu_buf)
        act = (a * _sigmoid(a) * u).astype(BF16)
        ys_ref[...] = jnp.dot(act, wd_buf[slot].astype(BF16), preferred_element_type=F32)


def _experts(tile_expert, n_tiles, hs, wg, wu, wd):
    n_rows, hw = hs.shape
    d, f = wg.shape[1], wg.shape[2]
    row = lambda t, te, nt: (jnp.minimum(t, nt[0] - 1), 0)
    hbm = pl.BlockSpec(memory_space=pl.ANY)
    return pl.pallas_call(
        _expert_kernel,
        grid_spec=pltpu.PrefetchScalarGridSpec(
            num_scalar_prefetch=2,
            grid=(n_rows // MOE_TILE,),
            in_specs=[pl.BlockSpec((MOE_TILE, hw), row), hbm, hbm, hbm],
            out_specs=pl.BlockSpec((MOE_TILE, d), lambda t, te, nt: (t, 0)),
            scratch_shapes=[pltpu.VMEM((2, d, f), wg.dtype),
                            pltpu.VMEM((2, d, f), wu.dtype),
                            pltpu.VMEM((2, f, d), wd.dtype),
                            pltpu.SemaphoreType.DMA((3, 2)),
                            pltpu.SMEM((1,), I32)]),
        out_shape=jax.ShapeDtypeStruct((n_rows, d), F32),
        compiler_params=_cparams("arbitrary"),
        name="moe_experts",
    )(tile_expert, n_tiles, hs, wg, wu, wd)


def _combine_kernel(dest_ref, dest_next_ref, ys_ref, x1_ref, cw_ref, gt_ref, o_ref, buf, sem):
    s, n_steps = pl.program_id(0), pl.num_programs(0)
    tq = x1_ref.shape[1]

    def request(d_ref, slot):
        def issue(r, c):
            for k in range(MOE_SLOTS):
                row = d_ref[0, 0, MOE_SLOTS * r + k]
                pltpu.make_async_copy(ys_ref.at[pl.ds(row, 1)], buf.at[slot, k, pl.ds(r, 1)],
                                      sem.at[slot]).start(priority=k)
            return c

        lax.fori_loop(0, tq, issue, 0, unroll=8)

    @pl.when(s == 0)
    def _():
        request(dest_ref, 0)

    @pl.when(s + 1 < n_steps)
    def _():
        request(dest_next_ref, 1 - s % 2)

    slot = s % 2
    pltpu.make_async_copy(buf.at[slot], buf.at[slot], sem.at[slot]).wait()
    cw = cw_ref[0]
    y = cw[:, 0:1] * buf[slot, 0] + cw[:, 1:2] * buf[slot, 1]
    o_ref[0] = x1_ref[0] + gt_ref[0] * y


def _combine(dest3, ys, x1, cw, gt2):
    b, s, d = x1.shape
    tq = TOKEN_TILE
    spb = s // tq
    n_steps = b * spb
    row = lambda t: (t // spb, t % spb, 0)
    idx = lambda t: (t, 0, 0)
    idx_next = lambda t: (jnp.minimum(t + 1, n_steps - 1), 0, 0)
    return pl.pallas_call(
        _combine_kernel,
        grid=(n_steps,),
        in_specs=[pl.BlockSpec((1, 1, MOE_SLOTS * tq), idx, memory_space=pltpu.SMEM),
                  pl.BlockSpec((1, 1, MOE_SLOTS * tq), idx_next, memory_space=pltpu.SMEM),
                  pl.BlockSpec(memory_space=pl.ANY),
                  pl.BlockSpec((1, tq, d), row),
                  pl.BlockSpec((1, tq, LANES), row),
                  pl.BlockSpec((1, 1, d), lambda t: (t // spb, 0, 0))],
        out_specs=pl.BlockSpec((1, tq, d), row),
        out_shape=jax.ShapeDtypeStruct((b, s, d), F32),
        scratch_shapes=[pltpu.VMEM((2, MOE_SLOTS, tq, d), F32), pltpu.SemaphoreType.DMA((2,))],
        compiler_params=_cparams("arbitrary"),
        name="moe_combine",
    )(dest3, dest3, ys, x1, cw, gt2)


def _moe(h2p, ids, cw, wg, wu, wd, x1, gt2):
    b, s, d = x1.shape
    n = b * s
    n_rows = _moe_max_tiles(n) * MOE_TILE
    dest, te = _plan(ids.reshape(n, LANES))
    dest3 = dest[:, :MOE_SLOTS].reshape(n // TOKEN_TILE, 1, MOE_SLOTS * TOKEN_TILE)
    tile_expert, n_tiles = te[:n_rows // MOE_TILE, 0], te[0, 1:2]
    hs = _dispatch(tile_expert, n_tiles, dest3, h2p.reshape(n, d // 2), n_rows)
    ys = _experts(tile_expert, n_tiles, hs, wg, wu, wd)
    return _combine(dest3, ys, x1, cw, gt2)


def _alibi_slopes():
    n = SWA_HEADS + DSA_HEADS
    i = jnp.arange(1, n + 1, dtype=F32)
    return jnp.exp2(-8.0 * i / n)


def _layer(x, mod, ln1_g, w_in, swa_q_norm, swa_k_norm, swa_sinks, dsa_q_norm, dsa_k_norm,
           w_branch_a, w_branch_b, w_out, ln2_g, w_group, b_group, w_expert, b_expert,
           w_gate_up, w_up, w_down, slopes):
    b, s, d = x.shape
    sh1, sc1, gt1, sh2, sc2, gt2 = [m[:, None, :] for m in jnp.split(mod, 6, axis=-1)]

    w_in_t = jnp.swapaxes(w_in, 0, 1)

    h1 = _ln_mod(x, ln1_g[None, :], sc1, sh1)
    h1f = h1.reshape(b * s, d)
    ones = lambda n: jnp.ones((n,), F32)
    mult = jnp.concatenate([
        jnp.tile(swa_q_norm, SWA_HEADS) * HEAD_DIM ** -0.5, jnp.tile(swa_k_norm, SWA_KV_HEADS),
        ones(SWA_KV_HEADS * HEAD_DIM),
        jnp.tile(dsa_q_norm, DSA_HEADS) * HEAD_DIM ** -0.5, dsa_k_norm, ones(HEAD_DIM),
        ones(IDX_HEADS * IDX_DIM) * IDX_DIM ** -0.5, ones(IDX_DIM), ones(IDX_HEADS) * IDX_HEADS ** -0.5,
        ones(ATT_WIDTH - COL_GATE)])[None, :]
    col = jnp.arange(ATT_WIDTH)
    normed = (((col >= COL_QA) & (col < COL_VA)) | ((col >= COL_QB) & (col < COL_KVB + KVB_V))
              ).astype(F32)[None, :]
    proj = _proj(h1f, w_in_t, mult, normed, 1024, 768).reshape(b, s, ATT_WIDTH)
    oa = _swa(proj, swa_sinks, slopes[:SWA_HEADS])
    ob = _dsa(proj, slopes[SWA_HEADS:])
    merged = _merge(h1f, oa.reshape(b * s, -1), ob.reshape(b * s, -1), w_in_t,
                    w_branch_a, w_branch_b)

    w_r = jnp.concatenate([w_expert, w_group,
                           jnp.zeros((d, LANES - N_EXPERTS - N_GROUPS), F32)], axis=1)
    b_r = jnp.concatenate([b_expert, b_group,
                           jnp.zeros((LANES - N_EXPERTS - N_GROUPS,), F32)])[None, :]
    w_r_hi = _bf16_head(w_r)
    w_r_split = jnp.concatenate([w_r_hi.astype(BF16), (w_r - w_r_hi).astype(BF16)], axis=1)
    x1, h2p, ids, cw = _outproj(merged.reshape(b, s, d), w_out, x, gt1,
                                ln2_g[None, :], sc2, sh2, w_r_split, b_r)
    f = w_gate_up.shape[-1]
    return _moe(h2p, ids, cw, w_gate_up.reshape(N_EXPERTS, d, f), w_up.reshape(N_EXPERTS, d, f),
                w_down.reshape(N_EXPERTS, f, d), x1, gt2)


def kernel(x, c, w_ada, b_ada, ln1_g, w_in, swa_q_norm, swa_k_norm, swa_sinks, dsa_q_norm, dsa_k_norm,
           w_branch_a, w_branch_b, w_out, ln2_g, w_group, b_group, w_expert, b_expert,
           w_gate_up, w_up, w_down):
    slopes = _alibi_slopes()
    bsz = c.shape[0]
    c_pad = jnp.concatenate([c, jnp.zeros((ADA_ROWS - bsz, c.shape[1]), c.dtype)], axis=0)
    for l in range(w_ada.shape[0]):
        mod = _ada(c_pad, w_ada[l], b_ada[l][None, :])[:bsz]
        x = _layer(x, mod, ln1_g[l], w_in[l], swa_q_norm[l], swa_k_norm[l], swa_sinks[l],
                   dsa_q_norm[l], dsa_k_norm[l], w_branch_a[l], w_branch_b[l], w_out[l], ln2_g[l],
                   w_group[l], b_group[l], w_expert[l], b_expert[l],
                   w_gate_up[l], w_up[l], w_down[l], slopes)
    return x
```

```python
import functools

import jax
import jax.numpy as jnp
from jax import lax
from jax.experimental import pallas as pl
from jax.experimental.pallas import tpu as pltpu

F32 = jnp.float32
BF16 = jnp.bfloat16
I32 = jnp.int32

D_MODEL = 2048
HEAD_DIM = 64
SWA_HEADS = 16
SWA_KV_HEADS = 4
SWA_GROUP = SWA_HEADS // SWA_KV_HEADS
WINDOW = 128
DSA_HEADS = 16
IDX_HEADS = 16
IDX_DIM = 64
DSA_TOPK_MAX = 256
DSA_BLOCK = 128
N_GROUPS = 4
EXPERTS_PER_GROUP = 8
N_EXPERTS = N_GROUPS * EXPERTS_PER_GROUP
D_EXPERT = 512
RMS_EPS = 1e-6
NEG_INF = -1e30
INT_MIN = -(2 ** 31)

LANES = 128
VMEM_LIMIT = 56 * 1024 * 1024

COL_QA, COL_KA, COL_VA, COL_QB = 0, 1024, 1280, 1536
COL_KVB = 2560
COL_QI = 2688
COL_KWI = 3712
COL_GATE = 3792
ATT_WIDTH = 3840
KVB_K, KVB_V = 0, HEAD_DIM
KWI_K, KWI_W = 0, IDX_DIM

NT_DIMS = (((1,), (1,)), ((), ()))


def _cparams(*sem):
    return pltpu.CompilerParams(dimension_semantics=sem, vmem_limit_bytes=VMEM_LIMIT)


def _sigmoid(x):
    return 1.0 / (1.0 + jnp.exp(-x))


def _rms(x, g):
    return x * lax.rsqrt(jnp.mean(x * x, axis=-1, keepdims=True) + RMS_EPS) * g


def _ada_kernel(c_ref, w_ref, b_ref, o_ref):
    c = c_ref[...]
    cond = c * _sigmoid(c)
    c_hi = cond.astype(BF16)
    c_lo = (cond - c_hi.astype(F32)).astype(BF16)
    w = w_ref[...]
    w_hi = w.astype(BF16)
    w_lo = (w - w_hi.astype(F32)).astype(BF16)
    o_ref[...] = (jnp.dot(c_hi, w_hi, preferred_element_type=F32)
                  + jnp.dot(c_lo, w_hi, preferred_element_type=F32)
                  + jnp.dot(c_hi, w_lo, preferred_element_type=F32)) + b_ref[...]


ADA_ROWS = 16


def _ada(c_pad, w, b):
    n = w.shape[1]
    tn = 1024
    return pl.pallas_call(
        _ada_kernel,
        grid=(n // tn,),
        in_specs=[pl.BlockSpec((ADA_ROWS, D_MODEL), lambda j: (0, 0)),
                  pl.BlockSpec((D_MODEL, tn), lambda j: (0, j)),
                  pl.BlockSpec((1, tn), lambda j: (0, j))],
        out_specs=pl.BlockSpec((ADA_ROWS, tn), lambda j: (0, j)),
        out_shape=jax.ShapeDtypeStruct((ADA_ROWS, n), F32),
        compiler_params=_cparams("parallel"),
        name="ada_mod",
    )(c_pad, w, b)


def _ln_mod_kernel(x_ref, g_ref, sc_ref, sh_ref, o_ref):
    y = _rms(x_ref[0], g_ref[...])
    o_ref[0] = (y * (1.0 + sc_ref[0]) + sh_ref[0]).astype(o_ref.dtype)


def _ln_mod(x, g, sc, sh):
    b, s, d = x.shape
    ts = 512
    return pl.pallas_call(
        _ln_mod_kernel,
        grid=(b, s // ts),
        in_specs=[pl.BlockSpec((1, ts, d), lambda i, j: (i, j, 0)),
                  pl.BlockSpec((1, d), lambda i, j: (0, 0)),
                  pl.BlockSpec((1, 1, d), lambda i, j: (i, 0, 0)),
                  pl.BlockSpec((1, 1, d), lambda i, j: (i, 0, 0))],
        out_specs=pl.BlockSpec((1, ts, d), lambda i, j: (i, j, 0)),
        out_shape=jax.ShapeDtypeStruct((b, s, d), BF16),
        compiler_params=_cparams("parallel", "parallel"),
        name="ln_mod",
    )(x, g, sc, sh)


PROJ_ROW_PARTS = 4


def _proj_kernel(a_ref, wt_ref, mult_ref, normed_ref, o_ref):
    w = wt_ref[...].astype(BF16)
    tm, tn = o_ref.shape
    rows_per_part = tm // PROJ_ROW_PARTS
    half = lax.broadcasted_iota(I32, (rows_per_part, LANES), 1) < HEAD_DIM
    for part in range(PROJ_ROW_PARTS):
        rows = slice(part * rows_per_part, (part + 1) * rows_per_part)
        y = lax.dot_general(a_ref[rows, :], w, NT_DIMS, preferred_element_type=F32)
        for g in range(tn // LANES):
            cols = slice(g * LANES, (g + 1) * LANES)
            yg = y[:, cols]
            sq = yg * yg
            ms_lo = jnp.sum(jnp.where(half, sq, 0.0), axis=-1, keepdims=True) / HEAD_DIM
            ms_hi = jnp.sum(jnp.where(half, 0.0, sq), axis=-1, keepdims=True) / HEAD_DIM
            r = lax.rsqrt(jnp.where(half, ms_lo, ms_hi) + RMS_EPS)
            o_ref[rows, cols] = yg * jnp.where(normed_ref[:, cols] > 0.0, r, 1.0) * mult_ref[:, cols]


def _proj(a, wt, mult, normed, tm, tn):
    m, k = a.shape
    n = mult.shape[1]
    vec = pl.BlockSpec((1, tn), lambda i, j: (0, j))
    return pl.pallas_call(
        _proj_kernel,
        grid=(m // tm, n // tn),
        in_specs=[pl.BlockSpec((tm, k), lambda i, j: (i, 0)),
                  pl.BlockSpec((tn, k), lambda i, j: (j, 0)),
                  vec, vec],
        out_specs=pl.BlockSpec((tm, tn), lambda i, j: (i, j)),
        out_shape=jax.ShapeDtypeStruct((m, n), F32),
        compiler_params=_cparams("parallel", "parallel"),
        name="proj_att",
    )(a, wt, mult, normed)


def _swa_kernel(sink_ref, slope_ref, q_ref, kp_ref, kc_ref, vp_ref, vc_ref, o_ref, lhs, e_s):
    n = pl.program_id(1)
    w, w2 = WINDOW, 2 * WINDOW
    q = q_ref[0]
    kw = jnp.concatenate([kp_ref[0], kc_ref[0]], axis=0)
    vw = jnp.concatenate([vp_ref[0], vc_ref[0]], axis=0)
    half_q = lax.broadcasted_iota(I32, (w, LANES), 1) < HEAD_DIM
    half_k = lax.broadcasted_iota(I32, (w2, LANES), 1) < HEAD_DIM
    row = lax.broadcasted_iota(I32, (w, w2), 0)
    col = lax.broadcasted_iota(I32, (w, w2), 1)
    rel = row + w - col
    valid = (rel >= 0) & (rel < w) & ((col >= w) | (n > 0))
    mbias = jnp.where(valid, 0.0, NEG_INF)
    relf = rel.astype(F32)

    ones_col = jnp.ones((w2, LANES), BF16)
    for g in range(SWA_KV_HEADS):
        if g % 2 == 0:
            grp = slice((g // 2) * LANES, (g // 2 + 1) * LANES)
            kn = kw[:, grp]
            kn_sw = pltpu.roll(kn, HEAD_DIM, axis=1)
            vp = vw[:, grp]
            vp_sw = pltpu.roll(vp, HEAD_DIM, axis=1)
        kdup = jnp.where(half_k, kn, kn_sw) if g % 2 == 0 else jnp.where(half_k, kn_sw, kn)
        vdup = jnp.where(half_k, vp, vp_sw) if g % 2 == 0 else jnp.where(half_k, vp_sw, vp)
        vx = jnp.concatenate([vdup.astype(BF16), ones_col], axis=1)
        for r in range(SWA_GROUP):
            h = g * SWA_GROUP + r
            if r % 2 == 0:
                qn = q[:, (h // 2) * LANES:(h // 2 + 1) * LANES]
            keep = half_q if r % 2 == 0 else jnp.logical_not(half_q)
            lhs[r * w:(r + 1) * w, :] = jnp.where(keep, qn, 0.0).astype(BF16)
        s_all = lax.dot_general(lhs[...], kdup.astype(BF16), NT_DIMS, preferred_element_type=F32)
        sink_e = []
        for r in range(SWA_GROUP):
            h = g * SWA_GROUP + r
            rows = slice(r * w, (r + 1) * w)
            s = s_all[rows, :] + (mbias - slope_ref[h] * relf)
            sink = jnp.full((w, LANES), sink_ref[h], F32)
            m = jnp.maximum(jnp.max(s, axis=-1, keepdims=True), sink)
            e_s[rows, :] = jnp.exp(s - jnp.concatenate([m, m], axis=1)).astype(BF16)
            sink_e.append(jnp.exp(sink - m))
        pv = jnp.dot(e_s[...], vx, preferred_element_type=F32)
        for r in range(0, SWA_GROUP, 2):
            h = g * SWA_GROUP + r
            ev, od = pv[r * w:(r + 1) * w, :], pv[(r + 1) * w:(r + 2) * w, :]
            den = jnp.where(half_q, ev[:, LANES:] + sink_e[r], od[:, LANES:] + sink_e[r + 1])
            o = jnp.where(half_q, ev[:, :LANES], od[:, :LANES]) / den
            o_ref[0, :, (h // 2) * LANES:(h // 2 + 1) * LANES] = o.astype(o_ref.dtype)


def _swa(proj3, sinks, slopes):
    b, s, _ = proj3.shape
    nb = s // WINDOW
    kvw = SWA_KV_HEADS * HEAD_DIM
    qw = SWA_HEADS * HEAD_DIM
    smem = pl.BlockSpec(memory_space=pltpu.SMEM)
    prev =lambda i, n: (i, jnp.maximum(n - 1, 0), COL_KA // kvw)
    cur = lambda i, n: (i, n, COL_KA // kvw)
    prev_v = lambda i, n: (i, jnp.maximum(n - 1, 0), COL_VA // kvw)
    cur_v = lambda i, n: (i, n, COL_VA // kvw)
    return pl.pallas_call(
        _swa_kernel,
        grid=(b, nb),
        in_specs=[smem, smem,
                  pl.BlockSpec((1, WINDOW, qw), lambda i, n: (i, n, COL_QA // qw)),
                  pl.BlockSpec((1, WINDOW, kvw), prev),
                  pl.BlockSpec((1, WINDOW, kvw), cur),
                  pl.BlockSpec((1, WINDOW, kvw), prev_v),
                  pl.BlockSpec((1, WINDOW, kvw), cur_v)],
        out_specs=pl.BlockSpec((1, WINDOW, qw), lambda i, n: (i, n, 0)),
        out_shape=jax.ShapeDtypeStruct((b, s, qw), BF16),
        scratch_shapes=[pltpu.VMEM((SWA_GROUP * WINDOW, LANES), BF16),
                        pltpu.VMEM((SWA_GROUP * WINDOW, 2 * WINDOW), BF16)],
        compiler_params=_cparams("parallel", "parallel"),
        name="swa_attn",
    )(sinks, slopes, proj3, proj3, proj3, proj3, proj3)


DSA_CHUNK = 256
DSA_PART_HEADS = 4
ALIBI_SPLIT = 3
POS_LO_BITS = 7


def _bf16_head(x):
    return lax.bitcast_convert_type(lax.bitcast_convert_type(x, I32) & jnp.int32(-65536), F32)


def _alibi_columns(slopes):
    parts, rest = [], slopes.astype(F32)
    for _ in range(ALIBI_SPLIT):
        piece = _bf16_head(rest)
        parts.append(piece)
        rest = rest - piece
    cols = [p * float(2 ** POS_LO_BITS) for p in parts] + parts
    pad = jnp.zeros((slopes.shape[0], LANES - 2 * ALIBI_SPLIT), F32)
    return jnp.concatenate([jnp.stack(cols, axis=1), pad], axis=1)


def _dsa_kernel(*refs, topk):
    n_pairs = DSA_HEADS // 2
    qb_refs, qi_refs = refs[:n_pairs], refs[n_pairs:2 * n_pairs]
    (kwi_blk_ref, kvb_ref, kwi_ref, acol_ref, o_ref,
     ki2, kx, vx, lhs_i, lhs_q, wb, key_s, keyt_s, s_s, thr_s, acc_s, m_s) = refs[2 * n_pairs:]
    i = pl.program_id(1)
    blk, ck = DSA_BLOCK, DSA_CHUNK
    nch = (i + 2) // 2
    seq = ki2.shape[0]
    max_ch = seq // ck
    half = lax.broadcasted_iota(I32, (blk, LANES), 1) < HEAD_DIM

    @pl.when(i == 0)
    def _():
        kvb = kvb_ref[0]
        kib = kwi_ref[0][:, KWI_K:KWI_K + IDX_DIM].astype(BF16)
        ki2[...] = jnp.concatenate([kib, kib], axis=1)
        kn = kvb[:, KVB_K:KVB_K + HEAD_DIM].astype(BF16)
        pos = lax.broadcasted_iota(I32, (seq, LANES), 0)
        lane = lax.broadcasted_iota(I32, (seq, LANES), 1)
        hi = lax.shift_right_logical(pos, POS_LO_BITS).astype(F32)
        lo = (pos & (2 ** POS_LO_BITS - 1)).astype(F32)
        posc = jnp.where(lane < ALIBI_SPLIT, hi, jnp.where(lane < 2 * ALIBI_SPLIT, lo, 0.0)).astype(BF16)
        kx[...] = jnp.concatenate([kn, kn, posc], axis=1)
        ones = jnp.where(lax.broadcasted_iota(I32, (seq, HEAD_DIM), 1) == 0, 1.0, 0.0).astype(BF16)
        vx[...] = jnp.concatenate([kvb[:, KVB_V:KVB_V + HEAD_DIM].astype(BF16), ones], axis=1)

    wi = kwi_blk_ref[0][:, KWI_W:KWI_W + IDX_HEADS]
    for p in range(n_pairs):
        qn = qb_refs[p][0]
        qip = qi_refs[p][0]
        for k in range(2):
            h = 2 * p + k
            keep = half if k == 0 else jnp.logical_not(half)
            rows = slice(h * blk, (h + 1) * blk)
            lhs_i[rows, :] = jnp.where(keep, qip, 0.0).astype(BF16)
            lhs_q[rows, :LANES] = jnp.where(keep, qn, 0.0).astype(BF16)
            lhs_q[rows, LANES:] = jnp.broadcast_to(acol_ref[h:h + 1, :], (blk, LANES)).astype(BF16)
            wb[h] = jnp.broadcast_to(wi[:, h:h + 1], (blk, LANES))

    row_t = i * blk + lax.broadcasted_iota(I32, (blk, ck), 0)
    col_k = lax.broadcasted_iota(I32, (blk, ck), 1)

    def chunk_rows(c):
        return pl.ds(c * ck if isinstance(c, int) else pl.multiple_of(c * ck, ck), ck)

    def score_chunk(c, carry):
        ks = chunk_rows(c)
        kic = ki2[ks, :]
        sc = jnp.zeros((blk, ck), F32)
        for part in range(0, IDX_HEADS, DSA_PART_HEADS):
            logits = lax.dot_general(lhs_i[part * blk:(part + DSA_PART_HEADS) * blk, :], kic, NT_DIMS,
                                     preferred_element_type=F32)
            for k in range(DSA_PART_HEADS):
                w = wb[part + k]
                lh = jnp.maximum(logits[k * blk:(k + 1) * blk, :], 0.0)
                sc = sc + lh * jnp.concatenate([w, w], axis=1)
        bits = lax.bitcast_convert_type(sc + 0.0, I32)
        key = jnp.where(bits < 0, bits ^ jnp.int32(0x7FFFFFFF), bits)
        key = jnp.where(c * ck + col_k <= row_t, key, jnp.int32(INT_MIN))
        key_s[c] = key
        keyt_s[c] = key.T
        return carry

    lax.fori_loop(0, nch, score_chunk, 0)

    thr_s[...] = jnp.full(thr_s.shape, INT_MIN, I32)
    for n in range(2, max_ch + 1):
        @pl.when(nch == n)
        def _(n=n):
            def step(it, thr):
                cand = thr + lax.shift_left(jnp.int32(1), 31 - it)
                cnt8 = jnp.zeros((8, blk), F32)
                for c in range(n):
                    hit = jnp.where(keyt_s[c] >= cand, 1.0, 0.0)
                    cnt8 = cnt8 + jnp.sum(hit.reshape(ck // 8, 8, blk), axis=0)
                cnt = jnp.sum(cnt8, axis=0, keepdims=True)
                return jnp.where(cnt >= float(topk), cand, thr)

            thr = lax.fori_loop(0, 32, step, jnp.full((1, blk), INT_MIN, I32))
            thr_s[...] = jnp.broadcast_to(thr, thr_s.shape).T

    thr = thr_s[...]
    thr2 = jnp.concatenate([thr, thr], axis=1)

    def count_gt(c, cnt):
        k = key_s[c]
        return cnt + jnp.where(k[:, :LANES] > thr, 1.0, 0.0) + jnp.where(k[:, LANES:] > thr, 1.0, 0.0)

    n_gt = jnp.sum(lax.fori_loop(0, nch, count_gt, jnp.zeros((blk, LANES), F32)), axis=-1, keepdims=True)
    need = float(topk) - n_gt
    upper = (lax.broadcasted_iota(I32, (ck, ck), 0) < lax.broadcasted_iota(I32, (ck, ck), 1)).astype(BF16)

    def masked_scores(c, n_eq, first=False):
        ks = chunk_rows(c)
        k = key_s[c]
        eq = (k == thr2) & (c * ck + col_k <= row_t)
        eqf = eq.astype(BF16)
        before = jnp.dot(eqf, upper, preferred_element_type=F32) + n_eq
        sel = (k > thr2) | (eq & (before < need))
        mb = jnp.where(sel, 0.0, NEG_INF)
        kxc = kx[ks, :]
        for part in range(0, DSA_HEADS, DSA_PART_HEADS):
            s_part = lax.dot_general(lhs_q[part * blk:(part + DSA_PART_HEADS) * blk, :], kxc, NT_DIMS,
                                     preferred_element_type=F32)
            for k in range(DSA_PART_HEADS):
                rows = slice((part + k) * blk, (part + k + 1) * blk)
                s = s_part[k * blk:(k + 1) * blk, :] + mb
                s_s[c, rows, :] = s
                m_chunk = jnp.maximum(s[:, :LANES], s[:, LANES:])
                m_s[rows, :] = m_chunk if first else jnp.maximum(m_s[rows, :], m_chunk)
        return n_eq + jnp.sum(eqf.astype(F32), axis=-1, keepdims=True)

    n_eq0 = masked_scores(0, jnp.zeros((blk, 1), F32), first=True)
    lax.fori_loop(1, nch, masked_scores, n_eq0)

    m_s[...] = jnp.broadcast_to(jnp.max(m_s[...], axis=-1, keepdims=True), m_s.shape)

    def attend(c, carry, first=False):
        vxc = vx[chunk_rows(c), :]
        for part in range(0, DSA_HEADS, DSA_PART_HEADS):
            e = []
            for k in range(DSA_PART_HEADS):
                rows = slice((part + k) * blk, (part + k + 1) * blk)
                m = m_s[rows, :]
                e.append(jnp.exp(s_s[c, rows, :] - jnp.concatenate([m, m], axis=1)).astype(BF16))
            rows = slice(part * blk, (part + DSA_PART_HEADS) * blk)
            pv = jnp.dot(jnp.concatenate(e, axis=0), vxc, preferred_element_type=F32)
            acc_s[rows, :] = pv if first else acc_s[rows, :] + pv
        return carry

    attend(0, 0, first=True)
    lax.fori_loop(1, nch, attend, 0)

    for h in range(DSA_HEADS):
        a = acc_s[h * blk:(h + 1) * blk, :]
        o = a[:, :HEAD_DIM] / a[:, HEAD_DIM:HEAD_DIM + 1]
        o_ref[0, :, h * HEAD_DIM:(h + 1) * HEAD_DIM] = o.astype(o_ref.dtype)


def _dsa(proj3, slopes):
    b, s, _ = proj3.shape
    nb = s // DSA_BLOCK
    qw = DSA_HEADS * HEAD_DIM
    n_pairs = DSA_HEADS // 2
    rows = DSA_HEADS * DSA_BLOCK
    fixed = lambda i, n: (0, 0)

    def group(col):
        return pl.BlockSpec((1, DSA_BLOCK, LANES), lambda i, n: (i, n, col // LANES))

    def group_all(col):
        return pl.BlockSpec((1, s, LANES), lambda i, n: (i, 0, col // LANES))

    return pl.pallas_call(
        functools.partial(_dsa_kernel, topk=min(DSA_TOPK_MAX, s // 4)),
        grid=(b, nb),
        in_specs=([group(COL_QB + p * LANES) for p in range(n_pairs)]
                  + [group(COL_QI + p * LANES) for p in range(n_pairs)]
                  + [group(COL_KWI), group_all(COL_KVB), group_all(COL_KWI),
                     pl.BlockSpec((DSA_HEADS, LANES), fixed)]),
        out_specs=pl.BlockSpec((1, DSA_BLOCK, qw), lambda i, n: (i, n, 0)),
        out_shape=jax.ShapeDtypeStruct((b, s, qw), BF16),
        scratch_shapes=[pltpu.VMEM((s, LANES), BF16),
                        pltpu.VMEM((s, 2 * LANES), BF16),
                        pltpu.VMEM((s, LANES), BF16),
                        pltpu.VMEM((rows, LANES), BF16),
                        pltpu.VMEM((rows, 2 * LANES), BF16),
                        pltpu.VMEM((IDX_HEADS, DSA_BLOCK, LANES), F32),
                        pltpu.VMEM((s // DSA_CHUNK, DSA_BLOCK, DSA_CHUNK), I32),
                        pltpu.VMEM((s // DSA_CHUNK, DSA_CHUNK, DSA_BLOCK), I32),
                        pltpu.VMEM((s // DSA_CHUNK, rows, DSA_CHUNK), F32),
                        pltpu.VMEM((DSA_BLOCK, LANES), I32),
                        pltpu.VMEM((rows, LANES), F32),
                        pltpu.VMEM((rows, LANES), F32)],
        compiler_params=_cparams("arbitrary", "arbitrary"),
        name="dsa_attn",
    )(*([proj3] * (2 * n_pairs + 3)), _alibi_columns(slopes))


def _merge_kernel(h_ref, oa_ref, ob_ref, wt_hbm, wa_ref, wb_ref, o_ref, g32, gbf, sem):
    j, i = pl.program_id(0), pl.program_id(1)
    tn, d = o_ref.shape[1], h_ref.shape[1]

    def gate_copies(jj, slot):
        return [pltpu.make_async_copy(
            wt_hbm.at[pl.ds(pl.multiple_of(COL_GATE + k * d + jj * tn, 8), tn)], g32.at[slot, k], sem.at[slot, k])
            for k in range(2)]

    @pl.when((j == 0) & (i == 0))
    def _():
        for c in gate_copies(0, 0):
            c.start()

    @pl.when(i == 0)
    def _():
        slot = j % 2
        for c in gate_copies(j, slot):
            c.wait()
        gbf[...] = g32[slot].astype(BF16)

        @pl.when(j + 1 < pl.num_programs(0))
        def _():
            for c in gate_copies(j + 1, 1 - slot):
                c.start()

    h = h_ref[...]
    ga = lax.dot_general(h, gbf[0], NT_DIMS, preferred_element_type=F32)
    gb = lax.dot_general(h, gbf[1], NT_DIMS, preferred_element_type=F32)
    a = jnp.dot(oa_ref[...], wa_ref[...].astype(BF16), preferred_element_type=F32)
    b = jnp.dot(ob_ref[...], wb_ref[...].astype(BF16), preferred_element_type=F32)
    o_ref[...] = (_sigmoid(ga) * a + _sigmoid(gb) * b).astype(o_ref.dtype)


def _merge(h, oa, ob, w_in_t, wa, wb):
    m, d = h.shape
    tm, tn = 1024, 512
    return pl.pallas_call(
        _merge_kernel,
        grid=(d // tn, m // tm),
        in_specs=[pl.BlockSpec((tm, d), lambda j, i: (i, 0)),
                  pl.BlockSpec((tm, oa.shape[1]), lambda j, i: (i, 0)),
                  pl.BlockSpec((tm, ob.shape[1]), lambda j, i: (i, 0)),
                  pl.BlockSpec(memory_space=pl.ANY),
                  pl.BlockSpec((wa.shape[0], tn), lambda j, i: (0, j)),
                  pl.BlockSpec((wb.shape[0], tn), lambda j, i: (0, j))],
        out_specs=pl.BlockSpec((tm, tn), lambda j, i: (i, j)),
        out_shape=jax.ShapeDtypeStruct((m, d), BF16),
        scratch_shapes=[pltpu.VMEM((2, 2, tn, d), w_in_t.dtype),
                        pltpu.VMEM((2, tn, d), BF16),
                        pltpu.SemaphoreType.DMA((2, 2))],
        compiler_params=_cparams("arbitrary", "arbitrary"),
        name="merge_branches",
    )(h, oa, ob, w_in_t, wa, wb)


MOE_TILE = 256
MOE_SLOTS = 2
TOKEN_TILE = 512


def _moe_max_tiles(n_tokens):
    return -(-(n_tokens * MOE_SLOTS + N_EXPERTS * (MOE_TILE - 1)) // MOE_TILE)


def _pack_bf16_pair(x):
    k = x.shape[1] // 2
    bits = lax.bitcast_convert_type(x.astype(BF16).astype(F32), I32)
    return lax.shift_right_logical(bits[:, :k], 16) | (bits[:, k:] & jnp.int32(-65536))


def _unpack_bf16_pair(w):
    lo = lax.bitcast_convert_type(lax.shift_left(w, 16), F32).astype(BF16)
    hi = lax.bitcast_convert_type(w & jnp.int32(-65536), F32).astype(BF16)
    return lo, hi


def _route(logits):
    lane = lax.broadcasted_iota(I32, logits.shape, 1)
    is_g = (lane >= N_EXPERTS) & (lane < N_EXPERTS + N_GROUPS)
    neg = -jnp.inf
    gl = jnp.where(is_g, logits, neg)
    gmax = jnp.max(gl, axis=-1, keepdims=True)
    g_idx = jnp.min(jnp.where(gl == gmax, lane - N_EXPERTS, N_GROUPS), axis=-1, keepdims=True)
    p_g = 1.0 / jnp.sum(jnp.where(is_g, jnp.exp(gl - gmax), 0.0), axis=-1, keepdims=True)
    in_grp = (lane < N_EXPERTS) & ((lane // EXPERTS_PER_GROUP) == g_idx)
    el = jnp.where(in_grp, logits, neg)
    m1 = jnp.max(el, axis=-1, keepdims=True)
    i1 = jnp.min(jnp.where(el == m1, lane, LANES), axis=-1, keepdims=True)
    el2 = jnp.where(lane == i1, neg, el)
    m2 = jnp.max(el2, axis=-1, keepdims=True)
    i2 = jnp.min(jnp.where(el2 == m2, lane, LANES), axis=-1, keepdims=True)
    e2 = jnp.exp(m2 - m1)
    w1 = 1.0 / (1.0 + e2)
    w2 = e2 / (1.0 + e2)
    ids = jnp.where(lane == 0, i1, jnp.where(lane == 1, i2, 0))
    return ids, p_g * jnp.where(lane == 0, w1, jnp.where(lane == 1, w2, 0.0))


def _outproj_kernel(mg_ref, wo_ref, x_ref, gt_ref, g2_ref, sc_ref, sh_ref, wr_ref, br_ref,
                    x1_ref, h2p_ref, ids_ref, cw_ref):
    y = jnp.dot(mg_ref[0], wo_ref[...].astype(BF16), preferred_element_type=F32)
    x1 = x_ref[0] + gt_ref[0] * y
    x1_ref[0] = x1
    h2 = _rms(x1, g2_ref[...]) * (1.0 + sc_ref[0]) + sh_ref[0]
    h2p_ref[0] = _pack_bf16_pair(h2)
    h_hi = h2.astype(BF16)
    h_lo = (h2 - h_hi.astype(F32)).astype(BF16)
    both = jnp.dot(h_hi, wr_ref[...], preferred_element_type=F32)
    logits = (both[:, :LANES] + jnp.dot(h_lo, wr_ref[:, :LANES], preferred_element_type=F32)
              + both[:, LANES:]) + br_ref[...]
    ids_ref[0], cw_ref[0] = _route(logits)


def _outproj(merged3, w_out, x, gt1, g2, sc2, sh2, w_r, b_r):
    b, s, d = x.shape
    tm = 512
    row = lambda i, j: (i, j, 0)
    per_b = lambda i, j: (i, 0, 0)
    fixed = lambda i, j: (0, 0)
    return pl.pallas_call(
        _outproj_kernel,
        grid=(b, s // tm),
        in_specs=[pl.BlockSpec((1, tm, d), row),
                  pl.BlockSpec((d, d), fixed, pipeline_mode=pl.Buffered(1)),
                  pl.BlockSpec((1, tm, d), row),
                  pl.BlockSpec((1, 1, d), per_b),
                  pl.BlockSpec((1, d), fixed),
                  pl.BlockSpec((1, 1, d), per_b),
                  pl.BlockSpec((1, 1, d), per_b),
                  pl.BlockSpec((d, 2 * LANES), fixed),
                  pl.BlockSpec((1, LANES), fixed)],
        out_specs=[pl.BlockSpec((1, tm, d), row),
                   pl.BlockSpec((1, tm, d // 2), row),
                   pl.BlockSpec((1, tm, LANES), row),
                   pl.BlockSpec((1, tm, LANES), row)],
        out_shape=[jax.ShapeDtypeStruct((b, s, d), F32),
                   jax.ShapeDtypeStruct((b, s, d // 2), I32),
                   jax.ShapeDtypeStruct((b, s, LANES), I32),
                   jax.ShapeDtypeStruct((b, s, LANES), F32)],
        compiler_params=_cparams("parallel", "parallel"),
        name="outproj_ln2_router",
    )(merged3, w_out, x, gt1, g2, sc2, sh2, w_r, b_r)


def _plan_kernel(ids_ref, dest_ref, te_ref, rank_ref):
    n = ids_ref.shape[0]
    cb = 256
    lane = lax.broadcasted_iota(I32, (cb, LANES), 1)
    lower = (lax.broadcasted_iota(I32, (cb, cb), 0) > lax.broadcasted_iota(I32, (cb, cb), 1)).astype(BF16)

    def count(bi, carry):
        rows = pl.ds(pl.multiple_of(bi * cb, cb), cb)
        ids = ids_ref[rows, :]
        m1 = lane == ids[:, 0:1]
        m2 = lane == ids[:, 1:2]
        onehot = m1 | m2
        before = jnp.dot(lower, onehot.astype(BF16), preferred_element_type=F32) + carry
        r1 = jnp.sum(jnp.where(m1, before, 0.0), axis=-1, keepdims=True)
        r2 = jnp.sum(jnp.where(m2, before, 0.0), axis=-1, keepdims=True)
        rank_ref[rows, :] = jnp.where(lane == 0, r1, jnp.where(lane == 1, r2, 0.0))
        return carry + jnp.sum(onehot.astype(F32), axis=0, keepdims=True)

    counts = lax.fori_loop(0, n // cb, count, jnp.zeros((1, LANES), F32))
    padded = jnp.floor((counts + (MOE_TILE - 1)) / MOE_TILE) * MOE_TILE
    upper = (lax.broadcasted_iota(I32, (LANES, LANES), 0)
             < lax.broadcasted_iota(I32, (LANES, LANES), 1)).astype(F32)
    offs = jnp.dot(jnp.broadcast_to(padded, (8, LANES)), upper, preferred_element_type=F32,
                   precision=lax.Precision.HIGHEST)[0:1]

    def place(bi, c):
        rows = pl.ds(pl.multiple_of(bi * cb, cb), cb)
        ids = ids_ref[rows, :]
        o1 = jnp.sum(jnp.where(lane == ids[:, 0:1], offs, 0.0), axis=-1, keepdims=True)
        o2 = jnp.sum(jnp.where(lane == ids[:, 1:2], offs, 0.0), axis=-1, keepdims=True)
        slot = rank_ref[rows, :] + jnp.where(lane == 0, o1, jnp.where(lane == 1, o2, 0.0))
        dest_ref[rows, :] = slot.astype(I32)
        return c

    lax.fori_loop(0, n // cb, place, 0)

    sq_row = lax.broadcasted_iota(I32, (LANES, LANES), 0)
    sq_lane = lax.broadcasted_iota(I32, (LANES, LANES), 1)
    tile_start = sq_row.astype(F32) * MOE_TILE
    finished = ((offs + padded) <= tile_start) & (sq_lane < N_EXPERTS)
    tile_expert = jnp.minimum(jnp.sum(finished.astype(F32), axis=-1, keepdims=True), N_EXPERTS - 1.0)
    n_tiles = jnp.sum(padded, axis=-1, keepdims=True) / MOE_TILE
    te_ref[...] = jnp.where(sq_lane == 1, n_tiles, tile_expert).astype(I32)


def _plan(ids):
    n = ids.shape[0]
    return pl.pallas_call(
        _plan_kernel,
        out_shape=[jax.ShapeDtypeStruct((n, LANES), I32),
                   jax.ShapeDtypeStruct((LANES, LANES), I32)],
        scratch_shapes=[pltpu.VMEM((n, LANES), F32)],
        compiler_params=pltpu.CompilerParams(vmem_limit_bytes=VMEM_LIMIT),
        name="moe_plan",
    )(ids)


def _dispatch_kernel(te_ref, nt_ref, dest_ref, h_ref, hs_ref, zeros, sem, zsem):
    tq = h_ref.shape[0]

    @pl.when(pl.program_id(0) == 0)
    def _():
        zeros[...] = jnp.zeros_like(zeros)
        n_tiles, nt = te_ref.shape[0], nt_ref[0]

        def has_padding(t):
            return (t >= nt - 1) | (te_ref[jnp.minimum(t + 1, n_tiles - 1)] != te_ref[t])

        def tile_copy(t):
            return pltpu.make_async_copy(zeros, hs_ref.at[pl.ds(pl.multiple_of(t * MOE_TILE, MOE_TILE), MOE_TILE)], zsem)

        def start(t, c):
            @pl.when(has_padding(t))
            def _():
                tile_copy(t).start()
            return c

        def wait(t, c):
            @pl.when(has_padding(t))
            def _():
                tile_copy(t).wait()
            return c

        lax.fori_loop(0, n_tiles, start, 0)
        lax.fori_loop(0, n_tiles, wait, 0)

    def issue(r, c):
        for k in range(MOE_SLOTS):
            slot = dest_ref[0, 0, MOE_SLOTS * r + k]
            pltpu.make_async_copy(h_ref.at[pl.ds(r, 1)], hs_ref.at[pl.ds(slot, 1)], sem).start(priority=k)
        return c

    lax.fori_loop(0, tq, issue, 0, unroll=8)
    done = hs_ref.at[pl.ds(0, MOE_SLOTS * tq)]
    pltpu.make_async_copy(done, done, sem).wait()


def _dispatch(tile_expert, n_tiles, dest3, h2p, n_rows):
    n, w = h2p.shape
    tq = TOKEN_TILE
    return pl.pallas_call(
        _dispatch_kernel,
        grid_spec=pltpu.PrefetchScalarGridSpec(
            num_scalar_prefetch=2,
            grid=(n // tq,),
            in_specs=[pl.BlockSpec((1, 1, MOE_SLOTS * tq), lambda i, te, nt: (i, 0, 0),
                                   memory_space=pltpu.SMEM),
                      pl.BlockSpec((tq, w), lambda i, te, nt: (i, 0))],
            out_specs=pl.BlockSpec(memory_space=pl.ANY),
            scratch_shapes=[pltpu.VMEM((MOE_TILE, w), h2p.dtype),
                            pltpu.SemaphoreType.DMA(()),
                            pltpu.SemaphoreType.DMA(())]),
        out_shape=jax.ShapeDtypeStruct((n_rows, w), h2p.dtype),
        compiler_params=_cparams("arbitrary"),
        name="moe_dispatch",
    )(tile_expert, n_tiles, dest3, h2p)


def _expert_kernel(te_ref, nt_ref, hs_ref, wg_hbm, wu_hbm, wd_hbm, ys_ref,
                   wg_buf, wu_buf, wd_buf, sem, slot_ref):
    t = pl.program_id(0)
    nt = nt_ref[0]
    used = t < nt
    last = te_ref.shape[0] - 1

    def weight_copies(e, slot):
        return (pltpu.make_async_copy(wg_hbm.at[e], wg_buf.at[slot], sem.at[0, slot]),
                pltpu.make_async_copy(wu_hbm.at[e], wu_buf.at[slot], sem.at[1, slot]),
                pltpu.make_async_copy(wd_hbm.at[e], wd_buf.at[slot], sem.at[2, slot]))

    @pl.when(t == 0)
    def _():
        slot_ref[0] = 0
        for c in weight_copies(te_ref[0], 0):
            c.start()

    @pl.when(jnp.logical_not(used))
    def _():
        ys_ref[...] = jnp.zeros_like(ys_ref)

    @pl.when(used)
    def _():
        e = te_ref[t]
        first = (t == 0) | (te_ref[jnp.maximum(t - 1, 0)] != e)

        @pl.when(first & (t > 0))
        def _():
            slot_ref[0] = 1 - slot_ref[0]

        slot = slot_ref[0]

        @pl.when(first)
        def _():
            for c in weight_copies(e, slot):
                c.wait()
            nxt = lax.while_loop(lambda s: (s < nt) & (te_ref[jnp.minimum(s, last)] == e),
                                 lambda s: s + 1, t + 1)

            @pl.when(nxt < nt)
            def _():
                for c in weight_copies(te_ref[jnp.minimum(nxt, last)], 1 - slot):
                    c.start()

        lo, hi = _unpack_bf16_pair(hs_ref[...])
        k = lo.shape[1]

        def up(w_buf):
            return (jnp.dot(lo, w_buf[slot, :k, :].astype(BF16), preferred_element_type=F32)
                    + jnp.dot(hi, w_buf[slot, k:, :].astype(BF16), preferred_element_type=F32))

        a = up(wg_buf)
        u = up(wu_buf)
        act = (a * _sigmoid(a) * u).astype(BF16)
        ys_ref[...] = jnp.dot(act, wd_buf[slot].astype(BF16), preferred_element_type=F32)


def _experts(tile_expert, n_tiles, hs, wg, wu, wd):
    n_rows, hw = hs.shape
    d, f = wg.shape[1], wg.shape[2]
    row = lambda t, te, nt: (jnp.minimum(t, nt[0] - 1), 0)
    hbm = pl.BlockSpec(memory_space=pl.ANY)
    return pl.pallas_call(
        _expert_kernel,
        grid_spec=pltpu.PrefetchScalarGridSpec(
            num_scalar_prefetch=2,
            grid=(n_rows // MOE_TILE,),
            in_specs=[pl.BlockSpec((MOE_TILE, hw), row), hbm, hbm, hbm],
            out_specs=pl.BlockSpec((MOE_TILE, d), lambda t, te, nt: (t, 0)),
            scratch_shapes=[pltpu.VMEM((2, d, f), wg.dtype),
                            pltpu.VMEM((2, d, f), wu.dtype),
                            pltpu.VMEM((2, f, d), wd.dtype),
                            pltpu.SemaphoreType.DMA((3, 2)),
                            pltpu.SMEM((1,), I32)]),
        out_shape=jax.ShapeDtypeStruct((n_rows, d), F32),
        compiler_params=_cparams("arbitrary"),
        name="moe_experts",
    )(tile_expert, n_tiles, hs, wg, wu, wd)


def _combine_kernel(dest_ref, dest_next_ref, ys_ref, x1_ref, cw_ref, gt_ref, o_ref, buf, sem):
    s, n_steps = pl.program_id(0), pl.num_programs(0)
    tq = x1_ref.shape[1]

    def request(d_ref, slot):
        def issue(r, c):
            for k in range(MOE_SLOTS):
                row = d_ref[0, 0, MOE_SLOTS * r + k]
                pltpu.make_async_copy(ys_ref.at[pl.ds(row, 1)], buf.at[slot, k, pl.ds(r, 1)],
                                      sem.at[slot]).start(priority=k)
            return c

        lax.fori_loop(0, tq, issue, 0, unroll=8)

    @pl.when(s == 0)
    def _():
        request(dest_ref, 0)

    @pl.when(s + 1 < n_steps)
    def _():
        request(dest_next_ref, 1 - s % 2)

    slot = s % 2
    pltpu.make_async_copy(buf.at[slot], buf.at[slot], sem.at[slot]).wait()
    cw = cw_ref[0]
    y = cw[:, 0:1] * buf[slot, 0] + cw[:, 1:2] * buf[slot, 1]
    o_ref[0] = x1_ref[0] + gt_ref[0] * y


def _combine(dest3, ys, x1, cw, gt2):
    b, s, d = x1.shape
    tq = TOKEN_TILE
    spb = s // tq
    n_steps = b * spb
    row = lambda t: (t // spb, t % spb, 0)
    idx = lambda t: (t, 0, 0)
    idx_next = lambda t: (jnp.minimum(t + 1, n_steps - 1), 0, 0)
    return pl.pallas_call(
        _combine_kernel,
        grid=(n_steps,),
        in_specs=[pl.BlockSpec((1, 1, MOE_SLOTS * tq), idx, memory_space=pltpu.SMEM),
                  pl.BlockSpec((1, 1, MOE_SLOTS * tq), idx_next, memory_space=pltpu.SMEM),
                  pl.BlockSpec(memory_space=pl.ANY),
                  pl.BlockSpec((1, tq, d), row),
                  pl.BlockSpec((1, tq, LANES), row),
                  pl.BlockSpec((1, 1, d), lambda t: (t // spb, 0, 0))],
        out_specs=pl.BlockSpec((1, tq, d), row),
        out_shape=jax.ShapeDtypeStruct((b, s, d), F32),
        scratch_shapes=[pltpu.VMEM((2, MOE_SLOTS, tq, d), F32), pltpu.SemaphoreType.DMA((2,))],
        compiler_params=_cparams("arbitrary"),
        name="moe_combine",
    )(dest3, dest3, ys, x1, cw, gt2)


def _moe(h2p, ids, cw, wg, wu, wd, x1, gt2):
    b, s, d = x1.shape
    n = b * s
    n_rows = _moe_max_tiles(n) * MOE_TILE
    dest, te = _plan(ids.reshape(n, LANES))
    dest3 = dest[:, :MOE_SLOTS].reshape(n // TOKEN_TILE, 1, MOE_SLOTS * TOKEN_TILE)
    tile_expert, n_tiles = te[:n_rows // MOE_TILE, 0], te[0, 1:2]
    hs = _dispatch(tile_expert, n_tiles, dest3, h2p.reshape(n, d // 2), n_rows)
    ys = _experts(tile_expert, n_tiles, hs, wg, wu, wd)
    return _combine(dest3, ys, x1, cw, gt2)


def _alibi_slopes():
    n = SWA_HEADS + DSA_HEADS
    i = jnp.arange(1, n + 1, dtype=F32)
    return jnp.exp2(-8.0 * i / n)


def _layer(x, mod, ln1_g, w_in, swa_q_norm, swa_k_norm, swa_sinks, dsa_q_norm, dsa_k_norm,
           w_branch_a, w_branch_b, w_out, ln2_g, w_group, b_group, w_expert, b_expert,
           w_gate_up, w_up, w_down, slopes):
    b, s, d = x.shape
    sh1, sc1, gt1, sh2, sc2, gt2 = [m[:, None, :] for m in jnp.split(mod, 6, axis=-1)]

    w_in_t = jnp.swapaxes(w_in, 0, 1)

    h1 = _ln_mod(x, ln1_g[None, :], sc1, sh1)
    h1f = h1.reshape(b * s, d)
    ones = lambda n: jnp.ones((n,), F32)
    mult = jnp.concatenate([
        jnp.tile(swa_q_norm, SWA_HEADS) * HEAD_DIM ** -0.5, jnp.tile(swa_k_norm, SWA_KV_HEADS),
        ones(SWA_KV_HEADS * HEAD_DIM),
        jnp.tile(dsa_q_norm, DSA_HEADS) * HEAD_DIM ** -0.5, dsa_k_norm, ones(HEAD_DIM),
        ones(IDX_HEADS * IDX_DIM) * IDX_DIM ** -0.5, ones(IDX_DIM), ones(IDX_HEADS) * IDX_HEADS ** -0.5,
        ones(ATT_WIDTH - COL_GATE)])[None, :]
    col = jnp.arange(ATT_WIDTH)
    normed = (((col >= COL_QA) & (col < COL_VA)) | ((col >= COL_QB) & (col < COL_KVB + KVB_V))
              ).astype(F32)[None, :]
    proj = _proj(h1f, w_in_t, mult, normed, 1024, 768).reshape(b, s, ATT_WIDTH)
    oa = _swa(proj, swa_sinks, slopes[:SWA_HEADS])
    ob = _dsa(proj, slopes[SWA_HEADS:])
    merged = _merge(h1f, oa.reshape(b * s, -1), ob.reshape(b * s, -1), w_in_t,
                    w_branch_a, w_branch_b)

    w_r = jnp.concatenate([w_expert, w_group,
                           jnp.zeros((d, LANES - N_EXPERTS - N_GROUPS), F32)], axis=1)
    b_r = jnp.concatenate([b_expert, b_group,
                           jnp.zeros((LANES - N_EXPERTS - N_GROUPS,), F32)])[None, :]
    w_r_hi = _bf16_head(w_r)
    w_r_split = jnp.concatenate([w_r_hi.astype(BF16), (w_r - w_r_hi).astype(BF16)], axis=1)
    x1, h2p, ids, cw = _outproj(merged.reshape(b, s, d), w_out, x, gt1,
                                ln2_g[None, :], sc2, sh2, w_r_split, b_r)
    f = w_gate_up.shape[-1]
    return _moe(h2p, ids, cw, w_gate_up.reshape(N_EXPERTS, d, f), w_up.reshape(N_EXPERTS, d, f),
                w_down.reshape(N_EXPERTS, f, d), x1, gt2)


def kernel(x, c, w_ada, b_ada, ln1_g, w_in, swa_q_norm, swa_k_norm, swa_sinks, dsa_q_norm, dsa_k_norm,
           w_branch_a, w_branch_b, w_out, ln2_g, w_group, b_group, w_expert, b_expert,
           w_gate_up, w_up, w_down):
    slopes = _alibi_slopes()
    bsz = c.shape[0]
    c_pad = jnp.concatenate([c, jnp.zeros((ADA_ROWS - bsz, c.shape[1]), c.dtype)], axis=0)
    for l in range(w_ada.shape[0]):
        mod = _ada(c_pad, w_ada[l], b_ada[l][None, :])[:bsz]
        x = _layer(x, mod, ln1_g[l], w_in[l], swa_q_norm[l], swa_k_norm[l], swa_sinks[l],
                   dsa_q_norm[l], dsa_k_norm[l], w_branch_a[l], w_branch_b[l], w_out[l], ln2_g[l],
                   w_group[l], b_group[l], w_expert[l], b_expert[l],
                   w_gate_up[l], w_up[l], w_down[l], slopes)
    return x
```

```python
import functools

import jax
import jax.numpy as jnp
from jax import lax
from jax.experimental import pallas as pl
from jax.experimental.pallas import tpu as pltpu

F32 = jnp.float32
BF16 = jnp.bfloat16
I32 = jnp.int32

D_MODEL = 2048
HEAD_DIM = 64
SWA_HEADS = 16
SWA_KV_HEADS = 4
SWA_GROUP = SWA_HEADS // SWA_KV_HEADS
WINDOW = 128
DSA_HEADS = 16
IDX_HEADS = 16
IDX_DIM = 64
DSA_TOPK_MAX = 256
DSA_BLOCK = 128
N_GROUPS = 4
EXPERTS_PER_GROUP = 8
N_EXPERTS = N_GROUPS * EXPERTS_PER_GROUP
D_EXPERT = 512
RMS_EPS = 1e-6
NEG_INF = -1e30
INT_MIN = -(2 ** 31)

LANES = 128
VMEM_LIMIT = 56 * 1024 * 1024

COL_QA, COL_KA, COL_VA, COL_QB = 0, 1024, 1280, 1536
COL_KVB = 2560
COL_QI = 2688
COL_KWI = 3712
COL_GATE = 3792
ATT_WIDTH = 3840
KVB_K, KVB_V = 0, HEAD_DIM
KWI_K, KWI_W = 0, IDX_DIM

NT_DIMS = (((1,), (1,)), ((), ()))


def _cparams(*sem):
    return pltpu.CompilerParams(dimension_semantics=sem, vmem_limit_bytes=VMEM_LIMIT)


def _sigmoid(x):
    return 1.0 / (1.0 + jnp.exp(-x))


def _rms(x, g):
    return x * lax.rsqrt(jnp.mean(x * x, axis=-1, keepdims=True) + RMS_EPS) * g


def _ada_kernel(c_ref, w_ref, b_ref, o_ref):
    c = c_ref[...]
    cond = c * _sigmoid(c)
    c_hi = cond.astype(BF16)
    c_lo = (cond - c_hi.astype(F32)).astype(BF16)
    w = w_ref[...]
    w_hi = w.astype(BF16)
    w_lo = (w - w_hi.astype(F32)).astype(BF16)
    o_ref[...] = (jnp.dot(c_hi, w_hi, preferred_element_type=F32)
                  + jnp.dot(c_lo, w_hi, preferred_element_type=F32)
                  + jnp.dot(c_hi, w_lo, preferred_element_type=F32)) + b_ref[...]


ADA_ROWS = 16


def _ada(c_pad, w, b):
    n = w.shape[1]
    tn = 1024
    return pl.pallas_call(
        _ada_kernel,
        grid=(n // tn,),
        in_specs=[pl.BlockSpec((ADA_ROWS, D_MODEL), lambda j: (0, 0)),
                  pl.BlockSpec((D_MODEL, tn), lambda j: (0, j)),
                  pl.BlockSpec((1, tn), lambda j: (0, j))],
        out_specs=pl.BlockSpec((ADA_ROWS, tn), lambda j: (0, j)),
        out_shape=jax.ShapeDtypeStruct((ADA_ROWS, n), F32),
        compiler_params=_cparams("parallel"),
        name="ada_mod",
    )(c_pad, w, b)


def _ln_mod_kernel(x_ref, g_ref, sc_ref, sh_ref, o_ref):
    y = _rms(x_ref[0], g_ref[...])
    o_ref[0] = (y * (1.0 + sc_ref[0]) + sh_ref[0]).astype(o_ref.dtype)


def _ln_mod(x, g, sc, sh):
    b, s, d = x.shape
    ts = 1024
    return pl.pallas_call(
        _ln_mod_kernel,
        grid=(b, s // ts),
        in_specs=[pl.BlockSpec((1, ts, d), lambda i, j: (i, j, 0)),
                  pl.BlockSpec((1, d), lambda i, j: (0, 0)),
                  pl.BlockSpec((1, 1, d), lambda i, j: (i, 0, 0)),
                  pl.BlockSpec((1, 1, d), lambda i, j: (i, 0, 0))],
        out_specs=pl.BlockSpec((1, ts, d), lambda i, j: (i, j, 0)),
        out_shape=jax.ShapeDtypeStruct((b, s, d), BF16),
        compiler_params=_cparams("parallel", "parallel"),
        name="ln_mod",
    )(x, g, sc, sh)


PROJ_ROW_PARTS = 4


def _proj_kernel(a_ref, wt_ref, mult_ref, normed_ref, o_ref):
    w = wt_ref[...].astype(BF16)
    tm, tn = o_ref.shape
    rows_per_part = tm // PROJ_ROW_PARTS
    half = lax.broadcasted_iota(I32, (rows_per_part, LANES), 1) < HEAD_DIM
    for part in range(PROJ_ROW_PARTS):
        rows = slice(part * rows_per_part, (part + 1) * rows_per_part)
        y = lax.dot_general(a_ref[rows, :], w, NT_DIMS, preferred_element_type=F32)
        for g in range(tn // LANES):
            cols = slice(g * LANES, (g + 1) * LANES)
            yg = y[:, cols]
            sq = yg * yg
            ms_lo = jnp.sum(jnp.where(half, sq, 0.0), axis=-1, keepdims=True) / HEAD_DIM
            ms_hi = jnp.sum(jnp.where(half, 0.0, sq), axis=-1, keepdims=True) / HEAD_DIM
            r = lax.rsqrt(jnp.where(half, ms_lo, ms_hi) + RMS_EPS)
            o_ref[rows, cols] = yg * jnp.where(normed_ref[:, cols] > 0.0, r, 1.0) * mult_ref[:, cols]


def _proj(a, wt, mult, normed, tm, tn):
    m, k = a.shape
    n = mult.shape[1]
    vec = pl.BlockSpec((1, tn), lambda i, j: (0, j))
    return pl.pallas_call(
        _proj_kernel,
        grid=(m // tm, n // tn),
        in_specs=[pl.BlockSpec((tm, k), lambda i, j: (i, 0)),
                  pl.BlockSpec((tn, k), lambda i, j: (j, 0)),
                  vec, vec],
        out_specs=pl.BlockSpec((tm, tn), lambda i, j: (i, j)),
        out_shape=jax.ShapeDtypeStruct((m, n), F32),
        compiler_params=_cparams("parallel", "parallel"),
        name="proj_att",
    )(a, wt, mult, normed)


def _swa_kernel(sink_ref, slope_ref, q_ref, kp_ref, kc_ref, vp_ref, vc_ref, o_ref, lhs, e_s):
    n = pl.program_id(1)
    w, w2 = WINDOW, 2 * WINDOW
    q = q_ref[0]
    kw = jnp.concatenate([kp_ref[0], kc_ref[0]], axis=0)
    vw = jnp.concatenate([vp_ref[0], vc_ref[0]], axis=0)
    half_q = lax.broadcasted_iota(I32, (w, LANES), 1) < HEAD_DIM
    half_k = lax.broadcasted_iota(I32, (w2, LANES), 1) < HEAD_DIM
    row = lax.broadcasted_iota(I32, (w, w2), 0)
    col = lax.broadcasted_iota(I32, (w, w2), 1)
    rel = row + w - col
    valid = (rel >= 0) & (rel < w) & ((col >= w) | (n > 0))
    mbias = jnp.where(valid, 0.0, NEG_INF)
    relf = rel.astype(F32)

    ones_col = jnp.ones((w2, LANES), BF16)
    for g in range(SWA_KV_HEADS):
        if g % 2 == 0:
            grp = slice((g // 2) * LANES, (g // 2 + 1) * LANES)
            kn = kw[:, grp]
            kn_sw = pltpu.roll(kn, HEAD_DIM, axis=1)
            vp = vw[:, grp]
            vp_sw = pltpu.roll(vp, HEAD_DIM, axis=1)
        kdup = jnp.where(half_k, kn, kn_sw) if g % 2 == 0 else jnp.where(half_k, kn_sw, kn)
        vdup = jnp.where(half_k, vp, vp_sw) if g % 2 == 0 else jnp.where(half_k, vp_sw, vp)
        vx = jnp.concatenate([vdup.astype(BF16), ones_col], axis=1)
        for r in range(SWA_GROUP):
            h = g * SWA_GROUP + r
            if r % 2 == 0:
                qn = q[:, (h // 2) * LANES:(h // 2 + 1) * LANES]
            keep = half_q if r % 2 == 0 else jnp.logical_not(half_q)
            lhs[r * w:(r + 1) * w, :] = jnp.where(keep, qn, 0.0).astype(BF16)
        s_all = lax.dot_general(lhs[...], kdup.astype(BF16), NT_DIMS, preferred_element_type=F32)
        sink_e = []
        for r in range(SWA_GROUP):
            h = g * SWA_GROUP + r
            rows = slice(r * w, (r + 1) * w)
            s = s_all[rows, :] + (mbias - slope_ref[h] * relf)
            sink = jnp.full((w, LANES), sink_ref[h], F32)
            m = jnp.maximum(jnp.max(s, axis=-1, keepdims=True), sink)
            e_s[rows, :] = jnp.exp(s - jnp.concatenate([m, m], axis=1)).astype(BF16)
            sink_e.append(jnp.exp(sink - m))
        pv = jnp.dot(e_s[...], vx, preferred_element_type=F32)
        for r in range(0, SWA_GROUP, 2):
            h = g * SWA_GROUP + r
            ev, od = pv[r * w:(r + 1) * w, :], pv[(r + 1) * w:(r + 2) * w, :]
            den = jnp.where(half_q, ev[:, LANES:] + sink_e[r], od[:, LANES:] + sink_e[r + 1])
            o = jnp.where(half_q, ev[:, :LANES], od[:, :LANES]) / den
            o_ref[0, :, (h // 2) * LANES:(h // 2 + 1) * LANES] = o.astype(o_ref.dtype)


def _swa(proj3, sinks, slopes):
    b, s, _ = proj3.shape
    nb = s // WINDOW
    kvw = SWA_KV_HEADS * HEAD_DIM
    qw = SWA_HEADS * HEAD_DIM
    smem = pl.BlockSpec(memory_space=pltpu.SMEM)
    prev =lambda i, n: (i, jnp.maximum(n - 1, 0), COL_KA // kvw)
    cur = lambda i, n: (i, n, COL_KA // kvw)
    prev_v = lambda i, n: (i, jnp.maximum(n - 1, 0), COL_VA // kvw)
    cur_v = lambda i, n: (i, n, COL_VA // kvw)
    return pl.pallas_call(
        _swa_kernel,
        grid=(b, nb),
        in_specs=[smem, smem,
                  pl.BlockSpec((1, WINDOW, qw), lambda i, n: (i, n, COL_QA // qw)),
                  pl.BlockSpec((1, WINDOW, kvw), prev),
                  pl.BlockSpec((1, WINDOW, kvw), cur),
                  pl.BlockSpec((1, WINDOW, kvw), prev_v),
                  pl.BlockSpec((1, WINDOW, kvw), cur_v)],
        out_specs=pl.BlockSpec((1, WINDOW, qw), lambda i, n: (i, n, 0)),
        out_shape=jax.ShapeDtypeStruct((b, s, qw), BF16),
        scratch_shapes=[pltpu.VMEM((SWA_GROUP * WINDOW, LANES), BF16),
                        pltpu.VMEM((SWA_GROUP * WINDOW, 2 * WINDOW), BF16)],
        compiler_params=_cparams("parallel", "parallel"),
        name="swa_attn",
    )(sinks, slopes, proj3, proj3, proj3, proj3, proj3)


DSA_CHUNK = 256
DSA_PART_HEADS = 4
ALIBI_SPLIT = 3
POS_LO_BITS = 7


def _bf16_head(x):
    return lax.bitcast_convert_type(lax.bitcast_convert_type(x, I32) & jnp.int32(-65536), F32)


def _alibi_columns(slopes):
    parts, rest = [], slopes.astype(F32)
    for _ in range(ALIBI_SPLIT):
        piece = _bf16_head(rest)
        parts.append(piece)
        rest = rest - piece
    cols = [p * float(2 ** POS_LO_BITS) for p in parts] + parts
    pad = jnp.zeros((slopes.shape[0], LANES - 2 * ALIBI_SPLIT), F32)
    return jnp.concatenate([jnp.stack(cols, axis=1), pad], axis=1)


def _dsa_kernel(*refs, topk):
    n_pairs = DSA_HEADS // 2
    qb_refs, qi_refs = refs[:n_pairs], refs[n_pairs:2 * n_pairs]
    (kwi_blk_ref, kvb_ref, kwi_ref, acol_ref, o_ref,
     ki2, kx, vx, lhs_i, lhs_q, wb, key_s, keyt_s, s_s, thr_s, acc_s, m_s) = refs[2 * n_pairs:]
    i = pl.program_id(1)
    blk, ck = DSA_BLOCK, DSA_CHUNK
    nch = (i + 2) // 2
    seq = ki2.shape[0]
    max_ch = seq // ck
    half = lax.broadcasted_iota(I32, (blk, LANES), 1) < HEAD_DIM

    @pl.when(i == 0)
    def _():
        kvb = kvb_ref[0]
        kib = kwi_ref[0][:, KWI_K:KWI_K + IDX_DIM].astype(BF16)
        ki2[...] = jnp.concatenate([kib, kib], axis=1)
        kn = kvb[:, KVB_K:KVB_K + HEAD_DIM].astype(BF16)
        pos = lax.broadcasted_iota(I32, (seq, LANES), 0)
        lane = lax.broadcasted_iota(I32, (seq, LANES), 1)
        hi = lax.shift_right_logical(pos, POS_LO_BITS).astype(F32)
        lo = (pos & (2 ** POS_LO_BITS - 1)).astype(F32)
        posc = jnp.where(lane < ALIBI_SPLIT, hi, jnp.where(lane < 2 * ALIBI_SPLIT, lo, 0.0)).astype(BF16)
        kx[...] = jnp.concatenate([kn, kn, posc], axis=1)
        vb = kvb[:, KVB_V:KVB_V + HEAD_DIM].astype(BF16)
        ones = jnp.ones((seq, HEAD_DIM), BF16)
        vx[0] = jnp.concatenate([vb, ones], axis=1)
        vx[1] = jnp.concatenate([ones, vb], axis=1)

    wi = kwi_blk_ref[0][:, KWI_W:KWI_W + IDX_HEADS]
    for p in range(n_pairs):
        qn = qb_refs[p][0]
        qip = qi_refs[p][0]
        for k in range(2):
            h = 2 * p + k
            slot = k * n_pairs + p
            keep = half if k == 0 else jnp.logical_not(half)
            rows = slice(slot * blk, (slot + 1) * blk)
            lhs_i[rows, :] = jnp.where(keep, qip, 0.0).astype(BF16)
            lhs_q[rows, :LANES] = jnp.where(keep, qn, 0.0).astype(BF16)
            lhs_q[rows, LANES:] = jnp.broadcast_to(acol_ref[h:h + 1, :], (blk, LANES)).astype(BF16)
            wb[slot] = jnp.broadcast_to(wi[:, h:h + 1], (blk, LANES))

    row_t = i * blk + lax.broadcasted_iota(I32, (blk, ck), 0)
    col_k = lax.broadcasted_iota(I32, (blk, ck), 1)

    def chunk_rows(c):
        return pl.ds(c * ck if isinstance(c, int) else pl.multiple_of(c * ck, ck), ck)

    def score_chunk(c, carry):
        ks = chunk_rows(c)
        kic = ki2[ks, :]
        sc = jnp.zeros((blk, ck), F32)
        for part in range(0, IDX_HEADS, DSA_PART_HEADS):
            logits = lax.dot_general(lhs_i[part * blk:(part + DSA_PART_HEADS) * blk, :], kic, NT_DIMS,
                                     preferred_element_type=F32)
            for k in range(DSA_PART_HEADS):
                w = wb[part + k]
                lh = jnp.maximum(logits[k * blk:(k + 1) * blk, :], 0.0)
                sc = sc + lh * jnp.concatenate([w, w], axis=1)
        bits = lax.bitcast_convert_type(sc + 0.0, I32)
        key = jnp.where(bits < 0, bits ^ jnp.int32(0x7FFFFFFF), bits)
        key = jnp.where(c * ck + col_k <= row_t, key, jnp.int32(INT_MIN))
        key_s[c] = key
        keyt_s[c] = key.T
        return carry

    lax.fori_loop(0, nch, score_chunk, 0)

    thr_s[...] = jnp.full(thr_s.shape, INT_MIN, I32)
    for n in range(2, max_ch + 1):
        @pl.when(nch == n)
        def _(n=n):
            def step(it, thr):
                cand = thr + lax.shift_left(jnp.int32(1), 31 - it)
                cnt8 = jnp.zeros((8, blk), F32)
                for c in range(n):
                    hit = jnp.where(keyt_s[c] >= cand, 1.0, 0.0)
                    cnt8 = cnt8 + jnp.sum(hit.reshape(ck // 8, 8, blk), axis=0)
                cnt = jnp.sum(cnt8, axis=0, keepdims=True)
                return jnp.where(cnt >= float(topk), cand, thr)

            thr = lax.fori_loop(0, 32, step, jnp.full((1, blk), INT_MIN, I32))
            thr_s[...] = jnp.broadcast_to(thr, thr_s.shape).T

    thr = thr_s[...]
    thr2 = jnp.concatenate([thr, thr], axis=1)

    def count_gt(c, cnt):
        k = key_s[c]
        return cnt + jnp.where(k[:, :LANES] > thr, 1.0, 0.0) + jnp.where(k[:, LANES:] > thr, 1.0, 0.0)

    n_gt = jnp.sum(lax.fori_loop(0, nch, count_gt, jnp.zeros((blk, LANES), F32)), axis=-1, keepdims=True)
    need = float(topk) - n_gt
    upper = (lax.broadcasted_iota(I32, (ck, ck), 0) < lax.broadcasted_iota(I32, (ck, ck), 1)).astype(BF16)

    def masked_scores(c, n_eq, first=False):
        ks = chunk_rows(c)
        k = key_s[c]
        eq = (k == thr2) & (c * ck + col_k <= row_t)
        eqf = eq.astype(BF16)
        before = jnp.dot(eqf, upper, preferred_element_type=F32) + n_eq
        sel = (k > thr2) | (eq & (before < need))
        mb = jnp.where(sel, 0.0, NEG_INF)
        kxc = kx[ks, :]
        for part in range(0, DSA_HEADS, DSA_PART_HEADS):
            s_part = lax.dot_general(lhs_q[part * blk:(part + DSA_PART_HEADS) * blk, :], kxc, NT_DIMS,
                                     preferred_element_type=F32)
            for k in range(DSA_PART_HEADS):
                rows = slice((part + k) * blk, (part + k + 1) * blk)
                s = s_part[k * blk:(k + 1) * blk, :] + mb
                s_s[c, rows, :] = s
                m_chunk = jnp.maximum(s[:, :LANES], s[:, LANES:])
                m_s[rows, :] = m_chunk if first else jnp.maximum(m_s[rows, :], m_chunk)
        return n_eq + jnp.sum(eqf.astype(F32), axis=-1, keepdims=True)

    n_eq0 = masked_scores(0, jnp.zeros((blk, 1), F32), first=True)
    lax.fori_loop(1, nch, masked_scores, n_eq0)

    m_s[...] = jnp.broadcast_to(jnp.max(m_s[...], axis=-1, keepdims=True), m_s.shape)

    def attend(c, carry, first=False):
        for part in range(0, DSA_HEADS, DSA_PART_HEADS):
            vxc = vx[part // n_pairs, chunk_rows(c), :]
            e = []
            for k in range(DSA_PART_HEADS):
                rows = slice((part + k) * blk, (part + k + 1) * blk)
                m = m_s[rows, :]
                e.append(jnp.exp(s_s[c, rows, :] - jnp.concatenate([m, m], axis=1)).astype(BF16))
            rows = slice(part * blk, (part + DSA_PART_HEADS) * blk)
            pv = jnp.dot(jnp.concatenate(e, axis=0), vxc, preferred_element_type=F32)
            acc_s[rows, :] = pv if first else acc_s[rows, :] + pv
        return carry

    attend(0, 0, first=True)
    lax.fori_loop(1, nch, attend, 0)

    for p in range(n_pairs):
        a_lo = acc_s[p * blk:(p + 1) * blk, :]
        a_hi = acc_s[(n_pairs + p) * blk:(n_pairs + p + 1) * blk, :]
        den = pltpu.roll(jnp.where(half, a_hi, a_lo), HEAD_DIM, axis=1)
        o_ref[0, :, p * LANES:(p + 1) * LANES] = (jnp.where(half, a_lo, a_hi) / den).astype(o_ref.dtype)


def _dsa(proj3, slopes):
    b, s, _ = proj3.shape
    nb = s // DSA_BLOCK
    qw = DSA_HEADS * HEAD_DIM
    n_pairs = DSA_HEADS // 2
    rows = DSA_HEADS * DSA_BLOCK
    fixed = lambda i, n: (0, 0)

    def group(col):
        return pl.BlockSpec((1, DSA_BLOCK, LANES), lambda i, n: (i, n, col // LANES))

    def group_all(col):
        return pl.BlockSpec((1, s, LANES), lambda i, n: (i, 0, col // LANES))

    return pl.pallas_call(
        functools.partial(_dsa_kernel, topk=min(DSA_TOPK_MAX, s // 4)),
        grid=(b, nb),
        in_specs=([group(COL_QB + p * LANES) for p in range(n_pairs)]
                  + [group(COL_QI + p * LANES) for p in range(n_pairs)]
                  + [group(COL_KWI), group_all(COL_KVB), group_all(COL_KWI),
                     pl.BlockSpec((DSA_HEADS, LANES), fixed)]),
        out_specs=pl.BlockSpec((1, DSA_BLOCK, qw), lambda i, n: (i, n, 0)),
        out_shape=jax.ShapeDtypeStruct((b, s, qw), BF16),
        scratch_shapes=[pltpu.VMEM((s, LANES), BF16),
                        pltpu.VMEM((s, 2 * LANES), BF16),
                        pltpu.VMEM((2, s, LANES), BF16),
                        pltpu.VMEM((rows, LANES), BF16),
                        pltpu.VMEM((rows, 2 * LANES), BF16),
                        pltpu.VMEM((IDX_HEADS, DSA_BLOCK, LANES), F32),
                        pltpu.VMEM((s // DSA_CHUNK, DSA_BLOCK, DSA_CHUNK), I32),
                        pltpu.VMEM((s // DSA_CHUNK, DSA_CHUNK, DSA_BLOCK), I32),
                        pltpu.VMEM((s // DSA_CHUNK, rows, DSA_CHUNK), F32),
                        pltpu.VMEM((DSA_BLOCK, LANES), I32),
                        pltpu.VMEM((rows, LANES), F32),
                        pltpu.VMEM((rows, LANES), F32)],
        compiler_params=_cparams("arbitrary", "arbitrary"),
        name="dsa_attn",
    )(*([proj3] * (2 * n_pairs + 3)), _alibi_columns(slopes))


def _merge_kernel(h_ref, oa_ref, ob_ref, wt_hbm, wa_ref, wb_ref, o_ref, g32, gbf, sem):
    j, i = pl.program_id(0), pl.program_id(1)
    tn, d = o_ref.shape[1], h_ref.shape[1]

    def gate_copies(jj, slot):
        return [pltpu.make_async_copy(
            wt_hbm.at[pl.ds(pl.multiple_of(COL_GATE + k * d + jj * tn, 8), tn)], g32.at[slot, k], sem.at[slot, k])
            for k in range(2)]

    @pl.when((j == 0) & (i == 0))
    def _():
        for c in gate_copies(0, 0):
            c.start()

    @pl.when(i == 0)
    def _():
        slot = j % 2
        for c in gate_copies(j, slot):
            c.wait()
        gbf[...] = g32[slot].astype(BF16)

        @pl.when(j + 1 < pl.num_programs(0))
        def _():
            for c in gate_copies(j + 1, 1 - slot):
                c.start()

    h = h_ref[...]
    ga = lax.dot_general(h, gbf[0], NT_DIMS, preferred_element_type=F32)
    gb = lax.dot_general(h, gbf[1], NT_DIMS, preferred_element_type=F32)
    a = jnp.dot(oa_ref[...], wa_ref[...].astype(BF16), preferred_element_type=F32)
    b = jnp.dot(ob_ref[...], wb_ref[...].astype(BF16), preferred_element_type=F32)
    o_ref[...] = (_sigmoid(ga) * a + _sigmoid(gb) * b).astype(o_ref.dtype)


def _merge(h, oa, ob, w_in_t, wa, wb):
    m, d = h.shape
    tm, tn = 1024, 512
    return pl.pallas_call(
        _merge_kernel,
        grid=(d // tn, m // tm),
        in_specs=[pl.BlockSpec((tm, d), lambda j, i: (i, 0)),
                  pl.BlockSpec((tm, oa.shape[1]), lambda j, i: (i, 0)),
                  pl.BlockSpec((tm, ob.shape[1]), lambda j, i: (i, 0)),
                  pl.BlockSpec(memory_space=pl.ANY),
                  pl.BlockSpec((wa.shape[0], tn), lambda j, i: (0, j)),
                  pl.BlockSpec((wb.shape[0], tn), lambda j, i: (0, j))],
        out_specs=pl.BlockSpec((tm, tn), lambda j, i: (i, j)),
        out_shape=jax.ShapeDtypeStruct((m, d), BF16),
        scratch_shapes=[pltpu.VMEM((2, 2, tn, d), w_in_t.dtype),
                        pltpu.VMEM((2, tn, d), BF16),
                        pltpu.SemaphoreType.DMA((2, 2))],
        compiler_params=_cparams("arbitrary", "arbitrary"),
        name="merge_branches",
    )(h, oa, ob, w_in_t, wa, wb)


MOE_TILE = 256
MOE_SLOTS = 2
TOKEN_TILE = 512


def _moe_max_tiles(n_tokens):
    return -(-(n_tokens * MOE_SLOTS + N_EXPERTS * (MOE_TILE - 1)) // MOE_TILE)


def _pack_bf16_pair(x):
    k = x.shape[1] // 2
    bits = lax.bitcast_convert_type(x.astype(BF16).astype(F32), I32)
    return lax.shift_right_logical(bits[:, :k], 16) | (bits[:, k:] & jnp.int32(-65536))


def _unpack_bf16_pair(w):
    lo = lax.bitcast_convert_type(lax.shift_left(w, 16), F32).astype(BF16)
    hi = lax.bitcast_convert_type(w & jnp.int32(-65536), F32).astype(BF16)
    return lo, hi


def _route(logits):
    lane = lax.broadcasted_iota(I32, logits.shape, 1)
    is_g = (lane >= N_EXPERTS) & (lane < N_EXPERTS + N_GROUPS)
    neg = -jnp.inf
    gl = jnp.where(is_g, logits, neg)
    gmax = jnp.max(gl, axis=-1, keepdims=True)
    g_idx = jnp.min(jnp.where(gl == gmax, lane - N_EXPERTS, N_GROUPS), axis=-1, keepdims=True)
    p_g = 1.0 / jnp.sum(jnp.where(is_g, jnp.exp(gl - gmax), 0.0), axis=-1, keepdims=True)
    in_grp = (lane < N_EXPERTS) & ((lane // EXPERTS_PER_GROUP) == g_idx)
    el = jnp.where(in_grp, logits, neg)
    m1 = jnp.max(el, axis=-1, keepdims=True)
    i1 = jnp.min(jnp.where(el == m1, lane, LANES), axis=-1, keepdims=True)
    el2 = jnp.where(lane == i1, neg, el)
    m2 = jnp.max(el2, axis=-1, keepdims=True)
    i2 = jnp.min(jnp.where(el2 == m2, lane, LANES), axis=-1, keepdims=True)
    e2 = jnp.exp(m2 - m1)
    w1 = 1.0 / (1.0 + e2)
    w2 = e2 / (1.0 + e2)
    ids = jnp.where(lane == 0, i1, jnp.where(lane == 1, i2, 0))
    return ids, p_g * jnp.where(lane == 0, w1, jnp.where(lane == 1, w2, 0.0))


def _outproj_kernel(mg_ref, wo_ref, x_ref, gt_ref, g2_ref, sc_ref, sh_ref, wr_ref, br_ref,
                    x1_ref, h2p_ref, ids_ref, cw_ref):
    y = jnp.dot(mg_ref[0], wo_ref[...].astype(BF16), preferred_element_type=F32)
    x1 = x_ref[0] + gt_ref[0] * y
    x1_ref[0] = x1
    h2 = _rms(x1, g2_ref[...]) * (1.0 + sc_ref[0]) + sh_ref[0]
    h2p_ref[0] = _pack_bf16_pair(h2)
    h_hi = h2.astype(BF16)
    h_lo = (h2 - h_hi.astype(F32)).astype(BF16)
    both = jnp.dot(h_hi, wr_ref[...], preferred_element_type=F32)
    logits = (both[:, :LANES] + jnp.dot(h_lo, wr_ref[:, :LANES], preferred_element_type=F32)
              + both[:, LANES:]) + br_ref[...]
    ids_ref[0], cw_ref[0] = _route(logits)


def _outproj(merged3, w_out, x, gt1, g2, sc2, sh2, w_r, b_r):
    b, s, d = x.shape
    tm = 512
    row = lambda i, j: (i, j, 0)
    per_b = lambda i, j: (i, 0, 0)
    fixed = lambda i, j: (0, 0)
    return pl.pallas_call(
        _outproj_kernel,
        grid=(b, s // tm),
        in_specs=[pl.BlockSpec((1, tm, d), row),
                  pl.BlockSpec((d, d), fixed, pipeline_mode=pl.Buffered(1)),
                  pl.BlockSpec((1, tm, d), row),
                  pl.BlockSpec((1, 1, d), per_b),
                  pl.BlockSpec((1, d), fixed),
                  pl.BlockSpec((1, 1, d), per_b),
                  pl.BlockSpec((1, 1, d), per_b),
                  pl.BlockSpec((d, 2 * LANES), fixed),
                  pl.BlockSpec((1, LANES), fixed)],
        out_specs=[pl.BlockSpec((1, tm, d), row),
                   pl.BlockSpec((1, tm, d // 2), row),
                   pl.BlockSpec((1, tm, LANES), row),
                   pl.BlockSpec((1, tm, LANES), row)],
        out_shape=[jax.ShapeDtypeStruct((b, s, d), F32),
                   jax.ShapeDtypeStruct((b, s, d // 2), I32),
                   jax.ShapeDtypeStruct((b, s, LANES), I32),
                   jax.ShapeDtypeStruct((b, s, LANES), F32)],
        compiler_params=_cparams("parallel", "parallel"),
        name="outproj_ln2_router",
    )(merged3, w_out, x, gt1, g2, sc2, sh2, w_r, b_r)


def _plan_kernel(ids_ref, dest_ref, te_ref, rank_ref):
    n = ids_ref.shape[0]
    cb = 256
    lane = lax.broadcasted_iota(I32, (cb, LANES), 1)
    lower = (lax.broadcasted_iota(I32, (cb, cb), 0) > lax.broadcasted_iota(I32, (cb, cb), 1)).astype(BF16)

    def count(bi, carry):
        rows = pl.ds(pl.multiple_of(bi * cb, cb), cb)
        ids = ids_ref[rows, :]
        m1 = lane == ids[:, 0:1]
        m2 = lane == ids[:, 1:2]
        onehot = m1 | m2
        before = jnp.dot(lower, onehot.astype(BF16), preferred_element_type=F32) + carry
        r1 = jnp.sum(jnp.where(m1, before, 0.0), axis=-1, keepdims=True)
        r2 = jnp.sum(jnp.where(m2, before, 0.0), axis=-1, keepdims=True)
        rank_ref[rows, :] = jnp.where(lane == 0, r1, jnp.where(lane == 1, r2, 0.0))
        return carry + jnp.sum(onehot.astype(F32), axis=0, keepdims=True)

    counts = lax.fori_loop(0, n // cb, count, jnp.zeros((1, LANES), F32))
    padded = jnp.floor((counts + (MOE_TILE - 1)) / MOE_TILE) * MOE_TILE
    upper = (lax.broadcasted_iota(I32, (LANES, LANES), 0)
             < lax.broadcasted_iota(I32, (LANES, LANES), 1)).astype(F32)
    offs = jnp.dot(jnp.broadcast_to(padded, (8, LANES)), upper, preferred_element_type=F32,
                   precision=lax.Precision.HIGHEST)[0:1]

    def place(bi, c):
        rows = pl.ds(pl.multiple_of(bi * cb, cb), cb)
        ids = ids_ref[rows, :]
        o1 = jnp.sum(jnp.where(lane == ids[:, 0:1], offs, 0.0), axis=-1, keepdims=True)
        o2 = jnp.sum(jnp.where(lane == ids[:, 1:2], offs, 0.0), axis=-1, keepdims=True)
        slot = rank_ref[rows, :] + jnp.where(lane == 0, o1, jnp.where(lane == 1, o2, 0.0))
        dest_ref[rows, :] = slot.astype(I32)
        return c

    lax.fori_loop(0, n // cb, place, 0)

    sq_row = lax.broadcasted_iota(I32, (LANES, LANES), 0)
    sq_lane = lax.broadcasted_iota(I32, (LANES, LANES), 1)
    tile_start = sq_row.astype(F32) * MOE_TILE
    finished = ((offs + padded) <= tile_start) & (sq_lane < N_EXPERTS)
    tile_expert = jnp.minimum(jnp.sum(finished.astype(F32), axis=-1, keepdims=True), N_EXPERTS - 1.0)
    n_tiles = jnp.sum(padded, axis=-1, keepdims=True) / MOE_TILE
    te_ref[...] = jnp.where(sq_lane == 1, n_tiles, tile_expert).astype(I32)


def _plan(ids):
    n = ids.shape[0]
    return pl.pallas_call(
        _plan_kernel,
        out_shape=[jax.ShapeDtypeStruct((n, LANES), I32),
                   jax.ShapeDtypeStruct((LANES, LANES), I32)],
        scratch_shapes=[pltpu.VMEM((n, LANES), F32)],
        compiler_params=pltpu.CompilerParams(vmem_limit_bytes=VMEM_LIMIT),
        name="moe_plan",
    )(ids)


def _dispatch_kernel(te_ref, nt_ref, dest_ref, h_ref, hs_ref, zeros, sem, zsem):
    tq = h_ref.shape[0]

    @pl.when(pl.program_id(0) == 0)
    def _():
        zeros[...] = jnp.zeros_like(zeros)
        n_tiles, nt = te_ref.shape[0], nt_ref[0]

        def has_padding(t):
            return (t >= nt - 1) | (te_ref[jnp.minimum(t + 1, n_tiles - 1)] != te_ref[t])

        def tile_copy(t):
            rows = pl.ds(pl.multiple_of(t * MOE_TILE, MOE_TILE), MOE_TILE)
            return pltpu.make_async_copy(zeros, hs_ref.at[rows], zsem)

        def start(t, c):
            @pl.when(has_padding(t))
            def _():
                tile_copy(t).start()
            return c

        def wait(t, c):
            @pl.when(has_padding(t))
            def _():
                tile_copy(t).wait()
            return c

        lax.fori_loop(0, n_tiles, start, 0)
        lax.fori_loop(0, n_tiles, wait, 0)

    def issue(r, c):
        for k in range(MOE_SLOTS):
            slot = dest_ref[0, 0, MOE_SLOTS * r + k]
            pltpu.make_async_copy(h_ref.at[pl.ds(r, 1)], hs_ref.at[pl.ds(slot, 1)], sem).start(priority=k)
        return c

    lax.fori_loop(0, tq, issue, 0, unroll=8)
    done = hs_ref.at[pl.ds(0, MOE_SLOTS * tq)]
    pltpu.make_async_copy(done, done, sem).wait()


def _dispatch(tile_expert, n_tiles, dest3, h2p, n_rows):
    n, w = h2p.shape
    tq = TOKEN_TILE
    return pl.pallas_call(
        _dispatch_kernel,
        grid_spec=pltpu.PrefetchScalarGridSpec(
            num_scalar_prefetch=2,
            grid=(n // tq,),
            in_specs=[pl.BlockSpec((1, 1, MOE_SLOTS * tq), lambda i, te, nt: (i, 0, 0),
                                   memory_space=pltpu.SMEM),
                      pl.BlockSpec((tq, w), lambda i, te, nt: (i, 0))],
            out_specs=pl.BlockSpec(memory_space=pl.ANY),
            scratch_shapes=[pltpu.VMEM((MOE_TILE, w), h2p.dtype),
                            pltpu.SemaphoreType.DMA(()),
                            pltpu.SemaphoreType.DMA(())]),
        out_shape=jax.ShapeDtypeStruct((n_rows, w), h2p.dtype),
        compiler_params=_cparams("arbitrary"),
        name="moe_dispatch",
    )(tile_expert, n_tiles, dest3, h2p)


def _expert_kernel(te_ref, nt_ref, hs_ref, wg_hbm, wu_hbm, wd_hbm, ys_ref,
                   wg_buf, wu_buf, wd_buf, sem, slot_ref):
    t = pl.program_id(0)
    nt = nt_ref[0]
    used = t < nt
    last = te_ref.shape[0] - 1

    def weight_copies(e, slot):
        return (pltpu.make_async_copy(wg_hbm.at[e], wg_buf.at[slot], sem.at[0, slot]),
                pltpu.make_async_copy(wu_hbm.at[e], wu_buf.at[slot], sem.at[1, slot]),
                pltpu.make_async_copy(wd_hbm.at[e], wd_buf.at[slot], sem.at[2, slot]))

    @pl.when(t == 0)
    def _():
        slot_ref[0] = 0
        for c in weight_copies(te_ref[0], 0):
            c.start()

    @pl.when(jnp.logical_not(used))
    def _():
        ys_ref[...] = jnp.zeros_like(ys_ref)

    @pl.when(used)
    def _():
        e = te_ref[t]
        first = (t == 0) | (te_ref[jnp.maximum(t - 1, 0)] != e)

        @pl.when(first & (t > 0))
        def _():
            slot_ref[0] = 1 - slot_ref[0]

        slot = slot_ref[0]

        @pl.when(first)
        def _():
            for c in weight_copies(e, slot):
                c.wait()
            nxt = lax.while_loop(lambda s: (s < nt) & (te_ref[jnp.minimum(s, last)] == e),
                                 lambda s: s + 1, t + 1)

            @pl.when(nxt < nt)
            def _():
                for c in weight_copies(te_ref[jnp.minimum(nxt, last)], 1 - slot):
                    c.start()

        lo, hi = _unpack_bf16_pair(hs_ref[...])
        k = lo.shape[1]

        def up(w_buf):
            return (jnp.dot(lo, w_buf[slot, :k, :].astype(BF16), preferred_element_type=F32)
                    + jnp.dot(hi, w_buf[slot, k:, :].astype(BF16), preferred_element_type=F32))

        a = up(wg_buf)
        u = up(wu_buf)
        act = (a * _sigmoid(a) * u).astype(BF16)
        ys_ref[...] = jnp.dot(act, wd_buf[slot].astype(BF16), preferred_element_type=F32)


def _experts(tile_expert, n_tiles, hs, wg, wu, wd):
    n_rows, hw = hs.shape
    d, f = wg.shape[1], wg.shape[2]
    row = lambda t, te, nt: (jnp.minimum(t, nt[0] - 1), 0)
    hbm = pl.BlockSpec(memory_space=pl.ANY)
    return pl.pallas_call(
        _expert_kernel,
        grid_spec=pltpu.PrefetchScalarGridSpec(
            num_scalar_prefetch=2,
            grid=(n_rows // MOE_TILE,),
            in_specs=[pl.BlockSpec((MOE_TILE, hw), row), hbm, hbm, hbm],
            out_specs=pl.BlockSpec((MOE_TILE, d), lambda t, te, nt: (t, 0)),
            scratch_shapes=[pltpu.VMEM((2, d, f), wg.dtype),
                            pltpu.VMEM((2, d, f), wu.dtype),
                            pltpu.VMEM((2, f, d), wd.dtype),
                            pltpu.SemaphoreType.DMA((3, 2)),
                            pltpu.SMEM((1,), I32)]),
        out_shape=jax.ShapeDtypeStruct((n_rows, d), F32),
        compiler_params=_cparams("arbitrary"),
        name="moe_experts",
    )(tile_expert, n_tiles, hs, wg, wu, wd)


def _combine_kernel(dest_ref, dest_next_ref, ys_ref, x1_ref, cw_ref, gt_ref, o_ref, buf, sem):
    s, n_steps = pl.program_id(0), pl.num_programs(0)
    tq = x1_ref.shape[1]

    def request(d_ref, slot):
        def issue(r, c):
            for k in range(MOE_SLOTS):
                row = d_ref[0, 0, MOE_SLOTS * r + k]
                pltpu.make_async_copy(ys_ref.at[pl.ds(row, 1)], buf.at[slot, k, pl.ds(r, 1)],
                                      sem.at[slot]).start(priority=k)
            return c

        lax.fori_loop(0, tq, issue, 0, unroll=8)

    @pl.when(s == 0)
    def _():
        request(dest_ref, 0)

    @pl.when(s + 1 < n_steps)
    def _():
        request(dest_next_ref, 1 - s % 2)

    slot = s % 2
    pltpu.make_async_copy(buf.at[slot], buf.at[slot], sem.at[slot]).wait()
    cw = cw_ref[0]
    y = cw[:, 0:1] * buf[slot, 0] + cw[:, 1:2] * buf[slot, 1]
    o_ref[0] = x1_ref[0] + gt_ref[0] * y


def _combine(dest3, ys, x1, cw, gt2):
    b, s, d = x1.shape
    tq = TOKEN_TILE
    spb = s // tq
    n_steps = b * spb
    row = lambda t: (t // spb, t % spb, 0)
    idx = lambda t: (t, 0, 0)
    idx_next = lambda t: (jnp.minimum(t + 1, n_steps - 1), 0, 0)
    return pl.pallas_call(
        _combine_kernel,
        grid=(n_steps,),
        in_specs=[pl.BlockSpec((1, 1, MOE_SLOTS * tq), idx, memory_space=pltpu.SMEM),
                  pl.BlockSpec((1, 1, MOE_SLOTS * tq), idx_next, memory_space=pltpu.SMEM),
                  pl.BlockSpec(memory_space=pl.ANY),
                  pl.BlockSpec((1, tq, d), row),
                  pl.BlockSpec((1, tq, LANES), row),
                  pl.BlockSpec((1, 1, d), lambda t: (t // spb, 0, 0))],
        out_specs=pl.BlockSpec((1, tq, d), row),
        out_shape=jax.ShapeDtypeStruct((b, s, d), F32),
        scratch_shapes=[pltpu.VMEM((2, MOE_SLOTS, tq, d), F32), pltpu.SemaphoreType.DMA((2,))],
        compiler_params=_cparams("arbitrary"),
        name="moe_combine",
    )(dest3, dest3, ys, x1, cw, gt2)


def _moe(h2p, ids, cw, wg, wu, wd, x1, gt2):
    b, s, d = x1.shape
    n = b * s
    n_rows = _moe_max_tiles(n) * MOE_TILE
    dest, te = _plan(ids.reshape(n, LANES))
    dest3 = dest[:, :MOE_SLOTS].reshape(n // TOKEN_TILE, 1, MOE_SLOTS * TOKEN_TILE)
    tile_expert, n_tiles = te[:n_rows // MOE_TILE, 0], te[0, 1:2]
    hs = _dispatch(tile_expert, n_tiles, dest3, h2p.reshape(n, d // 2), n_rows)
    ys = _experts(tile_expert, n_tiles, hs, wg, wu, wd)
    return _combine(dest3, ys, x1, cw, gt2)


def _alibi_slopes():
    n = SWA_HEADS + DSA_HEADS
    i = jnp.arange(1, n + 1, dtype=F32)
    return jnp.exp2(-8.0 * i / n)


def _layer(x, mod, ln1_g, w_in, swa_q_norm, swa_k_norm, swa_sinks, dsa_q_norm, dsa_k_norm,
           w_branch_a, w_branch_b, w_out, ln2_g, w_group, b_group, w_expert, b_expert,
           w_gate_up, w_up, w_down, slopes):
    b, s, d = x.shape
    sh1, sc1, gt1, sh2, sc2, gt2 = [m[:, None, :] for m in jnp.split(mod, 6, axis=-1)]

    w_in_t = jnp.swapaxes(w_in, 0, 1)

    h1 = _ln_mod(x, ln1_g[None, :], sc1, sh1)
    h1f = h1.reshape(b * s, d)
    ones =lambda n: jnp.ones((n,), F32)
    mult = jnp.concatenate([
        jnp.tile(swa_q_norm, SWA_HEADS) * HEAD_DIM ** -0.5, jnp.tile(swa_k_norm, SWA_KV_HEADS),
        ones(SWA_KV_HEADS * HEAD_DIM),
        jnp.tile(dsa_q_norm, DSA_HEADS) * HEAD_DIM ** -0.5, dsa_k_norm, ones(HEAD_DIM),
        ones(IDX_HEADS * IDX_DIM) * IDX_DIM ** -0.5, ones(IDX_DIM), ones(IDX_HEADS) * IDX_HEADS ** -0.5,
        ones(ATT_WIDTH - COL_GATE)])[None, :]
    col = jnp.arange(ATT_WIDTH)
    normed = (((col >= COL_QA) & (col < COL_VA)) | ((col >= COL_QB) & (col < COL_KVB + KVB_V))
              ).astype(F32)[None, :]
    proj = _proj(h1f, w_in_t, mult, normed, 1024, 768).reshape(b, s, ATT_WIDTH)
    oa = _swa(proj, swa_sinks, slopes[:SWA_HEADS])
    ob = _dsa(proj, slopes[SWA_HEADS:])
    merged = _merge(h1f, oa.reshape(b * s, -1), ob.reshape(b * s, -1), w_in_t,
                    w_branch_a, w_branch_b)

    w_r = jnp.concatenate([w_expert, w_group,
                           jnp.zeros((d, LANES - N_EXPERTS - N_GROUPS), F32)], axis=1)
    b_r = jnp.concatenate([b_expert, b_group,
                           jnp.zeros((LANES - N_EXPERTS - N_GROUPS,), F32)])[None, :]
    w_r_hi = _bf16_head(w_r)
    w_r_split = jnp.concatenate([w_r_hi.astype(BF16), (w_r - w_r_hi).astype(BF16)], axis=1)
    x1, h2p, ids, cw = _outproj(merged.reshape(b, s, d), w_out, x, gt1,
                                ln2_g[None, :], sc2, sh2, w_r_split, b_r)
    f = w_gate_up.shape[-1]
    return _moe(h2p, ids, cw, w_gate_up.reshape(N_EXPERTS, d, f), w_up.reshape(N_EXPERTS, d, f),
                w_down.reshape(N_EXPERTS, f, d), x1, gt2)


def kernel(x, c, w_ada, b_ada, ln1_g, w_in, swa_q_norm, swa_k_norm, swa_sinks, dsa_q_norm, dsa_k_norm,
           w_branch_a, w_branch_b, w_out, ln2_g, w_group, b_group, w_expert, b_expert,
           w_gate_up, w_up, w_down):
    slopes = _alibi_slopes()
    bsz = c.shape[0]
    c_pad = jnp.concatenate([c, jnp.zeros((ADA_ROWS - bsz, c.shape[1]), c.dtype)], axis=0)
    for l in range(w_ada.shape[0]):
        mod = _ada(c_pad, w_ada[l], b_ada[l][None, :])[:bsz]
        x = _layer(x, mod, ln1_g[l], w_in[l], swa_q_norm[l], swa_k_norm[l], swa_sinks[l],
                   dsa_q_norm[l], dsa_k_norm[l], w_branch_a[l], w_branch_b[l], w_out[l], ln2_g[l],
                   w_group[l], b_group[l], w_expert[l], b_expert[l],
                   w_gate_up[l], w_up[l], w_down[l], slopes)
    return x
```
